```python
import jax
import jax.numpy as jnp
from jax import lax
import numpy as np

D_MODEL = 4096
BATCH = 8
SEQ = 2048
DEPTH = 1
DEC_BATCH = 8
DEC_SEQ = 32
PAST_LEN = 1024

CHUNK = 64
M_HEADS = 8
M_DQK = 128
M_DV = 256
A_HEADS = 32
A_KV_HEADS = 4
A_GROUP = A_HEADS // A_KV_HEADS
A_HD = 64
WINDOW = 128
WIN_CHUNKS = WINDOW // CHUNK
ROPE_THETA = 10000.0
D_FF = 4 * D_MODEL
N_BRANCH = 2
ALPHA = (2.0 * DEPTH) ** 0.25
BETA = (8.0 * DEPTH) ** -0.25
LN_EPS = 1e-5
RMS_EPS = 1e-6

M_QK_W = M_HEADS * M_DQK
M_V_W = M_HEADS * M_DV
A_Q_W = A_HEADS * A_HD
A_KV_W = A_KV_HEADS * A_HD
IN_SPLITS = (M_QK_W, M_QK_W, M_V_W, M_V_W, M_HEADS, M_HEADS, A_Q_W, A_KV_W, A_KV_W, N_BRANCH * D_MODEL)
D_IN = sum(IN_SPLITS)

kernel_name = 'hybrid_mlstm_swa_sink_stream_step'


def _layer_norm(x, g, b):
    xf = x.astype(jnp.float32)
    mu = jnp.mean(xf, -1, keepdims=True)
    var = jnp.mean(jnp.square(xf - mu), -1, keepdims=True)
    return ((xf - mu) * lax.rsqrt(var + LN_EPS) * g + b).astype(x.dtype)


def _split_in(z):
    parts, off = [], 0
    for n in IN_SPLITS:
        parts.append(z[..., off:off + n])
        off += n
    return parts


def _rope(x, pos):
    half = A_HD // 2
    inv = ROPE_THETA ** (-jnp.arange(half, dtype=jnp.float32) / half)
    ang = pos.astype(jnp.float32)[:, None] * inv[None, :]
    cos = jnp.cos(ang)[:, None, :]
    sin = jnp.sin(ang)[:, None, :]
    xf = x.astype(jnp.float32)
    x1, x2 = xf[..., :half], xf[..., half:]
    return jnp.concatenate([x1 * cos - x2 * sin, x2 * cos + x1 * sin], -1).astype(x.dtype)


def _mlstm_prep(q, k, v, i_pre, f_pre, b_ig, b_fg):
    B, S = q.shape[:2]
    heads = lambda t, d: t.reshape(B, S, M_HEADS, d).transpose(0, 2, 1, 3).astype(jnp.float32)
    qh = heads(q, M_DQK) * (M_DQK ** -0.5)
    kh = heads(k, M_DQK)
    vh = heads(v, M_DV)
    ig = (i_pre.astype(jnp.float32) + b_ig).transpose(0, 2, 1)
    lf = jax.nn.log_sigmoid(f_pre.astype(jnp.float32) + b_fg).transpose(0, 2, 1)
    return qh, kh, vh, ig, lf


def _mlstm_chunk(carry, inp):
    C, n_st, m = carry
    q, k, v, ig, lf = inp
    L = q.shape[2]
    b = jnp.cumsum(lf, axis=-1)
    causal = jnp.tril(jnp.ones((L, L), bool))
    dlog = jnp.where(causal, b[..., :, None] - b[..., None, :] + ig[..., None, :], -jnp.inf)
    inter = b + m[..., None]
    m_t = jnp.maximum(inter, jnp.max(dlog, -1))
    s = jnp.einsum('bhtd,bhsd->bhts', q, k) * jnp.exp(dlog - m_t[..., None])
    a = jnp.exp(inter - m_t)
    num = a[..., None] * jnp.einsum('bhvd,bhtd->bhtv', C, q) + jnp.einsum('bhts,bhsv->bhtv', s, v)
    den = a * jnp.einsum('bhd,bhtd->bht', n_st, q) + jnp.sum(s, -1)
    h = num / jnp.maximum(jnp.abs(den), jnp.exp(-m_t))[..., None]
    bL = b[..., -1]
    g_log = bL[..., None] - b + ig
    m_new = jnp.maximum(bL + m, jnp.max(g_log, -1))
    wk = jnp.exp(g_log - m_new[..., None])
    decay = jnp.exp(bL + m - m_new)
    C_new = decay[..., None, None] * C + jnp.einsum('bhs,bhsv,bhsd->bhvd', wk, v, k)
    n_new = decay[..., None] * n_st + jnp.einsum('bhs,bhsd->bhd', wk, k)
    return (C_new, n_new, m_new), h


def _mlstm_prompt(qh, kh, vh, ig, lf):
    B, H, S, _ = qh.shape
    nC = S // CHUNK
    blk = lambda t: jnp.moveaxis(t.reshape(B, H, nC, CHUNK, *t.shape[3:]), 2, 0)
    C0 = jnp.zeros((B, H, M_DV, M_DQK), jnp.float32)
    n0 = jnp.zeros((B, H, M_DQK), jnp.float32)
    m0 = jnp.zeros((B, H), jnp.float32)
    (C, n_st, m), hs = lax.scan(_mlstm_chunk, (C0, n0, m0), (blk(qh), blk(kh), blk(vh), blk(ig), blk(lf)))
    h = jnp.moveaxis(hs, 0, 2).reshape(B, H, S, M_DV)
    return h, C, n_st, m


def _mlstm_out(h, o_pre, w_mnorm, dtype):
    B, _, S, _ = h.shape
    hn = h * lax.rsqrt(jnp.mean(jnp.square(h), -1, keepdims=True) + RMS_EPS)
    hn = hn.transpose(0, 2, 1, 3).reshape(B, S, M_V_W) * w_mnorm
    return (jax.nn.sigmoid(o_pre.astype(jnp.float32)) * hn).astype(dtype)


def _sink_probs(s, sink):
    sk = sink.astype(jnp.float32).reshape(A_KV_HEADS, A_GROUP, 1)
    mx = jnp.maximum(jnp.max(s, -1), sk)
    p = jnp.exp(s - mx[..., None])
    return p / (jnp.sum(p, -1) + jnp.exp(sk - mx))[..., None]


def _swa_prompt(q, k, v, sink):
    B, S = q.shape[:2]
    nC = S // CHUNK
    qb = q.reshape(B, nC, CHUNK, A_KV_HEADS, A_GROUP, A_HD)

    def bands(t):
        tc = t.reshape(B, nC, CHUNK, A_KV_HEADS, A_HD)
        tp = jnp.concatenate([jnp.zeros((B, WIN_CHUNKS, CHUNK, A_KV_HEADS, A_HD), t.dtype), tc], axis=1)
        return jnp.concatenate([tp[:, j:j + nC] for j in range(WIN_CHUNKS + 1)], axis=2)

    kb, vb = bands(k), bands(v)
    src_chunk = jnp.arange(nC)[:, None] + jnp.arange(WIN_CHUNKS + 1)[None, :] - WIN_CHUNKS
    valid = jnp.repeat(src_chunk >= 0, CHUNK, axis=1)
    s = jnp.einsum('bcqkgd,bcjkd->bckgqj', qb, kb, preferred_element_type=jnp.float32) * (A_HD ** -0.5)
    s = jnp.where(valid[None, :, None, None, None, :], s, -jnp.inf)
    p = _sink_probs(s, sink)
    o = jnp.einsum('bckgqj,bcjkd->bcqkgd', p.astype(v.dtype), vb)
    return o.reshape(B, S, A_Q_W)


def _swa_sample(q, k_all, v_all, sink):
    B, T = q.shape[:2]
    qg = q.reshape(B, T, A_KV_HEADS, A_GROUP, A_HD)
    s = jnp.einsum('btkgd,bjkd->bkgtj', qg, k_all, preferred_element_type=jnp.float32) * (A_HD ** -0.5)
    p = _sink_probs(s, sink)
    o = jnp.einsum('bkgtj,bjkd->btkgd', p.astype(v_all.dtype), v_all)
    return o.reshape(B, T, A_Q_W)


def _mixer_inputs(x, pos, w_in, b_ig, b_fg):
    B, S, _ = x.shape
    z = jnp.einsum('bsd,de->bse', x, w_in)
    mq, mk, mv, mo, mi, mf, aq, ak, av, gp = _split_in(z)
    mlstm = _mlstm_prep(mq, mk, mv, mi, mf, b_ig, b_fg)
    aq = _rope(aq.reshape(B, S, A_HEADS, A_HD), pos)
    ak = _rope(ak.reshape(B, S, A_KV_HEADS, A_HD), pos)
    av = av.reshape(B, S, A_KV_HEADS, A_HD)
    return mlstm, mo, aq, ak, av, gp


def _finish(x, ya, yb, gp, w_br_a, w_br_b, w_out, ln1_g, ln1_b, w_up, w_down, ln2_g, ln2_b):
    B, S, _ = x.shape
    g = jax.nn.sigmoid(gp.astype(jnp.float32)).reshape(B, S, N_BRANCH, D_MODEL).astype(x.dtype)
    merged = g[:, :, 0] * (ya @ w_br_a) + g[:, :, 1] * (yb @ w_br_b)
    h = _layer_norm(ALPHA * x + merged @ w_out, ln1_g, ln1_b)
    f = jnp.square(jax.nn.relu(h @ w_up)) @ w_down
    return _layer_norm(ALPHA * h + f, ln2_g, ln2_b)


def setup_inputs(seed: int = 0) -> dict:
    key = jax.random.key(seed)
    ks = jax.random.split(key, 24)
    nrm = lambda k, shape, scale: scale * jax.random.normal(k, shape, jnp.float32)
    L = DEPTH
    return {
        'x_prompt': nrm(ks[0], (BATCH, SEQ, D_MODEL), 1.0),
        'x_sample': nrm(ks[1], (DEC_BATCH, DEC_SEQ, D_MODEL), 1.0),
        'cache_swa_k': nrm(ks[2], (L, DEC_BATCH, WINDOW, A_KV_HEADS, A_HD), 1.0),
        'cache_swa_v': nrm(ks[3], (L, DEC_BATCH, WINDOW, A_KV_HEADS, A_HD), 1.0),
        'state_mlstm_C': nrm(ks[4], (L, DEC_BATCH, M_HEADS, M_DV, M_DQK), 0.3),
        'state_mlstm_n': nrm(ks[5], (L, DEC_BATCH, M_HEADS, M_DQK), 0.3),
        'state_mlstm_m': nrm(ks[6], (L, DEC_BATCH, M_HEADS), 0.5),
        'w_in': nrm(ks[7], (L, D_MODEL, D_IN), D_MODEL ** -0.5),
        'b_igate': nrm(ks[8], (L, M_HEADS), 0.1),
        'b_fgate': jnp.linspace(3.0, 6.0, M_HEADS)[None, :] + nrm(ks[9], (L, M_HEADS), 0.1),
        'w_mnorm': 1.0 + nrm(ks[10], (L, M_V_W), 0.05),
        'attn_sink': nrm(ks[11], (L, A_HEADS), 0.5),
        'w_branch_a': nrm(ks[12], (L, M_V_W, D_MODEL), BETA * M_V_W ** -0.5),
        'w_branch_b': nrm(ks[13], (L, A_Q_W, D_MODEL), BETA * A_Q_W ** -0.5),
        'w_out': nrm(ks[14], (L, D_MODEL, D_MODEL), BETA * D_MODEL ** -0.5),
        'ln1_g': 1.0 + nrm(ks[15], (L, D_MODEL), 0.05),
        'ln1_b': nrm(ks[16], (L, D_MODEL), 0.02),
        'w_up': nrm(ks[17], (L, D_MODEL, D_FF), D_MODEL ** -0.5),
        'w_down': nrm(ks[18], (L, D_FF, D_MODEL), BETA * D_FF ** -0.5),
        'ln2_g': 1.0 + nrm(ks[19], (L, D_MODEL), 0.05),
        'ln2_b': nrm(ks[20], (L, D_MODEL), 0.02),
    }


def reference(x_prompt, x_sample, cache_swa_k, cache_swa_v, state_mlstm_C, state_mlstm_n, state_mlstm_m,
              w_in, b_igate, b_fgate, w_mnorm, attn_sink, w_branch_a, w_branch_b, w_out,
              ln1_g, ln1_b, w_up, w_down, ln2_g, ln2_b):
    S = x_prompt.shape[1]
    T = x_sample.shape[1]
    pos_p = jnp.arange(S, dtype=jnp.int32)
    pos_s = PAST_LEN + jnp.arange(T, dtype=jnp.int32)
    xp, xs = x_prompt, x_sample
    pk, pv, pC, pn, pm = [], [], [], [], []
    sk, sv, sC, sn, sm = [], [], [], [], []
    for l in range(DEPTH):
        (qh, kh, vh, ig, lf), mo, aq, ak, av, gp = _mixer_inputs(xp, pos_p, w_in[l], b_igate[l], b_fgate[l])
        h, C, n_st, m = _mlstm_prompt(qh, kh, vh, ig, lf)
        ya = _mlstm_out(h, mo, w_mnorm[l], xp.dtype)
        yb = _swa_prompt(aq, ak, av, attn_sink[l])
        pk.append(ak[:, -WINDOW:])
        pv.append(av[:, -WINDOW:])
        pC.append(C)
        pn.append(n_st)
        pm.append(m)
        xp = _finish(xp, ya, yb, gp, w_branch_a[l], w_branch_b[l], w_out[l],
                     ln1_g[l], ln1_b[l], w_up[l], w_down[l], ln2_g[l], ln2_b[l])
        (qs, kss, vs, igs, lfs), mo_s, aq_s, ak_s, av_s, gp_s = _mixer_inputs(xs, pos_s, w_in[l], b_igate[l], b_fgate[l])
        carry0 = (state_mlstm_C[l].astype(jnp.float32), state_mlstm_n[l].astype(jnp.float32),
                  state_mlstm_m[l].astype(jnp.float32))
        (C_s, n_s, m_s), h_s = _mlstm_chunk(carry0, (qs, kss, vs, igs, lfs))
        ya_s = _mlstm_out(h_s, mo_s, w_mnorm[l], xs.dtype)
        k_all = jnp.concatenate([cache_swa_k[l].astype(ak_s.dtype), ak_s], axis=1)
        v_all = jnp.concatenate([cache_swa_v[l].astype(av_s.dtype), av_s], axis=1)
        yb_s = _swa_sample(aq_s, k_all, v_all, attn_sink[l])
        sk.append(ak_s)
        sv.append(av_s)
        sC.append(C_s)
        sn.append(n_s)
        sm.append(m_s)
        xs = _finish(xs, ya_s, yb_s, gp_s, w_branch_a[l], w_branch_b[l], w_out[l],
                     ln1_g[l], ln1_b[l], w_up[l], w_down[l], ln2_g[l], ln2_b[l])
    return (xp, xs, jnp.stack(pk), jnp.stack(pv), jnp.stack(pC), jnp.stack(pn), jnp.stack(pm),
            jnp.stack(sk), jnp.stack(sv), jnp.stack(sC), jnp.stack(sn), jnp.stack(sm))
```

```python
import functools

import jax
import jax.numpy as jnp
from jax import lax
from jax.experimental import pallas as pl
from jax.experimental.pallas import tpu as pltpu

CHUNK = 64
M_HEADS = 8
M_DQK = 128
M_DV = 256
A_HEADS = 32
A_KV_HEADS = 4
A_GROUP = A_HEADS // A_KV_HEADS
A_HD = 64
WINDOW = 128
WIN_CHUNKS = WINDOW // CHUNK
ROPE_THETA = 10000.0
PAST_LEN = 1024
N_BRANCH = 2
LN_EPS = 1e-5
RMS_EPS = 1e-6

M_QK_W = M_HEADS * M_DQK
M_V_W = M_HEADS * M_DV
A_Q_W = A_HEADS * A_HD
A_KV_W = A_KV_HEADS * A_HD

V7X_VMEM_BYTES = 64 * 1024 * 1024
V7X_LANES = 128
V7X_SUBLANES = 8
VMEM_REQUEST_CAP = (V7X_VMEM_BYTES * 7) // 8

F32 = jnp.float32
BF16 = jnp.bfloat16


def _nbytes(shape, dtype):
    n = 1
    for s in shape:
        n *= s
    return n * jnp.dtype(dtype).itemsize


def _vmem_limit(pipelined, resident=0):
    return int(min(VMEM_REQUEST_CAP, 2 * sum(pipelined) + resident + (4 << 20)))


def _matmul(x, w, *, bm, bn, bk, epilogue, out_dtypes, extras=(), name):
    M, K = x.shape
    N = w.shape[1]
    assert M % bm == 0 and N % bn == 0 and K % bk == 0, (M, N, K, bm, bn, bk)
    nm, nn, nk = M // bm, N // bn, K // bk
    n_ex, n_out = len(extras), len(out_dtypes)

    def body(*refs):
        x_ref, w_ref = refs[0], refs[1]
        ex_refs = refs[2:2 + n_ex]
        out_refs = refs[2 + n_ex:2 + n_ex + n_out]
        part = jnp.dot(x_ref[...], w_ref[...], preferred_element_type=F32)

        def finish(acc):
            outs = epilogue(acc, *[r[...] for r in ex_refs])
            for o_ref, o in zip(out_refs, outs):
                o_ref[...] = o.astype(o_ref.dtype)

        if nk == 1:
            finish(part)
        else:
            acc_ref = refs[-1]
            k = pl.program_id(2)

            @pl.when(k == 0)
            def _():
                acc_ref[...] = part

            @pl.when(jnp.logical_and(k > 0, k < nk - 1))
            def _():
                acc_ref[...] += part

            @pl.when(k == nk - 1)
            def _():
                finish(acc_ref[...] + part)

    in_specs = [pl.BlockSpec((bm, bk), lambda i, j, k: (i, k)),
                pl.BlockSpec((bk, bn), lambda i, j, k: (k, j))]
    in_specs += [pl.BlockSpec(bs, im) for (_, bs, im) in extras]
    out_specs = [pl.BlockSpec((bm, bn), lambda i, j, k: (i, j)) for _ in out_dtypes]
    out_shape = [jax.ShapeDtypeStruct((M, N), dt) for dt in out_dtypes]
    scratch = [pltpu.VMEM((bm, bn), F32)] if nk > 1 else []
    pipelined = [_nbytes((bm, bk), x.dtype), _nbytes((bk, bn), w.dtype)]
    pipelined += [_nbytes(bs, a.dtype) for (a, bs, _) in extras]
    pipelined += [_nbytes((bm, bn), dt) for dt in out_dtypes]
    resident = _nbytes((bm, bn), F32) * (4 if nk > 1 else 3)
    outs = pl.pallas_call(
        body,
        grid=(nm, nn, nk),
        in_specs=in_specs,
        out_specs=out_specs,
        out_shape=out_shape,
        scratch_shapes=scratch,
        compiler_params=pltpu.CompilerParams(
            dimension_semantics=("parallel", "parallel", "arbitrary"),
            vmem_limit_bytes=_vmem_limit(pipelined, resident)),
        name=name,
    )(x, w, *[a for (a, _, _) in extras])
    return outs


def _ep_colscale(acc, scale_row):
    return (acc * scale_row,)


def _ep_sigmoid(acc):
    return (jax.nn.sigmoid(acc),)


def _ep_identity(acc):
    return (acc,)


def _ep_relu_sq(acc):
    r = jnp.maximum(acc, 0.0)
    return (r * r,)


def _ep_rope(acc, cos, sin_signed):
    width = acc.shape[1]
    half = A_HD // 2
    lane = lax.broadcasted_iota(jnp.int32, acc.shape, 1)
    first_half = (lane % A_HD) < half
    partner = jnp.where(first_half, pltpu.roll(acc, width - half, 1), pltpu.roll(acc, half, 1))
    return (acc * cos + partner * sin_signed,)


def _ep_residual(alpha, acc, res):
    return (alpha * res + acc,)


def _merge(ya, yb, wa, wb, gates, g0_col, g1_col, *, bm, bn):
    M, Ka = ya.shape
    Kb = yb.shape[1]
    N = wa.shape[1]
    assert M % bm == 0 and N % bn == 0 and g0_col % bn == 0 and g1_col % bn == 0
    o0, o1 = g0_col // bn, g1_col // bn

    def body(ya_ref, yb_ref, wa_ref, wb_ref, g0_ref, g1_ref, o_ref):
        a = jnp.dot(ya_ref[...], wa_ref[...], preferred_element_type=F32)
        b = jnp.dot(yb_ref[...], wb_ref[...], preferred_element_type=F32)
        o_ref[...] = (g0_ref[...].astype(F32) * a + g1_ref[...].astype(F32) * b).astype(o_ref.dtype)

    pipelined = [_nbytes((bm, Ka), BF16), _nbytes((bm, Kb), BF16), _nbytes((Ka, bn), BF16),
                 _nbytes((Kb, bn), BF16), 3 * _nbytes((bm, bn), BF16)]
    return pl.pallas_call(
        body,
        grid=(M // bm, N // bn),
        in_specs=[pl.BlockSpec((bm, Ka), lambda i, j: (i, 0)),
                  pl.BlockSpec((bm, Kb), lambda i, j: (i, 0)),
                  pl.BlockSpec((Ka, bn), lambda i, j: (0, j)),
                  pl.BlockSpec((Kb, bn), lambda i, j: (0, j)),
                  pl.BlockSpec((bm, bn), lambda i, j: (i, o0 + j)),
                  pl.BlockSpec((bm, bn), lambda i, j: (i, o1 + j))],
        out_specs=pl.BlockSpec((bm, bn), lambda i, j: (i, j)),
        out_shape=jax.ShapeDtypeStruct((M, N), BF16),
        compiler_params=pltpu.CompilerParams(
            dimension_semantics=("parallel", "parallel"),
            vmem_limit_bytes=_vmem_limit(pipelined, 2 * _nbytes((bm, bn), F32))),
        name="branch_merge",
    )(ya, yb, wa, wb, gates, gates)


def _layer_norm(x, g, b, *, bm, out_dtypes):
    M, D = x.shape
    assert M % bm == 0

    def body(x_ref, g_ref, b_ref, *o_refs):
        xf = x_ref[...]
        mu = jnp.mean(xf, axis=-1, keepdims=True)
        xc = xf - mu
        var = jnp.mean(xc * xc, axis=-1, keepdims=True)
        y = xc * lax.rsqrt(var + LN_EPS) * g_ref[...] + b_ref[...]
        for o_ref in o_refs:
            o_ref[...] = y.astype(o_ref.dtype)

    pipelined = [_nbytes((bm, D), F32)] + [_nbytes((bm, D), dt) for dt in out_dtypes]
    return pl.pallas_call(
        body,
        grid=(M // bm,),
        in_specs=[pl.BlockSpec((bm, D), lambda i: (i, 0)),
                  pl.BlockSpec((1, D), lambda i: (0, 0)),
                  pl.BlockSpec((1, D), lambda i: (0, 0))],
        out_specs=[pl.BlockSpec((bm, D), lambda i: (i, 0)) for _ in out_dtypes],
        out_shape=[jax.ShapeDtypeStruct((M, D), dt) for dt in out_dtypes],
        compiler_params=pltpu.CompilerParams(
            dimension_semantics=("parallel",),
            vmem_limit_bytes=_vmem_limit(pipelined, 2 * _nbytes((bm, D), F32))),
        name="layer_norm",
    )(x, g.reshape(1, D), b.reshape(1, D))


def _mlstm(qkv, og, gates_t, b_ig, b_fg, w_mnorm, C0, n0, m0, *, L):
    B, S, _ = qkv.shape
    nC = S // L
    nCp = gates_t.shape[2]
    H, dk, dv = M_HEADS, M_DQK, M_DV
    k_blk0 = M_QK_W // dk
    v_blk0 = (2 * M_QK_W) // dv

    def body(big_ref, bfg_ref, q_ref, k_ref, v_ref, og_ref, ig_ref, fg_ref, wm_ref, C0_ref, n0_ref, m0_ref,
             ya_ref, C_ref, n_ref, m_ref, b_s, ig_s):
        h = pl.program_id(1)
        ig_all = ig_ref[0, 0] + big_ref[h]
        lf_all = jax.nn.log_sigmoid(fg_ref[0, 0] + bfg_ref[h])
        r_i = lax.broadcasted_iota(jnp.int32, (L, L), 0)
        c_i = lax.broadcasted_iota(jnp.int32, (L, L), 1)
        tri_incl = (r_i <= c_i).astype(F32)
        b_s[...] = jnp.dot(lf_all, tri_incl, precision=lax.Precision.HIGHEST, preferred_element_type=F32)
        ig_s[...] = ig_all
        C_ref[0, 0] = C0_ref[0, 0]
        n_ref[0, 0] = n0_ref[0, 0]
        m_ref[0, 0] = m0_ref[0, 0]
        eye = r_i == c_i
        causal = c_i <= r_i
        wm = wm_ref[...]

        def col_of(row):
            return jnp.sum(jnp.where(eye, jnp.broadcast_to(row, (L, L)), 0.0), axis=1, keepdims=True)

        def chunk(c, carry):
            rows = pl.ds(pl.multiple_of(c * L, L), L)
            q = q_ref[0, rows, :]
            k = k_ref[0, rows, :]
            v = v_ref[0, rows, :]
            b_row = b_s[pl.ds(c, 1), :]
            ig_row = ig_s[pl.ds(c, 1), :]
            m_prev = m_ref[0, 0]
            C_prev = C_ref[0, 0]
            n_prev = n_ref[0, 0]

            b_col = col_of(b_row)
            dlog = b_col - jnp.broadcast_to(b_row, (L, L)) + jnp.broadcast_to(ig_row, (L, L))
            dlog = jnp.where(causal, dlog, -jnp.inf)
            inter = b_col + m_prev
            m_t = jnp.maximum(inter, jnp.max(dlog, axis=1, keepdims=True))
            qk = lax.dot_general(q, k, (((1,), (1,)), ((), ())), preferred_element_type=F32)
            s = qk * jnp.exp(dlog - m_t)
            a = jnp.exp(inter - m_t)
            qC = lax.dot_general(q, C_prev.astype(BF16), (((1,), (1,)), ((), ())),
                                 preferred_element_type=F32)
            sv = jnp.dot(s.astype(BF16), v, preferred_element_type=F32)
            num = a * qC + sv
            qn = jnp.sum(q.astype(F32) * n_prev, axis=1, keepdims=True)
            den = a * qn + jnp.sum(s, axis=1, keepdims=True)
            hid = num / jnp.maximum(jnp.abs(den), jnp.exp(-m_t))
            hn = hid * lax.rsqrt(jnp.mean(hid * hid, axis=1, keepdims=True) + RMS_EPS) * wm
            ya_ref[0, rows, :] = (og_ref[0, rows, :].astype(F32) * hn).astype(ya_ref.dtype)

            bL = b_row[:, L - 1:L]
            g_row = bL - b_row + ig_row
            m_new = jnp.maximum(bL + m_prev, jnp.max(g_row, axis=1, keepdims=True))
            wk_col = col_of(jnp.exp(g_row - m_new))
            decay = jnp.exp(bL + m_prev - m_new)
            kw = k.astype(F32) * wk_col
            vk = lax.dot_general(v, kw.astype(BF16), (((0,), (0,)), ((), ())),
                                 preferred_element_type=F32)
            C_ref[0, 0] = decay * C_prev + vk
            n_ref[0, 0] = decay * n_prev + jnp.sum(kw, axis=0, keepdims=True)
            m_ref[0, 0] = m_new
            return carry

        lax.fori_loop(0, nC, chunk, 0)

    smem = pl.BlockSpec(memory_space=pltpu.SMEM)
    pipelined = [2 * _nbytes((S, dk), BF16), 3 * _nbytes((S, dv), BF16), 2 * _nbytes((nCp, L), F32),
                 2 * _nbytes((dv, dk), F32)]
    ya, C, n, m = pl.pallas_call(
        body,
        grid=(B, H),
        in_specs=[smem, smem,
                  pl.BlockSpec((1, S, dk), lambda b, h: (b, 0, h)),
                  pl.BlockSpec((1, S, dk), lambda b, h: (b, 0, k_blk0 + h)),
                  pl.BlockSpec((1, S, dv), lambda b, h: (b, 0, v_blk0 + h)),
                  pl.BlockSpec((1, S, dv), lambda b, h: (b, 0, h)),
                  pl.BlockSpec((1, 1, nCp, L), lambda b, h: (b, h, 0, 0)),
                  pl.BlockSpec((1, 1, nCp, L), lambda b, h: (b, H + h, 0, 0)),
                  pl.BlockSpec((1, dv), lambda b, h: (0, h)),
                  pl.BlockSpec((1, 1, dv, dk), lambda b, h: (b, h, 0, 0)),
                  pl.BlockSpec((1, 1, 1, dk), lambda b, h: (b, h, 0, 0)),
                  pl.BlockSpec((1, 1, 1, 1), lambda b, h: (b, h, 0, 0))],
        out_specs=[pl.BlockSpec((1, S, dv), lambda b, h: (b, 0, h)),
                   pl.BlockSpec((1, 1, dv, dk), lambda b, h: (b, h, 0, 0)),
                   pl.BlockSpec((1, 1, 1, dk), lambda b, h: (b, h, 0, 0)),
                   pl.BlockSpec((1, 1, 1, 1), lambda b, h: (b, h, 0, 0))],
        out_shape=[jax.ShapeDtypeStruct((B, S, M_V_W), BF16),
                   jax.ShapeDtypeStruct((B, H, dv, dk), F32),
                   jax.ShapeDtypeStruct((B, H, 1, dk), F32),
                   jax.ShapeDtypeStruct((B, H, 1, 1), F32)],
        scratch_shapes=[pltpu.VMEM((nCp, L), F32), pltpu.VMEM((nCp, L), F32)],
        compiler_params=pltpu.CompilerParams(
            dimension_semantics=("parallel", "parallel"),
            vmem_limit_bytes=_vmem_limit(pipelined, 4 << 20)),
        name="mlstm_chunks",
    )(b_ig, b_fg, qkv, qkv, qkv, og, gates_t, gates_t, w_mnorm.reshape(1, M_V_W), C0, n0, m0)
    return ya, C, n, m


def _swa(q, kv, sink, *, tq, win, banded):
    B, S, _ = q.shape
    Skv = kv.shape[1]
    nq = S // tq
    G, KVH, LN = A_GROUP, A_KV_HEADS, V7X_LANES
    assert 2 * A_HD == LN and tq % 16 == 0

    def body(sink_ref, q_ref, kv_ref, o_ref):
        c = pl.program_id(1)
        if banded:
            start = pl.multiple_of(jnp.maximum(c - WIN_CHUNKS, 0) * CHUNK, CHUNK)
            n_valid = (jnp.minimum(c, WIN_CHUNKS) + 1) * CHUNK
            kvw = kv_ref[0, pl.ds(start, win), :]
        else:
            kvw = kv_ref[0]
        lane_q = lax.broadcasted_iota(jnp.int32, (tq, LN), 1)
        lane_k = lax.broadcasted_iota(jnp.int32, (win, LN), 1)
        lo_q = lane_q < A_HD
        lo_k = lane_k < A_HD
        zero = jnp.zeros((tq, LN), q_ref.dtype)
        for kvh in range(KVH):
            pair, odd = kvh // 2, kvh % 2
            kx = kvw[:, pair * LN:(pair + 1) * LN]
            vx = kvw[:, A_KV_W + pair * LN:A_KV_W + (pair + 1) * LN]
            kr = pltpu.roll(kx, A_HD, 1)
            vr = pltpu.roll(vx, A_HD, 1)
            k2 = (jnp.where(lo_k, kr, kx) if odd else jnp.where(lo_k, kx, kr)).astype(BF16)
            v2 = (jnp.where(lo_k, vr, vx) if odd else jnp.where(lo_k, vx, vr)).astype(BF16)
            pieces, sinks = [], []
            for g in range(G):
                head = kvh * G + g
                q2 = q_ref[0, :, (head // 2) * LN:(head // 2 + 1) * LN]
                pieces.append(jnp.where(lo_q, zero, q2) if head % 2 else jnp.where(lo_q, q2, zero))
                sinks.append(jnp.full((tq, 1), sink_ref[head], F32))
            qs = jnp.concatenate(pieces, axis=0)
            sk = jnp.concatenate(sinks, axis=0)
            s = lax.dot_general(qs, k2, (((1,), (1,)), ((), ())), preferred_element_type=F32)
            if banded:
                col = lax.broadcasted_iota(jnp.int32, s.shape, 1)
                s = jnp.where(col < n_valid, s, -jnp.inf)
            mx = jnp.maximum(jnp.max(s, axis=1, keepdims=True), sk)
            p = jnp.exp(s - mx)
            den = jnp.sum(p, axis=1, keepdims=True) + jnp.exp(sk - mx)
            o = jnp.dot(p.astype(BF16), v2, preferred_element_type=F32) / den
            for j in range(G // 2):
                pair_o = jnp.where(lo_q, o[(2 * j) * tq:(2 * j + 1) * tq], o[(2 * j + 1) * tq:(2 * j + 2) * tq])
                col0 = (kvh * G + 2 * j) * A_HD
                o_ref[0, :, col0:col0 + LN] = pair_o.astype(o_ref.dtype)

    pipelined = [2 * _nbytes((tq, A_Q_W), BF16), _nbytes((Skv, 2 * A_KV_W), F32)]
    return pl.pallas_call(
        body,
        grid=(B, nq),
        in_specs=[pl.BlockSpec(memory_space=pltpu.SMEM),
                  pl.BlockSpec((1, tq, A_Q_W), lambda b, c: (b, c, 0)),
                  pl.BlockSpec((1, Skv, 2 * A_KV_W), lambda b, c: (b, 0, 0))],
        out_specs=pl.BlockSpec((1, tq, A_Q_W), lambda b, c: (b, c, 0)),
        out_shape=jax.ShapeDtypeStruct((B, S, A_Q_W), BF16),
        compiler_params=pltpu.CompilerParams(
            dimension_semantics=("parallel", "arbitrary"),
            vmem_limit_bytes=_vmem_limit(pipelined, 8 << 20)),
        name="swa_sink_attention",
    )(sink, q, kv)


def _rope_tables(pos, n_rot_heads, n_plain_cols, scale):
    half = A_HD // 2
    inv = ROPE_THETA ** (-jnp.arange(half, dtype=F32) / half)
    ang = pos.astype(F32)[:, None] * inv[None, :]
    cos, sin = jnp.cos(ang) * scale, jnp.sin(ang) * scale
    cos_h = jnp.concatenate([cos, cos], -1)
    sin_h = jnp.concatenate([-sin, sin], -1)
    n = pos.shape[0]
    cos_t = jnp.concatenate([jnp.tile(cos_h, (1, n_rot_heads)), jnp.ones((n, n_plain_cols), F32)], -1)
    sin_t = jnp.concatenate([jnp.tile(sin_h, (1, n_rot_heads)), jnp.zeros((n, n_plain_cols), F32)], -1)
    return cos_t, sin_t


def _prep_weights(w_in, w_branch_a, w_branch_b, w_out, w_up, w_down):
    o_mq = 0
    o_mo = 2 * M_QK_W + M_V_W
    o_mi = o_mo + M_V_W
    o_aq = o_mi + 2 * M_HEADS
    o_ak = o_aq + A_Q_W
    o_gp = o_ak + 2 * A_KV_W
    d_model = w_in.shape[0]
    w_gate = jnp.pad(w_in[:, o_mi:o_aq], ((0, 0), (0, V7X_LANES - 2 * M_HEADS)))
    return dict(
        qkv=w_in[:, o_mq:o_mo].astype(BF16),
        sig=jnp.concatenate([w_in[:, o_mo:o_mi], w_in[:, o_gp:o_gp + N_BRANCH * d_model]], 1).astype(BF16),
        gate=w_gate.astype(BF16),
        aq=w_in[:, o_aq:o_ak].astype(BF16),
        kv=w_in[:, o_ak:o_gp].astype(BF16),
        br_a=w_branch_a.astype(BF16), br_b=w_branch_b.astype(BF16), out=w_out.astype(BF16),
        up=w_up.astype(BF16), down=w_down.astype(BF16))


def _layer(x, pos, W, b_ig, b_fg, w_mnorm, sink, ln1_g, ln1_b, ln2_g, ln2_b, alpha, state, cache, *, bm):
    B, S, D = x.shape
    M = B * S
    L = min(CHUNK, S)
    nC = S // L
    xf = x.reshape(M, D)
    xb = xf.astype(BF16)
    bn_big = min(1024, D)

    def pos_extra(table, width):
        if bm <= S:
            per = S // bm
            return (table, (bm, width), lambda i, j, k: (i % per, 0))
        return (jnp.tile(table, (bm // S, 1)), (bm, width), lambda i, j, k: (0, 0))

    colscale = jnp.concatenate([jnp.full((1, M_QK_W), M_DQK ** -0.5, F32),
                                jnp.ones((1, M_QK_W + M_V_W), F32)], 1)
    (qkv,) = _matmul(xb, W["qkv"], bm=bm, bn=bn_big, bk=D, epilogue=_ep_colscale, out_dtypes=[BF16],
                     extras=[(colscale, (1, bn_big), lambda i, j, k: (0, j))], name="in_proj_mlstm_qkv")
    (sig,) = _matmul(xb, W["sig"], bm=bm, bn=bn_big, bk=D, epilogue=_ep_sigmoid, out_dtypes=[BF16],
                     name="in_proj_sigmoid_gates")
    (gates,) = _matmul(xb, W["gate"], bm=bm, bn=V7X_LANES, bk=D, epilogue=_ep_identity, out_dtypes=[F32],
                       name="in_proj_if_gates")
    wq = 512
    cos_q, sin_q = _rope_tables(pos, wq // A_HD, 0, A_HD ** -0.5)
    (aq,) = _matmul(xb, W["aq"], bm=bm, bn=wq, bk=D, epilogue=_ep_rope, out_dtypes=[BF16],
                    extras=[pos_extra(cos_q, wq), pos_extra(sin_q, wq)], name="in_proj_attn_q")
    wkv = 2 * A_KV_W
    cos_k, sin_k = _rope_tables(pos, A_KV_HEADS, A_KV_W, 1.0)
    (kv_new,) = _matmul(xb, W["kv"], bm=bm, bn=wkv, bk=D, epilogue=_ep_rope, out_dtypes=[F32],
                        extras=[pos_extra(cos_k, wkv), pos_extra(sin_k, wkv)], name="in_proj_attn_kv")

    nCp = max(nC, V7X_SUBLANES)
    gates_t = gates[:, :2 * M_HEADS].reshape(B, nC, L, 2 * M_HEADS).transpose(0, 3, 1, 2)
    gates_t = jnp.pad(gates_t, ((0, 0), (0, 0), (0, nCp - nC), (0, 0)))
    if state is None:
        C0 = jnp.zeros((B, M_HEADS, M_DV, M_DQK), F32)
        n0 = jnp.zeros((B, M_HEADS, 1, M_DQK), F32)
        m0 = jnp.zeros((B, M_HEADS, 1, 1), F32)
    else:
        C0 = state[0].astype(F32)
        n0 = state[1].astype(F32).reshape(B, M_HEADS, 1, M_DQK)
        m0 = state[2].astype(F32).reshape(B, M_HEADS, 1, 1)
    ya, C, n_st, m = _mlstm(qkv.reshape(B, S, -1), sig.reshape(B, S, -1), gates_t, b_ig, b_fg, w_mnorm,
                            C0, n0, m0, L=L)

    kv3 = kv_new.reshape(B, S, wkv)
    k_new = kv3[:, :, :A_KV_W].reshape(B, S, A_KV_HEADS, A_HD)
    v_new = kv3[:, :, A_KV_W:].reshape(B, S, A_KV_HEADS, A_HD)
    if cache is None:
        yb = _swa(aq.reshape(B, S, A_Q_W), kv3, sink, tq=CHUNK, win=(WIN_CHUNKS + 1) * CHUNK, banded=True)
    else:
        ck = cache[0].astype(F32).reshape(B, WINDOW, A_KV_W)
        cv = cache[1].astype(F32).reshape(B, WINDOW, A_KV_W)
        kv_all = jnp.concatenate([jnp.concatenate([ck, cv], -1), kv3], axis=1)
        yb = _swa(aq.reshape(B, S, A_Q_W), kv_all, sink, tq=S, win=WINDOW + S, banded=False)

    merged = _merge(ya.reshape(M, M_V_W), yb.reshape(M, A_Q_W), W["br_a"], W["br_b"], sig,
                    M_V_W, M_V_W + D, bm=bm, bn=bn_big)
    res1 = (xf, (bm, bn_big), lambda i, j, k: (i, j))
    (h_pre,) = _matmul(merged, W["out"], bm=bm, bn=bn_big, bk=D, epilogue=functools.partial(_ep_residual, alpha),
                       out_dtypes=[F32], extras=[res1], name="out_proj_residual")
    bm_ln = min(256, M)
    h32, h16 = _layer_norm(h_pre, ln1_g, ln1_b, bm=bm_ln, out_dtypes=[F32, BF16])
    (act,) = _matmul(h16, W["up"], bm=bm, bn=bn_big, bk=D, epilogue=_ep_relu_sq, out_dtypes=[BF16],
                     name="mlp_up_relu_sq")
    res2 = (h32, (bm, bn_big), lambda i, j, k: (i, j))
    (y_pre,) = _matmul(act, W["down"], bm=bm, bn=bn_big, bk=min(2048, act.shape[1]),
                       epilogue=functools.partial(_ep_residual, alpha), out_dtypes=[F32], extras=[res2],
                       name="mlp_down_residual")
    (y,) = _layer_norm(y_pre, ln2_g, ln2_b, bm=bm_ln, out_dtypes=[F32])
    return (y.reshape(B, S, D), k_new, v_new, C, n_st.reshape(B, M_HEADS, M_DQK), m.reshape(B, M_HEADS))


def kernel(x_prompt, x_sample, cache_swa_k, cache_swa_v, state_mlstm_C, state_mlstm_n, state_mlstm_m,
           w_in, b_igate, b_fgate, w_mnorm, attn_sink, w_branch_a, w_branch_b, w_out,
           ln1_g, ln1_b, w_up, w_down, ln2_g, ln2_b):
    depth = w_in.shape[0]
    alpha = (2.0 * depth) ** 0.25
    S, T = x_prompt.shape[1], x_sample.shape[1]
    pos_p = jnp.arange(S, dtype=jnp.int32)
    pos_s = PAST_LEN + jnp.arange(T, dtype=jnp.int32)
    xp, xs = x_prompt, x_sample
    outs_p, outs_s = [], []
    for l in range(depth):
        W = _prep_weights(w_in[l], w_branch_a[l], w_branch_b[l], w_out[l], w_up[l], w_down[l])
        shared = (W, b_igate[l], b_fgate[l], w_mnorm[l], attn_sink[l], ln1_g[l], ln1_b[l], ln2_g[l], ln2_b[l], alpha)
        xp, *rest_p = _layer(xp, pos_p, *shared, None, None, bm=min(1024, xp.shape[0] * S))
        outs_p.append(rest_p)
        xs, *rest_s = _layer(xs, pos_s, *shared,
                             (state_mlstm_C[l], state_mlstm_n[l], state_mlstm_m[l]),
                             (cache_swa_k[l], cache_swa_v[l]), bm=xs.shape[0] * T)
        outs_s.append(rest_s)
    stack = lambda outs, i: jnp.stack([o[i] for o in outs])
    pk = jnp.stack([o[0][:, -WINDOW:] for o in outs_p])
    pv = jnp.stack([o[1][:, -WINDOW:] for o in outs_p])
    return (xp, xs, pk, pv, stack(outs_p, 2), stack(outs_p, 3), stack(outs_p, 4),
            stack(outs_s, 0), stack(outs_s, 1), stack(outs_s, 2), stack(outs_s, 3), stack(outs_s, 4))
```

```python
import functools

import jax
import jax.numpy as jnp
from jax import lax
from jax.experimental import pallas as pl
from jax.experimental.pallas import tpu as pltpu

CHUNK = 64
M_HEADS = 8
M_DQK = 128
M_DV = 256
A_HEADS = 32
A_KV_HEADS = 4
A_GROUP = A_HEADS // A_KV_HEADS
A_HD = 64
WINDOW = 128
WIN_CHUNKS = WINDOW // CHUNK
ROPE_THETA = 10000.0
PAST_LEN = 1024
N_BRANCH = 2
LN_EPS = 1e-5
RMS_EPS = 1e-6

M_QK_W = M_HEADS * M_DQK
M_V_W = M_HEADS * M_DV
A_Q_W = A_HEADS * A_HD
A_KV_W = A_KV_HEADS * A_HD

V7X_VMEM_BYTES = 64 * 1024 * 1024
V7X_LANES = 128
V7X_SUBLANES = 8
VMEM_REQUEST_CAP = (V7X_VMEM_BYTES * 7) // 8

F32 = jnp.float32
BF16 = jnp.bfloat16
NT_DIMS = (((1,), (1,)), ((), ()))
TN_DIMS = (((0,), (0,)), ((), ()))


def _nbytes(shape, dtype):
    n = 1
    for s in shape:
        n *= s
    return n * jnp.dtype(dtype).itemsize


def _vmem_limit(pipelined, resident=0):
    return int(min(VMEM_REQUEST_CAP, 2 * sum(pipelined) + resident + (4 << 20)))


def _matmul(x, w, *, bm, bn, bk, epilogue, out_dtypes, extras=(), w_col0=0, n_cols=None, name):
    M, K = x.shape
    N = w.shape[1] if n_cols is None else n_cols
    assert M % bm == 0 and N % bn == 0 and K % bk == 0 and w_col0 % bn == 0, (M, N, K, bm, bn, bk, w_col0)
    nm, nn, nk = M // bm, N // bn, K // bk
    j0 = w_col0 // bn
    n_ex, n_out = len(extras), len(out_dtypes)

    def body(*refs):
        x_ref, w_ref = refs[0], refs[1]
        ex_refs = refs[2:2 + n_ex]
        out_refs = refs[2 + n_ex:2 + n_ex + n_out]
        part = jnp.dot(x_ref[...], w_ref[...], preferred_element_type=F32)

        def finish(acc):
            outs = epilogue(acc, *[r[...] for r in ex_refs])
            for o_ref, o in zip(out_refs, outs):
                o_ref[...] = o.astype(o_ref.dtype)

        if nk == 1:
            finish(part)
        else:
            acc_ref = refs[-1]
            k = pl.program_id(2)

            @pl.when(k == 0)
            def _():
                acc_ref[...] = part

            @pl.when(jnp.logical_and(k > 0, k < nk - 1))
            def _():
                acc_ref[...] += part

            @pl.when(k == nk - 1)
            def _():
                finish(acc_ref[...] + part)

    in_specs = [pl.BlockSpec((bm, bk), lambda i, j, k: (i, k)),
                pl.BlockSpec((bk, bn), lambda i, j, k: (k, j0 + j))]
    in_specs += [pl.BlockSpec(bs, im) for (_, bs, im) in extras]
    out_specs = [pl.BlockSpec((bm, bn), lambda i, j, k: (i, j)) for _ in out_dtypes]
    out_shape = [jax.ShapeDtypeStruct((M, N), dt) for dt in out_dtypes]
    scratch = [pltpu.VMEM((bm, bn), F32)] if nk > 1 else []
    pipelined = [_nbytes((bm, bk), x.dtype), _nbytes((bk, bn), w.dtype)]
    pipelined += [_nbytes(bs, a.dtype) for (a, bs, _) in extras]
    pipelined += [_nbytes((bm, bn), dt) for dt in out_dtypes]
    resident = _nbytes((bm, bn), F32) * (4 if nk > 1 else 3)
    outs = pl.pallas_call(
        body,
        grid=(nm, nn, nk),
        in_specs=in_specs,
        out_specs=out_specs,
        out_shape=out_shape,
        scratch_shapes=scratch,
        compiler_params=pltpu.CompilerParams(
            dimension_semantics=("parallel", "parallel", "arbitrary"),
            vmem_limit_bytes=_vmem_limit(pipelined, resident)),
        name=name,
    )(x, w, *[a for (a, _, _) in extras])
    return outs


def _matmul_nt(wt, x, *, bm, bn, epilogue, out_dtype, extras=(), name):
    N, K = wt.shape
    M = x.shape[0]
    assert M % bm == 0 and N % bn == 0

    def body(w_ref, x_ref, *refs):
        acc = lax.dot_general(w_ref[...], x_ref[...], NT_DIMS, preferred_element_type=F32)
        (out,) = epilogue(acc, *[r[...] for r in refs[:-1]])
        refs[-1][...] = out.astype(refs[-1].dtype)

    pipelined = [_nbytes((bn, K), wt.dtype), _nbytes((bm, K), x.dtype), _nbytes((bn, bm), out_dtype)]
    pipelined += [_nbytes(bs, a.dtype) for (a, bs, _) in extras]
    return pl.pallas_call(
        body,
        grid=(M // bm, N // bn),
        in_specs=[pl.BlockSpec((bn, K), lambda i, j: (j, 0)),
                  pl.BlockSpec((bm, K), lambda i, j: (i, 0))] + [pl.BlockSpec(bs, im) for (_, bs, im) in extras],
        out_specs=pl.BlockSpec((bn, bm), lambda i, j: (j, i)),
        out_shape=jax.ShapeDtypeStruct((N, M), out_dtype),
        compiler_params=pltpu.CompilerParams(
            dimension_semantics=("parallel", "parallel"),
            vmem_limit_bytes=_vmem_limit(pipelined, 3 * _nbytes((bn, bm), F32))),
        name=name,
    )(wt, x, *[a for (a, _, _) in extras])


def _ep_colscale(acc, scale_row):
    return (acc * scale_row,)


def _ep_sigmoid(acc):
    return (jax.nn.sigmoid(acc),)


def _ep_identity(acc):
    return (acc,)


def _ep_relu_sq(acc):
    r = jnp.maximum(acc, 0.0)
    return (r * r,)


def _ep_rope(acc, cos, sin_signed):
    width = acc.shape[1]
    half = A_HD // 2
    lane = lax.broadcasted_iota(jnp.int32, acc.shape, 1)
    first_half = (lane % A_HD) < half
    partner = jnp.where(first_half, pltpu.roll(acc, width - half, 1), pltpu.roll(acc, half, 1))
    return (acc * cos + partner * sin_signed,)


def _ep_rope_t(acc, cos, sin_signed):
    rows = acc.shape[0]
    half = A_HD // 2
    row = lax.broadcasted_iota(jnp.int32, acc.shape, 0)
    first_half = (row % A_HD) < half
    partner = jnp.where(first_half, pltpu.roll(acc, rows - half, 0), pltpu.roll(acc, half, 0))
    return (acc * cos + partner * sin_signed,)


def _ep_residual(alpha, acc, res):
    return (alpha * res + acc,)


def _merge(ya, yb, wa, wb, gates, g0_col, g1_col, *, bm, bn):
    M, Ka = ya.shape
    Kb = yb.shape[1]
    N = wa.shape[1]
    assert M % bm == 0 and N % bn == 0 and g0_col % bn == 0 and g1_col % bn == 0
    o0, o1 = g0_col // bn, g1_col // bn

    def body(ya_ref, yb_ref, wa_ref, wb_ref, g0_ref, g1_ref, o_ref):
        a = jnp.dot(ya_ref[...], wa_ref[...], preferred_element_type=F32)
        b = jnp.dot(yb_ref[...], wb_ref[...], preferred_element_type=F32)
        o_ref[...] = (g0_ref[...].astype(F32) * a + g1_ref[...].astype(F32) * b).astype(o_ref.dtype)

    pipelined = [_nbytes((bm, Ka), BF16), _nbytes((bm, Kb), BF16), _nbytes((Ka, bn), BF16),
                 _nbytes((Kb, bn), BF16), 3 * _nbytes((bm, bn), BF16)]
    return pl.pallas_call(
        body,
        grid=(M // bm, N // bn),
        in_specs=[pl.BlockSpec((bm, Ka), lambda i, j: (i, 0)),
                  pl.BlockSpec((bm, Kb), lambda i, j: (i, 0)),
                  pl.BlockSpec((Ka, bn), lambda i, j: (0, j)),
                  pl.BlockSpec((Kb, bn), lambda i, j: (0, j)),
                  pl.BlockSpec((bm, bn), lambda i, j: (i, o0 + j)),
                  pl.BlockSpec((bm, bn), lambda i, j: (i, o1 + j))],
        out_specs=pl.BlockSpec((bm, bn), lambda i, j: (i, j)),
        out_shape=jax.ShapeDtypeStruct((M, N), BF16),
        compiler_params=pltpu.CompilerParams(
            dimension_semantics=("parallel", "parallel"),
            vmem_limit_bytes=_vmem_limit(pipelined, 2 * _nbytes((bm, bn), F32))),
        name="branch_merge",
    )(ya, yb, wa, wb, gates, gates)


def _layer_norm(x, g, b, *, bm, out_dtypes):
    M, D = x.shape
    assert M % bm == 0

    def body(x_ref, g_ref, b_ref, *o_refs):
        xf = x_ref[...]
        mu = jnp.mean(xf, axis=-1, keepdims=True)
        xc = xf - mu
        var = jnp.mean(xc * xc, axis=-1, keepdims=True)
        y = xc * lax.rsqrt(var + LN_EPS) * g_ref[...] + b_ref[...]
        for o_ref in o_refs:
            o_ref[...] = y.astype(o_ref.dtype)

    pipelined = [_nbytes((bm, D), F32)] + [_nbytes((bm, D), dt) for dt in out_dtypes]
    return pl.pallas_call(
        body,
        grid=(M // bm,),
        in_specs=[pl.BlockSpec((bm, D), lambda i: (i, 0)),
                  pl.BlockSpec((1, D), lambda i: (0, 0)),
                  pl.BlockSpec((1, D), lambda i: (0, 0))],
        out_specs=[pl.BlockSpec((bm, D), lambda i: (i, 0)) for _ in out_dtypes],
        out_shape=[jax.ShapeDtypeStruct((M, D), dt) for dt in out_dtypes],
        compiler_params=pltpu.CompilerParams(
            dimension_semantics=("parallel",),
            vmem_limit_bytes=_vmem_limit(pipelined, 2 * _nbytes((bm, D), F32))),
        name="layer_norm",
    )(x, g.reshape(1, D), b.reshape(1, D))


MLSTM_HEADS_PER_STEP = 4


def _mlstm(qkv, og, gates_t, b_ig, b_fg, w_mnorm, C0, n0, m0, *, L):
    B, S, _ = qkv.shape
    nC = S // L
    nCp = gates_t.shape[2]
    H, dk, dv = M_HEADS, M_DQK, M_DV
    hb = MLSTM_HEADS_PER_STEP
    ng = H // hb
    k_blk0 = M_QK_W // (hb * dk)
    v_blk0 = (2 * M_QK_W) // (hb * dv)

    def body(big_ref, bfg_ref, q_ref, k_ref, v_ref, og_ref, ig_ref, fg_ref, wm_ref, C0_ref, n0_ref, m0_ref,
             ya_ref, C_ref, n_ref, m_ref, b_s, ig_s):
        g = pl.program_id(1)
        r_i = lax.broadcasted_iota(jnp.int32, (L, L), 0)
        c_i = lax.broadcasted_iota(jnp.int32, (L, L), 1)
        tri_incl = (r_i <= c_i).astype(F32)
        for j in range(hb):
            ig_s[j] = ig_ref[0, j] + big_ref[g * hb + j]
            lf_all = jax.nn.log_sigmoid(fg_ref[0, j] + bfg_ref[g * hb + j])
            b_s[j] = jnp.dot(lf_all, tri_incl, precision=lax.Precision.HIGHEST, preferred_element_type=F32)
        C_ref[...] = C0_ref[...]
        n_ref[...] = n0_ref[...]
        m_ref[...] = m0_ref[...]
        eye = r_i == c_i
        causal = c_i <= r_i

        def col_of(row):
            return jnp.sum(jnp.where(eye, jnp.broadcast_to(row, (L, L)), 0.0), axis=1, keepdims=True)

        def head_chunk(j, c, rows):
            q = q_ref[0, rows, j * dk:(j + 1) * dk]
            k = k_ref[0, rows, j * dk:(j + 1) * dk]
            v = v_ref[0, rows, j * dv:(j + 1) * dv]
            b_row = b_s[j, pl.ds(c, 1), :]
            ig_row = ig_s[j, pl.ds(c, 1), :]
            m_prev = m_ref[0, j]
            C_prev = C_ref[0, j]
            n_prev = n_ref[0, j]

            b_col = col_of(b_row)
            dlog = b_col - jnp.broadcast_to(b_row, (L, L)) + jnp.broadcast_to(ig_row, (L, L))
            dlog = jnp.where(causal, dlog, -jnp.inf)
            inter = b_col + m_prev
            m_t = jnp.maximum(inter, jnp.max(dlog, axis=1, keepdims=True))
            qk = lax.dot_general(q, k, NT_DIMS, preferred_element_type=F32)
            s = qk * jnp.exp(dlog - m_t)
            a = jnp.exp(inter - m_t)
            qC = lax.dot_general(q, C_prev.astype(BF16), NT_DIMS, preferred_element_type=F32)
            sv = jnp.dot(s.astype(BF16), v, preferred_element_type=F32)
            num = a * qC + sv
            qn = jnp.sum(q.astype(F32) * n_prev, axis=1, keepdims=True)
            den = a * qn + jnp.sum(s, axis=1, keepdims=True)
            hid = num / jnp.maximum(jnp.abs(den), jnp.exp(-m_t))
            hn = hid * lax.rsqrt(jnp.mean(hid * hid, axis=1, keepdims=True) + RMS_EPS)
            hn = hn * wm_ref[:, j * dv:(j + 1) * dv]
            gate = og_ref[0, rows, j * dv:(j + 1) * dv].astype(F32)
            ya_ref[0, rows, j * dv:(j + 1) * dv] = (gate * hn).astype(ya_ref.dtype)

            bL = b_row[:, L - 1:L]
            g_row = bL - b_row + ig_row
            m_new = jnp.maximum(bL + m_prev, jnp.max(g_row, axis=1, keepdims=True))
            wk_col = col_of(jnp.exp(g_row - m_new))
            decay = jnp.exp(bL + m_prev - m_new)
            kw = k.astype(F32) * wk_col
            vk = lax.dot_general(v, kw.astype(BF16), TN_DIMS, preferred_element_type=F32)
            C_ref[0, j] = decay * C_prev + vk
            n_ref[0, j] = decay * n_prev + jnp.sum(kw, axis=0, keepdims=True)
            m_ref[0, j] = m_new

        def chunk(c, carry):
            rows = pl.ds(pl.multiple_of(c * L, L), L)
            for j in range(hb):
                head_chunk(j, c, rows)
            return carry

        lax.fori_loop(0, nC, chunk, 0)

    smem = pl.BlockSpec(memory_space=pltpu.SMEM)
    pipelined = [2 * _nbytes((S, hb * dk), BF16), 3 * _nbytes((S, hb * dv), BF16),
                 2 * _nbytes((hb, nCp, L), F32), 2 * _nbytes((hb, dv, dk), F32)]
    ya, C, n, m = pl.pallas_call(
        body,
        grid=(B, ng),
        in_specs=[smem, smem,
                  pl.BlockSpec((1, S, hb * dk), lambda b, g: (b, 0, g)),
                  pl.BlockSpec((1, S, hb * dk), lambda b, g: (b, 0, k_blk0 + g)),
                  pl.BlockSpec((1, S, hb * dv), lambda b, g: (b, 0, v_blk0 + g)),
                  pl.BlockSpec((1, S, hb * dv), lambda b, g: (b, 0, g)),
                  pl.BlockSpec((1, hb, nCp, L), lambda b, g: (b, g, 0, 0)),
                  pl.BlockSpec((1, hb, nCp, L), lambda b, g: (b, ng + g, 0, 0)),
                  pl.BlockSpec((1, hb * dv), lambda b, g: (0, g)),
                  pl.BlockSpec((1, hb, dv, dk), lambda b, g: (b, g, 0, 0)),
                  pl.BlockSpec((1, hb, 1, dk), lambda b, g: (b, g, 0, 0)),
                  pl.BlockSpec((1, hb, 1, 1), lambda b, g: (b, g, 0, 0))],
        out_specs=[pl.BlockSpec((1, S, hb * dv), lambda b, g: (b, 0, g)),
                   pl.BlockSpec((1, hb, dv, dk), lambda b, g: (b, g, 0, 0)),
                   pl.BlockSpec((1, hb, 1, dk), lambda b, g: (b, g, 0, 0)),
                   pl.BlockSpec((1, hb, 1, 1), lambda b, g: (b, g, 0, 0))],
        out_shape=[jax.ShapeDtypeStruct((B, S, M_V_W), BF16),
                   jax.ShapeDtypeStruct((B, H, dv, dk), F32),
                   jax.ShapeDtypeStruct((B, H, 1, dk), F32),
                   jax.ShapeDtypeStruct((B, H, 1, 1), F32)],
        scratch_shapes=[pltpu.VMEM((hb, nCp, L), F32), pltpu.VMEM((hb, nCp, L), F32)],
        compiler_params=pltpu.CompilerParams(
            dimension_semantics=("parallel", "parallel"),
            vmem_limit_bytes=_vmem_limit(pipelined, 4 << 20)),
        name="mlstm_chunks",
    )(b_ig, b_fg, qkv, qkv, qkv, og, gates_t, gates_t, w_mnorm.reshape(1, M_V_W), C0, n0, m0)
    return ya, C, n, m


def _sink_row(sink_ref, kvh, width):
    return jnp.concatenate([jnp.full((1, width), sink_ref[kvh * A_GROUP + g], F32) for g in range(A_GROUP)], axis=1)


def _swa_prompt(qt, kv, vt, sink, *, B, S):
    TB = 2 * CHUNK
    assert TB == V7X_LANES and S % TB == 0 and WIN_CHUNKS == 2
    nblk = S // TB
    G, KVH = A_GROUP, A_KV_HEADS

    def body(sink_ref, q_ref, kp_ref, kc_ref, vp_ref, vc_ref, o_ref):
        p_id = pl.program_id(1)
        key_chunk = lax.broadcasted_iota(jnp.int32, (2 * TB, TB), 0) // CHUNK
        q_half = lax.broadcasted_iota(jnp.int32, (2 * TB, TB), 1) // CHUNK
        valid = jnp.logical_and(key_chunk >= q_half, key_chunk <= q_half + WIN_CHUNKS)
        valid = jnp.logical_and(valid, jnp.logical_or(key_chunk >= WIN_CHUNKS, p_id > 0))
        bias = jnp.where(valid, 0.0, -jnp.inf).astype(F32)
        bias = jnp.concatenate([bias] * G, axis=1)
        zeros = jnp.zeros((A_HD, TB), q_ref.dtype)
        for kvh in range(KVH):
            pr = kvh // 2
            lanes = slice(pr * V7X_LANES, (pr + 1) * V7X_LANES)
            kc = jnp.concatenate([kp_ref[:, lanes], kc_ref[:, lanes]], axis=0).astype(BF16)
            rows = slice(kvh * A_HD, (kvh + 1) * A_HD)
            vt_w = jnp.concatenate([vp_ref[rows, :], vc_ref[rows, :]], axis=1)
            pieces = []
            for g in range(G):
                head = kvh * G + g
                qg = q_ref[head * A_HD:(head + 1) * A_HD, :]
                pieces.append(jnp.concatenate([zeros, qg] if kvh % 2 else [qg, zeros], axis=0))
            qz = jnp.concatenate(pieces, axis=1)
            s_t = jnp.dot(kc, qz, preferred_element_type=F32) + bias
            sk = _sink_row(sink_ref, kvh, TB)
            mx = jnp.maximum(jnp.max(s_t, axis=0, keepdims=True), sk)
            p_t = jnp.exp(s_t - mx)
            den = jnp.sum(p_t, axis=0, keepdims=True) + jnp.exp(sk - mx)
            o_t = jnp.dot(vt_w, p_t.astype(BF16), preferred_element_type=F32) / den
            for j in range(G // 2):
                two = jnp.concatenate([o_t[:, (2 * j) * TB:(2 * j + 1) * TB],
                                       o_t[:, (2 * j + 1) * TB:(2 * j + 2) * TB]], axis=0)
                col0 = (kvh * G + 2 * j) * A_HD
                o_ref[:, col0:col0 + V7X_LANES] = two.T.astype(o_ref.dtype)

    prev = lambda b, p: b * nblk + jnp.maximum(p - 1, 0)
    cur = lambda b, p: b * nblk + p
    pipelined = [2 * _nbytes((A_Q_W, TB), BF16), 2 * _nbytes((TB, A_KV_W), F32), 2 * _nbytes((A_KV_W, TB), BF16)]
    return pl.pallas_call(
        body,
        grid=(B, nblk),
        in_specs=[pl.BlockSpec(memory_space=pltpu.SMEM),
                  pl.BlockSpec((A_Q_W, TB), lambda b, p: (0, cur(b, p))),
                  pl.BlockSpec((TB, A_KV_W), lambda b, p: (prev(b, p), 0)),
                  pl.BlockSpec((TB, A_KV_W), lambda b, p: (cur(b, p), 0)),
                  pl.BlockSpec((A_KV_W, TB), lambda b, p: (0, prev(b, p))),
                  pl.BlockSpec((A_KV_W, TB), lambda b, p: (0, cur(b, p)))],
        out_specs=pl.BlockSpec((TB, A_Q_W), lambda b, p: (cur(b, p), 0)),
        out_shape=jax.ShapeDtypeStruct((B * S, A_Q_W), BF16),
        compiler_params=pltpu.CompilerParams(
            dimension_semantics=("parallel", "arbitrary"),
            vmem_limit_bytes=_vmem_limit(pipelined, 16 << 20)),
        name="swa_prompt_attention",
    )(sink, qt, kv, kv, vt, vt)


def _swa_sample(q, kv, sink):
    B, T, _ = q.shape
    Skv = kv.shape[1]
    G, KVH, LN = A_GROUP, A_KV_HEADS, V7X_LANES
    assert 2 * A_HD == LN and T % 16 == 0

    def body(sink_ref, q_ref, kv_ref, o_ref):
        kvw = kv_ref[0]
        lo_q = lax.broadcasted_iota(jnp.int32, (T, LN), 1) < A_HD
        lo_k = lax.broadcasted_iota(jnp.int32, (Skv, LN), 1) < A_HD
        zero = jnp.zeros((T, LN), q_ref.dtype)
        for kvh in range(KVH):
            pair, odd = kvh // 2, kvh % 2
            kx = kvw[:, pair * LN:(pair + 1) * LN]
            vx = kvw[:, A_KV_W + pair * LN:A_KV_W + (pair + 1) * LN]
            kr = pltpu.roll(kx, A_HD, 1)
            vr = pltpu.roll(vx, A_HD, 1)
            k2 = (jnp.where(lo_k, kr, kx) if odd else jnp.where(lo_k, kx, kr)).astype(BF16)
            v2 = (jnp.where(lo_k, vr, vx) if odd else jnp.where(lo_k, vx, vr)).astype(BF16)
            pieces, sinks = [], []
            for g in range(G):
                head = kvh * G + g
                q2 = q_ref[0, :, (head // 2) * LN:(head // 2 + 1) * LN]
                pieces.append(jnp.where(lo_q, zero, q2) if head % 2 else jnp.where(lo_q, q2, zero))
                sinks.append(jnp.full((T, 1), sink_ref[head], F32))
            qs = jnp.concatenate(pieces, axis=0)
            sk = jnp.concatenate(sinks, axis=0)
            s = lax.dot_general(qs, k2, NT_DIMS, preferred_element_type=F32)
            mx = jnp.maximum(jnp.max(s, axis=1, keepdims=True), sk)
            p = jnp.exp(s - mx)
            den = jnp.sum(p, axis=1, keepdims=True) + jnp.exp(sk - mx)
            o = jnp.dot(p.astype(BF16), v2, preferred_element_type=F32) / den
            for j in range(G // 2):
                pair_o = jnp.where(lo_q, o[(2 * j) * T:(2 * j + 1) * T], o[(2 * j + 1) * T:(2 * j + 2) * T])
                col0 = (kvh * G + 2 * j) * A_HD
                o_ref[0, :, col0:col0 + LN] = pair_o.astype(o_ref.dtype)

    pipelined = [2 * _nbytes((T, A_Q_W), BF16), _nbytes((Skv, 2 * A_KV_W), F32)]
    return pl.pallas_call(
        body,
        grid=(B,),
        in_specs=[pl.BlockSpec(memory_space=pltpu.SMEM),
                  pl.BlockSpec((1, T, A_Q_W), lambda b: (b, 0, 0)),
                  pl.BlockSpec((1, Skv, 2 * A_KV_W), lambda b: (b, 0, 0))],
        out_specs=pl.BlockSpec((1, T, A_Q_W), lambda b: (b, 0, 0)),
        out_shape=jax.ShapeDtypeStruct((B, T, A_Q_W), BF16),
        compiler_params=pltpu.CompilerParams(
            dimension_semantics=("parallel",),
            vmem_limit_bytes=_vmem_limit(pipelined, 8 << 20)),
        name="swa_sample_attention",
    )(sink, q, kv)


def _rope_tables(pos, n_rot_heads, n_plain_cols, scale):
    half = A_HD // 2
    inv = ROPE_THETA ** (-jnp.arange(half, dtype=F32) / half)
    ang = pos.astype(F32)[:, None] * inv[None, :]
    cos, sin = jnp.cos(ang) * scale, jnp.sin(ang) * scale
    cos_h = jnp.concatenate([cos, cos], -1)
    sin_h = jnp.concatenate([-sin, sin], -1)
    n = pos.shape[0]
    cos_t = jnp.concatenate([jnp.tile(cos_h, (1, n_rot_heads)), jnp.ones((n, n_plain_cols), F32)], -1)
    sin_t = jnp.concatenate([jnp.tile(sin_h, (1, n_rot_heads)), jnp.zeros((n, n_plain_cols), F32)], -1)
    return cos_t, sin_t


def _in_proj_layout(d_model):
    widths = [("mqkv", 2 * M_QK_W + M_V_W), ("sig", M_V_W + N_BRANCH * d_model), ("aq", A_Q_W),
              ("kv", 2 * A_KV_W), ("gate", V7X_LANES)]
    off, col = {}, 0
    for name, w in widths:
        off[name] = col
        col += w
    return off


def _prep_weights(w_in, w_branch_a, w_branch_b, w_out, w_up, w_down):
    d_model = w_in.shape[0]
    o_mo = 2 * M_QK_W + M_V_W
    o_mi = o_mo + M_V_W
    o_aq = o_mi + 2 * M_HEADS
    o_ak = o_aq + A_Q_W
    o_av = o_ak + A_KV_W
    o_gp = o_av + A_KV_W
    w_perm = jnp.concatenate([
        w_in[:, :o_mi], w_in[:, o_gp:o_gp + N_BRANCH * d_model], w_in[:, o_aq:o_gp], w_in[:, o_mi:o_aq],
        jnp.zeros((d_model, V7X_LANES - 2 * M_HEADS), w_in.dtype)], axis=1).astype(BF16)
    return dict(
        w_in=w_perm,
        aq_t=w_in[:, o_aq:o_ak].T.astype(BF16), v_t=w_in[:, o_av:o_gp].T.astype(BF16),
        br_a=w_branch_a.astype(BF16), br_b=w_branch_b.astype(BF16), out=w_out.astype(BF16),
        up=w_up.astype(BF16), down=w_down.astype(BF16))


def _layer(x, pos, W, b_ig, b_fg, w_mnorm, sink, ln1_g, ln1_b, ln2_g, ln2_b, alpha, state, cache, *, bm):
    B, S, D = x.shape
    M = B * S
    L = min(CHUNK, S)
    nC = S // L
    xf = x.reshape(M, D)
    xb = xf.astype(BF16)
    bn_big = min(1024, D)
    off = _in_proj_layout(D)
    w_in = W["w_in"]

    def pos_extra(table, width):
        if bm <= S:
            per = S // bm
            return (table, (bm, width), lambda i, j, k: (i % per, 0))
        return (jnp.tile(table, (bm // S, 1)), (bm, width), lambda i, j, k: (0, 0))

    n_qkv = 2 * M_QK_W + M_V_W
    colscale = jnp.concatenate([jnp.full((1, M_QK_W), M_DQK ** -0.5, F32),
                                jnp.ones((1, M_QK_W + M_V_W), F32)], 1)
    (qkv,) = _matmul(xb, w_in, bm=bm, bn=bn_big, bk=D, epilogue=_ep_colscale, out_dtypes=[BF16],
                     extras=[(colscale, (1, bn_big), lambda i, j, k: (0, j))],
                     w_col0=off["mqkv"], n_cols=n_qkv, name="in_proj_mlstm_qkv")
    (sig,) = _matmul(xb, w_in, bm=bm, bn=bn_big, bk=D, epilogue=_ep_sigmoid, out_dtypes=[BF16],
                     w_col0=off["sig"], n_cols=M_V_W + N_BRANCH * D, name="in_proj_sigmoid_gates")
    (gates,) = _matmul(xb, w_in, bm=bm, bn=V7X_LANES, bk=D, epilogue=_ep_identity, out_dtypes=[F32],
                       w_col0=off["gate"], n_cols=V7X_LANES, name="in_proj_if_gates")
    wkv = 2 * A_KV_W
    cos_k, sin_k = _rope_tables(pos, A_KV_HEADS, A_KV_W, 1.0)
    (kv_new,) = _matmul(xb, w_in, bm=bm, bn=wkv, bk=D, epilogue=_ep_rope, out_dtypes=[F32],
                        extras=[pos_extra(cos_k, wkv), pos_extra(sin_k, wkv)],
                        w_col0=off["kv"], n_cols=wkv, name="in_proj_attn_kv")

    nCp = max(nC, V7X_SUBLANES)
    gates_t = gates[:, :2 * M_HEADS].reshape(B, nC, L, 2 * M_HEADS).transpose(0, 3, 1, 2)
    gates_t = jnp.pad(gates_t, ((0, 0), (0, 0), (0, nCp - nC), (0, 0)))
    if state is None:
        C0 = jnp.zeros((B, M_HEADS, M_DV, M_DQK), F32)
        n0 = jnp.zeros((B, M_HEADS, 1, M_DQK), F32)
        m0 = jnp.zeros((B, M_HEADS, 1, 1), F32)
    else:
        C0 = state[0].astype(F32)
        n0 = state[1].astype(F32).reshape(B, M_HEADS, 1, M_DQK)
        m0 = state[2].astype(F32).reshape(B, M_HEADS, 1, 1)
    ya, C, n_st, m = _mlstm(qkv.reshape(B, S, -1), sig.reshape(B, S, -1), gates_t, b_ig, b_fg, w_mnorm,
                            C0, n0, m0, L=L)

    kv3 = kv_new.reshape(B, S, wkv)
    k_new = kv3[:, :, :A_KV_W].reshape(B, S, A_KV_HEADS, A_HD)
    v_new = kv3[:, :, A_KV_W:].reshape(B, S, A_KV_HEADS, A_HD)
    q_scale = A_HD ** -0.5
    if cache is None:
        wq = 512
        cos_q, sin_q = _rope_tables(pos, wq // A_HD, 0, q_scale)
        if bm <= S:
            tab = lambda t: (t.T, (wq, bm), lambda i, j: (0, i % (S // bm)))
        else:
            tab = lambda t: (jnp.tile(t.T, (1, bm // S)), (wq, bm), lambda i, j: (0, 0))
        qt = _matmul_nt(W["aq_t"], xb, bm=bm, bn=wq, epilogue=_ep_rope_t, out_dtype=BF16,
                        extras=[tab(cos_q), tab(sin_q)], name="in_proj_attn_q_t")
        vt = _matmul_nt(W["v_t"], xb, bm=bm, bn=A_KV_W, epilogue=_ep_identity, out_dtype=BF16,
                        name="in_proj_attn_v_t")
        yb = _swa_prompt(qt, kv_new, vt, sink, B=B, S=S)
    else:
        wq = 512
        cos_q, sin_q = _rope_tables(pos, wq // A_HD, 0, q_scale)
        (aq,) = _matmul(xb, w_in, bm=bm, bn=wq, bk=D, epilogue=_ep_rope, out_dtypes=[BF16],
                        extras=[pos_extra(cos_q, wq), pos_extra(sin_q, wq)],
                        w_col0=off["aq"], n_cols=A_Q_W, name="in_proj_attn_q")
        ck = cache[0].astype(F32).reshape(B, WINDOW, A_KV_W)
        cv = cache[1].astype(F32).reshape(B, WINDOW, A_KV_W)
        kv_all = jnp.concatenate([jnp.concatenate([ck, cv], -1), kv3], axis=1)
        yb = _swa_sample(aq.reshape(B, S, A_Q_W), kv_all, sink).reshape(M, A_Q_W)

    merged = _merge(ya.reshape(M, M_V_W), yb, W["br_a"], W["br_b"], sig,
                    M_V_W, M_V_W + D, bm=bm, bn=bn_big)
    res1 = (xf, (bm, bn_big), lambda i, j, k: (i, j))
    (h_pre,) = _matmul(merged, W["out"], bm=bm, bn=bn_big, bk=D, epilogue=functools.partial(_ep_residual, alpha),
                       out_dtypes=[F32], extras=[res1], name="out_proj_residual")
    bm_ln = min(256, M)
    h32, h16 = _layer_norm(h_pre, ln1_g, ln1_b, bm=bm_ln, out_dtypes=[F32, BF16])
    (act,) = _matmul(h16, W["up"], bm=bm, bn=bn_big, bk=D, epilogue=_ep_relu_sq, out_dtypes=[BF16],
                     name="mlp_up_relu_sq")
    res2 = (h32, (bm, bn_big), lambda i, j, k: (i, j))
    (y_pre,) = _matmul(act, W["down"], bm=bm, bn=bn_big, bk=min(2048, act.shape[1]),
                       epilogue=functools.partial(_ep_residual, alpha), out_dtypes=[F32], extras=[res2],
                       name="mlp_down_residual")
    (y,) = _layer_norm(y_pre, ln2_g, ln2_b, bm=bm_ln, out_dtypes=[F32])
    return (y.reshape(B, S, D), k_new, v_new, C, n_st.reshape(B, M_HEADS, M_DQK), m.reshape(B, M_HEADS))


def kernel(x_prompt, x_sample, cache_swa_k, cache_swa_v, state_mlstm_C, state_mlstm_n, state_mlstm_m,
           w_in, b_igate, b_fgate, w_mnorm, attn_sink, w_branch_a, w_branch_b, w_out,
           ln1_g, ln1_b, w_up, w_down, ln2_g, ln2_b):
    depth = w_in.shape[0]
    alpha = (2.0 * depth) ** 0.25
    S, T = x_prompt.shape[1], x_sample.shape[1]
    pos_p = jnp.arange(S, dtype=jnp.int32)
    pos_s = PAST_LEN + jnp.arange(T, dtype=jnp.int32)
    xp, xs = x_prompt, x_sample
    outs_p, outs_s = [], []
    for l in range(depth):
        W = _prep_weights(w_in[l], w_branch_a[l], w_branch_b[l], w_out[l], w_up[l], w_down[l])
        shared = (W, b_igate[l], b_fgate[l], w_mnorm[l], attn_sink[l], ln1_g[l], ln1_b[l], ln2_g[l], ln2_b[l], alpha)
        xp, *rest_p = _layer(xp, pos_p, *shared, None, None, bm=min(1024, xp.shape[0] * S))
        outs_p.append(rest_p)
        xs, *rest_s = _layer(xs, pos_s, *shared,
                             (state_mlstm_C[l], state_mlstm_n[l], state_mlstm_m[l]),
                             (cache_swa_k[l], cache_swa_v[l]), bm=xs.shape[0] * T)
        outs_s.append(rest_s)
    stack = lambda outs, i: jnp.stack([o[i] for o in outs])
    pk = jnp.stack([o[0][:, -WINDOW:] for o in outs_p])
    pv = jnp.stack([o[1][:, -WINDOW:] for o in outs_p])
    return (xp, xs, pk, pv, stack(outs_p, 2), stack(outs_p, 3), stack(outs_p, 4),
            stack(outs_s, 0), stack(outs_s, 1), stack(outs_s, 2), stack(outs_s, 3), stack(outs_s, 4))
```

```python
import functools

import jax
import jax.numpy as jnp
from jax import lax
from jax.experimental import pallas as pl
from jax.experimental.pallas import tpu as pltpu

CHUNK = 64
M_HEADS = 8
M_DQK = 128
M_DV = 256
A_HEADS = 32
A_KV_HEADS = 4
A_GROUP = A_HEADS // A_KV_HEADS
A_HD = 64
WINDOW = 128
WIN_CHUNKS = WINDOW // CHUNK
ROPE_THETA = 10000.0
PAST_LEN = 1024
N_BRANCH = 2
LN_EPS = 1e-5
RMS_EPS = 1e-6

M_QK_W = M_HEADS * M_DQK
M_V_W = M_HEADS * M_DV
A_Q_W = A_HEADS * A_HD
A_KV_W = A_KV_HEADS * A_HD

V7X_VMEM_BYTES = 64 * 1024 * 1024
V7X_LANES = 128
V7X_SUBLANES = 8
VMEM_REQUEST_CAP = (V7X_VMEM_BYTES * 7) // 8

F32 = jnp.float32
BF16 = jnp.bfloat16
NT_DIMS = (((1,), (1,)), ((), ()))
TN_DIMS = (((0,), (0,)), ((), ()))


def _nbytes(shape, dtype):
    n = 1
    for s in shape:
        n *= s
    return n * jnp.dtype(dtype).itemsize


def _vmem_limit(pipelined, resident=0):
    return int(min(VMEM_REQUEST_CAP, 2 * sum(pipelined) + resident + (4 << 20)))


def _matmul(x, w, *, bm, bn, bk, epilogue, out_dtypes, extras=(), w_col0=0, n_cols=None, name):
    M, K = x.shape
    N = w.shape[1] if n_cols is None else n_cols
    assert M % bm == 0 and N % bn == 0 and K % bk == 0 and w_col0 % bn == 0, (M, N, K, bm, bn, bk, w_col0)
    nm, nn, nk = M // bm, N // bn, K // bk
    j0 = w_col0 // bn
    n_ex, n_out = len(extras), len(out_dtypes)

    def body(*refs):
        x_ref, w_ref = refs[0], refs[1]
        ex_refs = refs[2:2 + n_ex]
        out_refs = refs[2 + n_ex:2 + n_ex + n_out]
        def product():
            return jnp.dot(x_ref[...], w_ref[...], preferred_element_type=F32)

        def finish(acc):
            outs = epilogue(acc, *[r[...] for r in ex_refs])
            for o_ref, o in zip(out_refs, outs):
                o_ref[...] = o.astype(o_ref.dtype)

        if nk == 1:
            finish(product())
        else:
            acc_ref = refs[-1]
            k = pl.program_id(2)

            @pl.when(k == 0)
            def _():
                acc_ref[...] = product()

            @pl.when(jnp.logical_and(k > 0, k < nk - 1))
            def _():
                acc_ref[...] = acc_ref[...] + product()

            @pl.when(k == nk - 1)
            def _():
                finish(acc_ref[...] + product())

    in_specs = [pl.BlockSpec((bm, bk), lambda i, j, k: (i, k)),
                pl.BlockSpec((bk, bn), lambda i, j, k: (k, j0 + j))]
    in_specs += [pl.BlockSpec(bs, im) for (_, bs, im) in extras]
    out_specs = [pl.BlockSpec((bm, bn), lambda i, j, k: (i, j)) for _ in out_dtypes]
    out_shape = [jax.ShapeDtypeStruct((M, N), dt) for dt in out_dtypes]
    scratch = [pltpu.VMEM((bm, bn), F32)] if nk > 1 else []
    pipelined = [_nbytes((bm, bk), x.dtype), _nbytes((bk, bn), w.dtype)]
    pipelined += [_nbytes(bs, a.dtype) for (a, bs, _) in extras]
    pipelined += [_nbytes((bm, bn), dt) for dt in out_dtypes]
    resident = _nbytes((bm, bn), F32) * (4 if nk > 1 else 3)
    outs = pl.pallas_call(
        body,
        grid=(nm, nn, nk),
        in_specs=in_specs,
        out_specs=out_specs,
        out_shape=out_shape,
        scratch_shapes=scratch,
        compiler_params=pltpu.CompilerParams(
            dimension_semantics=("parallel", "parallel", "arbitrary"),
            vmem_limit_bytes=_vmem_limit(pipelined, resident)),
        name=name,
    )(x, w, *[a for (a, _, _) in extras])
    return outs


def _matmul_nt(wt, x, *, bm, bn, epilogue, out_dtype, extras=(), name):
    N, K = wt.shape
    M = x.shape[0]
    assert M % bm == 0 and N % bn == 0

    def body(w_ref, x_ref, *refs):
        acc = lax.dot_general(w_ref[...], x_ref[...], NT_DIMS, preferred_element_type=F32)
        (out,) = epilogue(acc, *[r[...] for r in refs[:-1]])
        refs[-1][...] = out.astype(refs[-1].dtype)

    pipelined = [_nbytes((bn, K), wt.dtype), _nbytes((bm, K), x.dtype), _nbytes((bn, bm), out_dtype)]
    pipelined += [_nbytes(bs, a.dtype) for (a, bs, _) in extras]
    return pl.pallas_call(
        body,
        grid=(M // bm, N // bn),
        in_specs=[pl.BlockSpec((bn, K), lambda i, j: (j, 0)),
                  pl.BlockSpec((bm, K), lambda i, j: (i, 0))] + [pl.BlockSpec(bs, im) for (_, bs, im) in extras],
        out_specs=pl.BlockSpec((bn, bm), lambda i, j: (j, i)),
        out_shape=jax.ShapeDtypeStruct((N, M), out_dtype),
        compiler_params=pltpu.CompilerParams(
            dimension_semantics=("parallel", "parallel"),
            vmem_limit_bytes=_vmem_limit(pipelined, 3 * _nbytes((bn, bm), F32))),
        name=name,
    )(wt, x, *[a for (a, _, _) in extras])


def _ep_colscale(acc, scale_row):
    return (acc * scale_row,)


def _ep_sigmoid(acc):
    return (0.5 * jnp.tanh(0.5 * acc) + 0.5,)


def _ep_identity(acc):
    return (acc,)


def _ep_relu_sq(acc):
    r = jnp.maximum(acc, 0.0)
    return (r * r,)


def _ep_rope(acc, cos, sin_signed):
    width = acc.shape[1]
    half = A_HD // 2
    lane = lax.broadcasted_iota(jnp.int32, acc.shape, 1)
    first_half = (lane % A_HD) < half
    partner = jnp.where(first_half, pltpu.roll(acc, width - half, 1), pltpu.roll(acc, half, 1))
    return (acc * cos + partner * sin_signed,)


def _ep_rope_t(acc, cos, sin_signed):
    rows = acc.shape[0]
    half = A_HD // 2
    row = lax.broadcasted_iota(jnp.int32, acc.shape, 0)
    first_half = (row % A_HD) < half
    partner = jnp.where(first_half, pltpu.roll(acc, rows - half, 0), pltpu.roll(acc, half, 0))
    return (acc * cos + partner * sin_signed,)


def _ep_residual(alpha, acc, res):
    return (alpha * res + acc,)


def _merge(ya, yb, wa, wb, gates, g0_col, g1_col, *, bm, bn):
    M, Ka = ya.shape
    Kb = yb.shape[1]
    N = wa.shape[1]
    assert M % bm == 0 and N % bn == 0 and g0_col % bn == 0 and g1_col % bn == 0
    o0, o1 = g0_col // bn, g1_col // bn

    def body(ya_ref, yb_ref, wa_ref, wb_ref, g0_ref, g1_ref, o_ref):
        a = jnp.dot(ya_ref[...], wa_ref[...], preferred_element_type=F32)
        b = jnp.dot(yb_ref[...], wb_ref[...], preferred_element_type=F32)
        o_ref[...] = (g0_ref[...].astype(F32) * a + g1_ref[...].astype(F32) * b).astype(o_ref.dtype)

    pipelined = [_nbytes((bm, Ka), BF16), _nbytes((bm, Kb), BF16), _nbytes((Ka, bn), BF16),
                 _nbytes((Kb, bn), BF16), 3 * _nbytes((bm, bn), BF16)]
    return pl.pallas_call(
        body,
        grid=(M // bm, N // bn),
        in_specs=[pl.BlockSpec((bm, Ka), lambda i, j: (i, 0)),
                  pl.BlockSpec((bm, Kb), lambda i, j: (i, 0)),
                  pl.BlockSpec((Ka, bn), lambda i, j: (0, j)),
                  pl.BlockSpec((Kb, bn), lambda i, j: (0, j)),
                  pl.BlockSpec((bm, bn), lambda i, j: (i, o0 + j)),
                  pl.BlockSpec((bm, bn), lambda i, j: (i, o1 + j))],
        out_specs=pl.BlockSpec((bm, bn), lambda i, j: (i, j)),
        out_shape=jax.ShapeDtypeStruct((M, N), BF16),
        compiler_params=pltpu.CompilerParams(
            dimension_semantics=("parallel", "parallel"),
            vmem_limit_bytes=_vmem_limit(pipelined, 2 * _nbytes((bm, bn), F32))),
        name="branch_merge",
    )(ya, yb, wa, wb, gates, gates)


def _layer_norm(x, g, b, *, bm, out_dtypes):
    M, D = x.shape
    assert M % bm == 0

    def body(x_ref, g_ref, b_ref, *o_refs):
        xf = x_ref[...]
        mu = jnp.mean(xf, axis=-1, keepdims=True)
        xc = xf - mu
        var = jnp.mean(xc * xc, axis=-1, keepdims=True)
        y = xc * lax.rsqrt(var + LN_EPS) * g_ref[...] + b_ref[...]
        for o_ref in o_refs:
            o_ref[...] = y.astype(o_ref.dtype)

    pipelined = [_nbytes((bm, D), F32)] + [_nbytes((bm, D), dt) for dt in out_dtypes]
    return pl.pallas_call(
        body,
        grid=(M // bm,),
        in_specs=[pl.BlockSpec((bm, D), lambda i: (i, 0)),
                  pl.BlockSpec((1, D), lambda i: (0, 0)),
                  pl.BlockSpec((1, D), lambda i: (0, 0))],
        out_specs=[pl.BlockSpec((bm, D), lambda i: (i, 0)) for _ in out_dtypes],
        out_shape=[jax.ShapeDtypeStruct((M, D), dt) for dt in out_dtypes],
        compiler_params=pltpu.CompilerParams(
            dimension_semantics=("parallel",),
            vmem_limit_bytes=_vmem_limit(pipelined, 2 * _nbytes((bm, D), F32))),
        name="layer_norm",
    )(x, g.reshape(1, D), b.reshape(1, D))


MLSTM_HEADS_PER_STEP = 4


def _mlstm(qkv, og, gates_t, b_ig, b_fg, w_mnorm, C0, n0, m0, *, L):
    B, S, _ = qkv.shape
    nC = S // L
    nCp = gates_t.shape[2]
    H, dk, dv = M_HEADS, M_DQK, M_DV
    hb = MLSTM_HEADS_PER_STEP
    ng = H // hb
    k_blk0 = M_QK_W // (hb * dk)
    v_blk0 = (2 * M_QK_W) // (hb * dv)

    def body(big_ref, bfg_ref, q_ref, k_ref, v_ref, og_ref, ig_ref, fg_ref, wm_ref, C0_ref, n0_ref, m0_ref,
             ya_ref, C_ref, n_ref, m_ref, b_s, ig_s):
        g = pl.program_id(1)
        r_i = lax.broadcasted_iota(jnp.int32, (L, L), 0)
        c_i = lax.broadcasted_iota(jnp.int32, (L, L), 1)
        tri_incl = (r_i <= c_i).astype(F32)
        for j in range(hb):
            ig_s[j] = ig_ref[0, j] + big_ref[g * hb + j]
            lf_all = jax.nn.log_sigmoid(fg_ref[0, j] + bfg_ref[g * hb + j])
            b_s[j] = jnp.dot(lf_all, tri_incl, precision=lax.Precision.HIGHEST, preferred_element_type=F32)
        C_ref[...] = C0_ref[...]
        n_ref[...] = n0_ref[...]
        m_ref[...] = m0_ref[...]
        eye = r_i == c_i
        causal = c_i <= r_i

        def col_of(row):
            return jnp.sum(jnp.where(eye, jnp.broadcast_to(row, (L, L)), 0.0), axis=1, keepdims=True)

        def head_chunk(j, c, rows):
            q = q_ref[0, rows, j * dk:(j + 1) * dk]
            k = k_ref[0, rows, j * dk:(j + 1) * dk]
            v = v_ref[0, rows, j * dv:(j + 1) * dv]
            b_row = b_s[j, pl.ds(c, 1), :]
            ig_row = ig_s[j, pl.ds(c, 1), :]
            m_prev = m_ref[0, j]
            C_prev = C_ref[0, j]
            n_prev = n_ref[0, j]

            b_col = col_of(b_row)
            dlog = b_col - jnp.broadcast_to(b_row, (L, L)) + jnp.broadcast_to(ig_row, (L, L))
            dlog = jnp.where(causal, dlog, -jnp.inf)
            inter = b_col + m_prev
            m_t = jnp.maximum(inter, jnp.max(dlog, axis=1, keepdims=True))
            qk = lax.dot_general(q, k, NT_DIMS, preferred_element_type=F32)
            s = qk * jnp.exp(dlog - m_t)
            a = jnp.exp(inter - m_t)
            qC = lax.dot_general(q, C_prev.astype(BF16), NT_DIMS, preferred_element_type=F32)
            sv = jnp.dot(s.astype(BF16), v, preferred_element_type=F32)
            num = a * qC + sv
            qn = jnp.sum(q.astype(F32) * n_prev, axis=1, keepdims=True)
            den = a * qn + jnp.sum(s, axis=1, keepdims=True)
            hid = num / jnp.maximum(jnp.abs(den), jnp.exp(-m_t))
            hn = hid * lax.rsqrt(jnp.mean(hid * hid, axis=1, keepdims=True) + RMS_EPS)
            hn = hn * wm_ref[:, j * dv:(j + 1) * dv]
            gate = og_ref[0, rows, j * dv:(j + 1) * dv].astype(F32)
            ya_ref[0, rows, j * dv:(j + 1) * dv] = (gate * hn).astype(ya_ref.dtype)

            bL = b_row[:, L - 1:L]
            g_row = bL - b_row + ig_row
            m_new = jnp.maximum(bL + m_prev, jnp.max(g_row, axis=1, keepdims=True))
            wk_col = col_of(jnp.exp(g_row - m_new))
            decay = jnp.exp(bL + m_prev - m_new)
            kw = k.astype(F32) * wk_col
            vk = lax.dot_general(v, kw.astype(BF16), TN_DIMS, preferred_element_type=F32)
            C_ref[0, j] = decay * C_prev + vk
            n_ref[0, j] = decay * n_prev + jnp.sum(kw, axis=0, keepdims=True)
            m_ref[0, j] = m_new

        def chunk(c, carry):
            rows = pl.ds(pl.multiple_of(c * L, L), L)
            for j in range(hb):
                head_chunk(j, c, rows)
            return carry

        lax.fori_loop(0, nC, chunk, 0)

    smem = pl.BlockSpec(memory_space=pltpu.SMEM)
    pipelined = [2 * _nbytes((S, hb * dk), BF16), 3 * _nbytes((S, hb * dv), BF16),
                 2 * _nbytes((hb, nCp, L), F32), 2 * _nbytes((hb, dv, dk), F32)]
    ya, C, n, m = pl.pallas_call(
        body,
        grid=(B, ng),
        in_specs=[smem, smem,
                  pl.BlockSpec((1, S, hb * dk), lambda b, g: (b, 0, g)),
                  pl.BlockSpec((1, S, hb * dk), lambda b, g: (b, 0, k_blk0 + g)),
                  pl.BlockSpec((1, S, hb * dv), lambda b, g: (b, 0, v_blk0 + g)),
                  pl.BlockSpec((1, S, hb * dv), lambda b, g: (b, 0, g)),
                  pl.BlockSpec((1, hb, nCp, L), lambda b, g: (b, g, 0, 0)),
                  pl.BlockSpec((1, hb, nCp, L), lambda b, g: (b, ng + g, 0, 0)),
                  pl.BlockSpec((1, hb * dv), lambda b, g: (0, g)),
                  pl.BlockSpec((1, hb, dv, dk), lambda b, g: (b, g, 0, 0)),
                  pl.BlockSpec((1, hb, 1, dk), lambda b, g: (b, g, 0, 0)),
                  pl.BlockSpec((1, hb, 1, 1), lambda b, g: (b, g, 0, 0))],
        out_specs=[pl.BlockSpec((1, S, hb * dv), lambda b, g: (b, 0, g)),
                   pl.BlockSpec((1, hb, dv, dk), lambda b, g: (b, g, 0, 0)),
                   pl.BlockSpec((1, hb, 1, dk), lambda b, g: (b, g, 0, 0)),
                   pl.BlockSpec((1, hb, 1, 1), lambda b, g: (b, g, 0, 0))],
        out_shape=[jax.ShapeDtypeStruct((B, S, M_V_W), BF16),
                   jax.ShapeDtypeStruct((B, H, dv, dk), F32),
                   jax.ShapeDtypeStruct((B, H, 1, dk), F32),
                   jax.ShapeDtypeStruct((B, H, 1, 1), F32)],
        scratch_shapes=[pltpu.VMEM((hb, nCp, L), F32), pltpu.VMEM((hb, nCp, L), F32)],
        compiler_params=pltpu.CompilerParams(
            dimension_semantics=("parallel", "parallel"),
            vmem_limit_bytes=_vmem_limit(pipelined, 4 << 20)),
        name="mlstm_chunks",
    )(b_ig, b_fg, qkv, qkv, qkv, og, gates_t, gates_t, w_mnorm.reshape(1, M_V_W), C0, n0, m0)
    return ya, C, n, m


def _sink_row(sink_ref, kvh, width):
    return jnp.concatenate([jnp.full((1, width), sink_ref[kvh * A_GROUP + g], F32) for g in range(A_GROUP)], axis=1)


def _swa_prompt(qt, kv, vt, sink, *, B, S):
    TB = 2 * CHUNK
    assert TB == V7X_LANES and S % TB == 0 and WIN_CHUNKS == 2
    nblk = S // TB
    G, KVH = A_GROUP, A_KV_HEADS

    def body(sink_ref, q_ref, kp_ref, kc_ref, vp_ref, vc_ref, o_ref):
        p_id = pl.program_id(1)
        key_chunk = lax.broadcasted_iota(jnp.int32, (2 * TB, TB), 0) // CHUNK
        q_half = lax.broadcasted_iota(jnp.int32, (2 * TB, TB), 1) // CHUNK
        valid = jnp.logical_and(key_chunk >= q_half, key_chunk <= q_half + WIN_CHUNKS)
        valid = jnp.logical_and(valid, jnp.logical_or(key_chunk >= WIN_CHUNKS, p_id > 0))
        bias = jnp.where(valid, 0.0, -jnp.inf).astype(F32)
        bias = jnp.concatenate([bias] * G, axis=1)
        zeros = jnp.zeros((A_HD, TB), q_ref.dtype)
        for kvh in range(KVH):
            pr = kvh // 2
            lanes = slice(pr * V7X_LANES, (pr + 1) * V7X_LANES)
            kc = jnp.concatenate([kp_ref[:, lanes], kc_ref[:, lanes]], axis=0).astype(BF16)
            rows = slice(kvh * A_HD, (kvh + 1) * A_HD)
            vt_w = jnp.concatenate([vp_ref[rows, :], vc_ref[rows, :]], axis=1)
            pieces = []
            for g in range(G):
                head = kvh * G + g
                qg = q_ref[head * A_HD:(head + 1) * A_HD, :]
                pieces.append(jnp.concatenate([zeros, qg] if kvh % 2 else [qg, zeros], axis=0))
            qz = jnp.concatenate(pieces, axis=1)
            s_t = jnp.dot(kc, qz, preferred_element_type=F32) + bias
            sk = _sink_row(sink_ref, kvh, TB)
            mx = jnp.maximum(jnp.max(s_t, axis=0, keepdims=True), sk)
            p_t = jnp.exp(s_t - mx)
            den = jnp.sum(p_t, axis=0, keepdims=True) + jnp.exp(sk - mx)
            o_t = jnp.dot(vt_w, p_t.astype(BF16), preferred_element_type=F32) / den
            for j in range(G // 2):
                two = jnp.concatenate([o_t[:, (2 * j) * TB:(2 * j + 1) * TB],
                                       o_t[:, (2 * j + 1) * TB:(2 * j + 2) * TB]], axis=0)
                col0 = (kvh * G + 2 * j) * A_HD
                o_ref[:, col0:col0 + V7X_LANES] = two.T.astype(o_ref.dtype)

    prev = lambda b, p: b * nblk + jnp.maximum(p - 1, 0)
    cur = lambda b, p: b * nblk + p
    pipelined = [2 * _nbytes((A_Q_W, TB), BF16), 2 * _nbytes((TB, A_KV_W), F32), 2 * _nbytes((A_KV_W, TB), BF16)]
    return pl.pallas_call(
        body,
        grid=(B, nblk),
        in_specs=[pl.BlockSpec(memory_space=pltpu.SMEM),
                  pl.BlockSpec((A_Q_W, TB), lambda b, p: (0, cur(b, p))),
                  pl.BlockSpec((TB, A_KV_W), lambda b, p: (prev(b, p), 0)),
                  pl.BlockSpec((TB, A_KV_W), lambda b, p: (cur(b, p), 0)),
                  pl.BlockSpec((A_KV_W, TB), lambda b, p: (0, prev(b, p))),
                  pl.BlockSpec((A_KV_W, TB), lambda b, p: (0, cur(b, p)))],
        out_specs=pl.BlockSpec((TB, A_Q_W), lambda b, p: (cur(b, p), 0)),
        out_shape=jax.ShapeDtypeStruct((B * S, A_Q_W), BF16),
        compiler_params=pltpu.CompilerParams(
            dimension_semantics=("parallel", "arbitrary"),
            vmem_limit_bytes=_vmem_limit(pipelined, 16 << 20)),
        name="swa_prompt_attention",
    )(sink, qt, kv, kv, vt, vt)


def _swa_sample(q, kv, sink):
    B, T, _ = q.shape
    Skv = kv.shape[1]
    G, KVH, LN = A_GROUP, A_KV_HEADS, V7X_LANES
    assert 2 * A_HD == LN and T % 16 == 0

    def body(sink_ref, q_ref, kv_ref, o_ref):
        kvw = kv_ref[0]
        lo_q = lax.broadcasted_iota(jnp.int32, (T, LN), 1) < A_HD
        lo_k = lax.broadcasted_iota(jnp.int32, (Skv, LN), 1) < A_HD
        zero = jnp.zeros((T, LN), q_ref.dtype)
        for kvh in range(KVH):
            pair, odd = kvh // 2, kvh % 2
            kx = kvw[:, pair * LN:(pair + 1) * LN]
            vx = kvw[:, A_KV_W + pair * LN:A_KV_W + (pair + 1) * LN]
            kr = pltpu.roll(kx, A_HD, 1)
            vr = pltpu.roll(vx, A_HD, 1)
            k2 = (jnp.where(lo_k, kr, kx) if odd else jnp.where(lo_k, kx, kr)).astype(BF16)
            v2 = (jnp.where(lo_k, vr, vx) if odd else jnp.where(lo_k, vx, vr)).astype(BF16)
            pieces, sinks = [], []
            for g in range(G):
                head = kvh * G + g
                q2 = q_ref[0, :, (head // 2) * LN:(head // 2 + 1) * LN]
                pieces.append(jnp.where(lo_q, zero, q2) if head % 2 else jnp.where(lo_q, q2, zero))
                sinks.append(jnp.full((T, 1), sink_ref[head], F32))
            qs = jnp.concatenate(pieces, axis=0)
            sk = jnp.concatenate(sinks, axis=0)
            s = lax.dot_general(qs, k2, NT_DIMS, preferred_element_type=F32)
            mx = jnp.maximum(jnp.max(s, axis=1, keepdims=True), sk)
            p = jnp.exp(s - mx)
            den = jnp.sum(p, axis=1, keepdims=True) + jnp.exp(sk - mx)
            o = jnp.dot(p.astype(BF16), v2, preferred_element_type=F32) / den
            for j in range(G // 2):
                pair_o = jnp.where(lo_q, o[(2 * j) * T:(2 * j + 1) * T], o[(2 * j + 1) * T:(2 * j + 2) * T])
                col0 = (kvh * G + 2 * j) * A_HD
                o_ref[0, :, col0:col0 + LN] = pair_o.astype(o_ref.dtype)

    pipelined = [2 * _nbytes((T, A_Q_W), BF16), _nbytes((Skv, 2 * A_KV_W), F32)]
    return pl.pallas_call(
        body,
        grid=(B,),
        in_specs=[pl.BlockSpec(memory_space=pltpu.SMEM),
                  pl.BlockSpec((1, T, A_Q_W), lambda b: (b, 0, 0)),
                  pl.BlockSpec((1, Skv, 2 * A_KV_W), lambda b: (b, 0, 0))],
        out_specs=pl.BlockSpec((1, T, A_Q_W), lambda b: (b, 0, 0)),
        out_shape=jax.ShapeDtypeStruct((B, T, A_Q_W), BF16),
        compiler_params=pltpu.CompilerParams(
            dimension_semantics=("parallel",),
            vmem_limit_bytes=_vmem_limit(pipelined, 8 << 20)),
        name="swa_sample_attention",
    )(sink, q, kv)


def _rope_tables(pos, n_rot_heads, n_plain_cols, scale):
    half = A_HD // 2
    inv = ROPE_THETA ** (-jnp.arange(half, dtype=F32) / half)
    ang = pos.astype(F32)[:, None] * inv[None, :]
    cos, sin = jnp.cos(ang) * scale, jnp.sin(ang) * scale
    cos_h = jnp.concatenate([cos, cos], -1)
    sin_h = jnp.concatenate([-sin, sin], -1)
    n = pos.shape[0]
    cos_t = jnp.concatenate([jnp.tile(cos_h, (1, n_rot_heads)), jnp.ones((n, n_plain_cols), F32)], -1)
    sin_t = jnp.concatenate([jnp.tile(sin_h, (1, n_rot_heads)), jnp.zeros((n, n_plain_cols), F32)], -1)
    return cos_t, sin_t


W_IN_BLOCK = 512


def _in_proj_layout(d_model):
    widths = [("mqkv", 2 * M_QK_W + M_V_W), ("sig", M_V_W + N_BRANCH * d_model), ("aq", A_Q_W),
              ("kv", 2 * A_KV_W), ("gate", W_IN_BLOCK)]
    off, col = {}, 0
    for name, w in widths:
        off[name] = col
        col += w
    off["end"] = col
    return off


def _regroup_w_in(w_in):
    D = w_in.shape[0]
    off = _in_proj_layout(D)
    bw, ln = W_IN_BLOCK, V7X_LANES
    n_gate = 2 * M_HEADS
    src_gate = 2 * M_QK_W + 2 * M_V_W
    src_aq = src_gate + n_gate
    src_gp = src_aq + A_Q_W + 2 * A_KV_W
    assert src_gate % bw == 0 and off["sig"] % bw == 0 and (N_BRANCH * D) % bw == 0 and n_gate < ln
    j_gp = (off["sig"] + M_V_W) // bw
    j_aq = off["aq"] // bw
    j_gate = off["gate"] // bw
    d_gp = (src_gp - n_gate) // bw - j_gp
    d_aq = (src_aq - n_gate) // bw - j_aq
    assert (src_gp - n_gate) % bw == 0 and (src_aq - n_gate) % bw == 0
    br = min(1024, D)

    def a_idx(j):
        return jnp.where(j < j_gp, j, jnp.where(j < j_aq, j + d_gp, jnp.where(j < j_gate, j + d_aq, src_gate // bw)))

    def body(a_ref, b_ref, o_ref):
        j = pl.program_id(1)
        rotated = jnp.logical_and(j >= j_gp, j < j_gate)

        @pl.when(j < j_gp)
        def _():
            o_ref[...] = a_ref[...].astype(o_ref.dtype)

        @pl.when(rotated)
        def _():
            cat = jnp.concatenate([a_ref[...], b_ref[...]], axis=1)
            o_ref[...] = pltpu.roll(cat, bw + ln - n_gate, 1)[:, :bw].astype(o_ref.dtype)

        @pl.when(j == j_gate)
        def _():
            a = a_ref[...]
            lane = lax.broadcasted_iota(jnp.int32, a.shape, 1)
            o_ref[...] = jnp.where(lane < n_gate, a, 0.0).astype(o_ref.dtype)

    pipelined = [_nbytes((br, bw), F32), _nbytes((br, ln), F32), _nbytes((br, bw), BF16)]
    return pl.pallas_call(
        body,
        grid=(D // br, off["end"] // bw),
        in_specs=[pl.BlockSpec((br, bw), lambda i, j: (i, a_idx(j))),
                  pl.BlockSpec((br, ln), lambda i, j: (i, (a_idx(j) + 1) * (bw // ln)))],
        out_specs=pl.BlockSpec((br, bw), lambda i, j: (i, j)),
        out_shape=jax.ShapeDtypeStruct((D, off["end"]), BF16),
        compiler_params=pltpu.CompilerParams(
            dimension_semantics=("parallel", "parallel"),
            vmem_limit_bytes=_vmem_limit(pipelined, 3 * _nbytes((br, bw + ln), F32))),
        name="regroup_w_in",
    )(w_in, w_in)


def _prep_weights(w_in, w_branch_a, w_branch_b, w_out, w_up, w_down):
    off = _in_proj_layout(w_in.shape[0])
    w_perm = _regroup_w_in(w_in)
    o_ak = off["kv"]
    return dict(
        w_in=w_perm,
        aq_t=w_perm[:, off["aq"]:o_ak].T, v_t=w_perm[:, o_ak + A_KV_W:o_ak + 2 * A_KV_W].T,
        br_a=w_branch_a.astype(BF16), br_b=w_branch_b.astype(BF16), out=w_out.astype(BF16),
        up=w_up.astype(BF16), down=w_down.astype(BF16))


def _layer(x, pos, W, b_ig, b_fg, w_mnorm, sink, ln1_g, ln1_b, ln2_g, ln2_b, alpha, state, cache, *, bm):
    B, S, D = x.shape
    M = B * S
    L = min(CHUNK, S)
    nC = S // L
    xf = x.reshape(M, D)
    xb = xf.astype(BF16)
    bn_big = min(1024, D)
    off = _in_proj_layout(D)
    w_in = W["w_in"]

    def pos_extra(table, width):
        if bm <= S:
            per = S // bm
            return (table, (bm, width), lambda i, j, k: (i % per, 0))
        return (jnp.tile(table, (bm // S, 1)), (bm, width), lambda i, j, k: (0, 0))

    n_qkv = 2 * M_QK_W + M_V_W
    colscale = jnp.concatenate([jnp.full((1, M_QK_W), M_DQK ** -0.5, F32),
                                jnp.ones((1, M_QK_W + M_V_W), F32)], 1)
    (qkv,) = _matmul(xb, w_in, bm=bm, bn=bn_big, bk=D, epilogue=_ep_colscale, out_dtypes=[BF16],
                     extras=[(colscale, (1, bn_big), lambda i, j, k: (0, j))],
                     w_col0=off["mqkv"], n_cols=n_qkv, name="in_proj_mlstm_qkv")
    (sig,) = _matmul(xb, w_in, bm=bm, bn=bn_big, bk=D, epilogue=_ep_sigmoid, out_dtypes=[BF16],
                     w_col0=off["sig"], n_cols=M_V_W + N_BRANCH * D, name="in_proj_sigmoid_gates")
    (gates,) = _matmul(xb, w_in, bm=bm, bn=V7X_LANES, bk=D, epilogue=_ep_identity, out_dtypes=[F32],
                       w_col0=off["gate"], n_cols=V7X_LANES, name="in_proj_if_gates")
    wkv = 2 * A_KV_W
    cos_k, sin_k = _rope_tables(pos, A_KV_HEADS, A_KV_W, 1.0)
    (kv_new,) = _matmul(xb, w_in, bm=bm, bn=wkv, bk=D, epilogue=_ep_rope, out_dtypes=[F32],
                        extras=[pos_extra(cos_k, wkv), pos_extra(sin_k, wkv)],
                        w_col0=off["kv"], n_cols=wkv, name="in_proj_attn_kv")

    nCp = max(nC, V7X_SUBLANES)
    gates_t = gates[:, :2 * M_HEADS].reshape(B, nC, L, 2 * M_HEADS).transpose(0, 3, 1, 2)
    gates_t = jnp.pad(gates_t, ((0, 0), (0, 0), (0, nCp - nC), (0, 0)))
    if state is None:
        C0 = jnp.zeros((B, M_HEADS, M_DV, M_DQK), F32)
        n0 = jnp.zeros((B, M_HEADS, 1, M_DQK), F32)
        m0 = jnp.zeros((B, M_HEADS, 1, 1), F32)
    else:
        C0 = state[0].astype(F32)
        n0 = state[1].astype(F32).reshape(B, M_HEADS, 1, M_DQK)
        m0 = state[2].astype(F32).reshape(B, M_HEADS, 1, 1)
    ya, C, n_st, m = _mlstm(qkv.reshape(B, S, -1), sig.reshape(B, S, -1), gates_t, b_ig, b_fg, w_mnorm,
                            C0, n0, m0, L=L)

    kv3 = kv_new.reshape(B, S, wkv)
    k_new = kv3[:, :, :A_KV_W].reshape(B, S, A_KV_HEADS, A_HD)
    v_new = kv3[:, :, A_KV_W:].reshape(B, S, A_KV_HEADS, A_HD)
    q_scale = A_HD ** -0.5
    if cache is None:
        wq = 512
        cos_q, sin_q = _rope_tables(pos, wq // A_HD, 0, q_scale)
        if bm <= S:
            tab = lambda t: (t.T, (wq, bm), lambda i, j: (0, i % (S // bm)))
        else:
            tab = lambda t: (jnp.tile(t.T, (1, bm // S)), (wq, bm), lambda i, j: (0, 0))
        qt = _matmul_nt(W["aq_t"], xb, bm=bm, bn=wq, epilogue=_ep_rope_t, out_dtype=BF16,
                        extras=[tab(cos_q), tab(sin_q)], name="in_proj_attn_q_t")
        vt = _matmul_nt(W["v_t"], xb, bm=bm, bn=A_KV_W, epilogue=_ep_identity, out_dtype=BF16,
                        name="in_proj_attn_v_t")
        yb = _swa_prompt(qt, kv_new, vt, sink, B=B, S=S)
    else:
        wq = 512
        cos_q, sin_q = _rope_tables(pos, wq // A_HD, 0, q_scale)
        (aq,) = _matmul(xb, w_in, bm=bm, bn=wq, bk=D, epilogue=_ep_rope, out_dtypes=[BF16],
                        extras=[pos_extra(cos_q, wq), pos_extra(sin_q, wq)],
                        w_col0=off["aq"], n_cols=A_Q_W, name="in_proj_attn_q")
        ck = cache[0].astype(F32).reshape(B, WINDOW, A_KV_W)
        cv = cache[1].astype(F32).reshape(B, WINDOW, A_KV_W)
        kv_all = jnp.concatenate([jnp.concatenate([ck, cv], -1), kv3], axis=1)
        yb = _swa_sample(aq.reshape(B, S, A_Q_W), kv_all, sink).reshape(M, A_Q_W)

    merged = _merge(ya.reshape(M, M_V_W), yb, W["br_a"], W["br_b"], sig,
                    M_V_W, M_V_W + D, bm=bm, bn=bn_big)
    res1 = (xf, (bm, bn_big), lambda i, j, k: (i, j))
    (h_pre,) = _matmul(merged, W["out"], bm=bm, bn=bn_big, bk=D, epilogue=functools.partial(_ep_residual, alpha),
                       out_dtypes=[F32], extras=[res1], name="out_proj_residual")
    bm_ln = min(256, M)
    h32, h16 = _layer_norm(h_pre, ln1_g, ln1_b, bm=bm_ln, out_dtypes=[F32, BF16])
    (act,) = _matmul(h16, W["up"], bm=bm, bn=bn_big, bk=D, epilogue=_ep_relu_sq, out_dtypes=[BF16],
                     name="mlp_up_relu_sq")
    res2 = (h32, (bm, bn_big), lambda i, j, k: (i, j))
    (y_pre,) = _matmul(act, W["down"], bm=bm, bn=bn_big, bk=min(2048, act.shape[1]),
                       epilogue=functools.partial(_ep_residual, alpha), out_dtypes=[F32], extras=[res2],
                       name="mlp_down_residual")
    (y,) = _layer_norm(y_pre, ln2_g, ln2_b, bm=bm_ln, out_dtypes=[F32])
    return (y.reshape(B, S, D), k_new, v_new, C, n_st.reshape(B, M_HEADS, M_DQK), m.reshape(B, M_HEADS))


def kernel(x_prompt, x_sample, cache_swa_k, cache_swa_v, state_mlstm_C, state_mlstm_n, state_mlstm_m,
           w_in, b_igate, b_fgate, w_mnorm, attn_sink, w_branch_a, w_branch_b, w_out,
           ln1_g, ln1_b, w_up, w_down, ln2_g, ln2_b):
    depth = w_in.shape[0]
    alpha = (2.0 * depth) ** 0.25
    S, T = x_prompt.shape[1], x_sample.shape[1]
    pos_p = jnp.arange(S, dtype=jnp.int32)
    pos_s = PAST_LEN + jnp.arange(T, dtype=jnp.int32)
    xp, xs = x_prompt, x_sample
    outs_p, outs_s = [], []
    for l in range(depth):
        W = _prep_weights(w_in[l], w_branch_a[l], w_branch_b[l], w_out[l], w_up[l], w_down[l])
        shared = (W, b_igate[l], b_fgate[l], w_mnorm[l], attn_sink[l], ln1_g[l], ln1_b[l], ln2_g[l], ln2_b[l], alpha)
        xp, *rest_p = _layer(xp, pos_p, *shared, None, None, bm=min(1024, xp.shape[0] * S))
        outs_p.append(rest_p)
        xs, *rest_s = _layer(xs, pos_s, *shared,
                             (state_mlstm_C[l], state_mlstm_n[l], state_mlstm_m[l]),
                             (cache_swa_k[l], cache_swa_v[l]), bm=xs.shape[0] * T)
        outs_s.append(rest_s)
    stack = lambda outs, i: jnp.stack([o[i] for o in outs])
    pk = jnp.stack([o[0][:, -WINDOW:] for o in outs_p])
    pv = jnp.stack([o[1][:, -WINDOW:] for o in outs_p])
    return (xp, xs, pk, pv, stack(outs_p, 2), stack(outs_p, 3), stack(outs_p, 4),
            stack(outs_s, 0), stack(outs_s, 1), stack(outs_s, 2), stack(outs_s, 3), stack(outs_s, 4))
```

```python
import functools

import jax
import jax.numpy as jnp
from jax import lax
from jax.experimental import pallas as pl
from jax.experimental.pallas import tpu as pltpu

CHUNK = 64
M_HEADS = 8
M_DQK = 128
M_DV = 256
A_HEADS = 32
A_KV_HEADS = 4
A_GROUP = A_HEADS // A_KV_HEADS
A_HD = 64
WINDOW = 128
WIN_CHUNKS = WINDOW // CHUNK
ROPE_THETA = 10000.0
PAST_LEN = 1024
N_BRANCH = 2
LN_EPS = 1e-5
RMS_EPS = 1e-6

M_QK_W = M_HEADS * M_DQK
M_V_W = M_HEADS * M_DV
A_Q_W = A_HEADS * A_HD
A_KV_W = A_KV_HEADS * A_HD

V7X_VMEM_BYTES = 64 * 1024 * 1024
V7X_LANES = 128
V7X_SUBLANES = 8
VMEM_REQUEST_CAP = (V7X_VMEM_BYTES * 7) // 8

F32 = jnp.float32
BF16 = jnp.bfloat16
NT_DIMS = (((1,), (1,)), ((), ()))
TN_DIMS = (((0,), (0,)), ((), ()))


def _nbytes(shape, dtype):
    n = 1
    for s in shape:
        n *= s
    return n * jnp.dtype(dtype).itemsize


def _vmem_limit(pipelined, resident=0):
    return int(min(VMEM_REQUEST_CAP, 2 * sum(pipelined) + resident + (4 << 20)))


def _matmul(x, w, *, bm, bn, bk, epilogue, out_dtypes, extras=(), w_nk=False, w_col0=0, n_cols=None, name):
    M, K = x.shape
    N = (w.shape[0] if w_nk else w.shape[1]) if n_cols is None else n_cols
    assert M % bm == 0 and N % bn == 0 and K % bk == 0 and w_col0 % bn == 0, (M, N, K, bm, bn, bk, w_col0)
    nm, nn, nk = M // bm, N // bn, K // bk
    j0 = w_col0 // bn
    n_ex, n_out = len(extras), len(out_dtypes)

    def body(*refs):
        x_ref, w_ref = refs[0], refs[1]
        ex_refs = refs[2:2 + n_ex]
        out_refs = refs[2 + n_ex:2 + n_ex + n_out]

        def product():
            if w_nk:
                return lax.dot_general(x_ref[...], w_ref[...], NT_DIMS, preferred_element_type=F32)
            return jnp.dot(x_ref[...], w_ref[...], preferred_element_type=F32)

        def finish(acc):
            outs = epilogue(acc, *[r[...] for r in ex_refs])
            for o_ref, o in zip(out_refs, outs):
                o_ref[...] = o.astype(o_ref.dtype)

        if nk == 1:
            finish(product())
        else:
            acc_ref = refs[-1]
            k = pl.program_id(2)

            @pl.when(k == 0)
            def _():
                acc_ref[...] = product()

            @pl.when(jnp.logical_and(k > 0, k < nk - 1))
            def _():
                acc_ref[...] = acc_ref[...] + product()

            @pl.when(k == nk - 1)
            def _():
                finish(acc_ref[...] + product())

    w_spec = (pl.BlockSpec((bn, bk), lambda i, j, k: (j0 + j, k)) if w_nk
              else pl.BlockSpec((bk, bn), lambda i, j, k: (k, j0 + j)))
    in_specs = [pl.BlockSpec((bm, bk), lambda i, j, k: (i, k)), w_spec]
    in_specs += [pl.BlockSpec(bs, im) for (_, bs, im) in extras]
    out_specs = [pl.BlockSpec((bm, bn), lambda i, j, k: (i, j)) for _ in out_dtypes]
    out_shape = [jax.ShapeDtypeStruct((M, N), dt) for dt in out_dtypes]
    scratch = [pltpu.VMEM((bm, bn), F32)] if nk > 1 else []
    pipelined = [_nbytes((bm, bk), x.dtype), _nbytes((bk, bn), w.dtype)]
    pipelined += [_nbytes(bs, a.dtype) for (a, bs, _) in extras]
    pipelined += [_nbytes((bm, bn), dt) for dt in out_dtypes]
    resident = _nbytes((bm, bn), F32) * (4 if nk > 1 else 3)
    outs = pl.pallas_call(
        body,
        grid=(nm, nn, nk),
        in_specs=in_specs,
        out_specs=out_specs,
        out_shape=out_shape,
        scratch_shapes=scratch,
        compiler_params=pltpu.CompilerParams(
            dimension_semantics=("parallel", "parallel", "arbitrary"),
            vmem_limit_bytes=_vmem_limit(pipelined, resident)),
        name=name,
    )(x, w, *[a for (a, _, _) in extras])
    return outs


TOKEN_BLOCK = V7X_LANES


def _matmul_nt(wt, x, *, bm, bn, epilogue, out_dtype, extras=(), w_row0=0, n_rows=None, name):
    K = wt.shape[1]
    N = wt.shape[0] if n_rows is None else n_rows
    M = x.shape[0]
    tb = TOKEN_BLOCK
    assert M % bm == 0 and N % bn == 0 and w_row0 % bn == 0 and bm % tb == 0
    j0 = w_row0 // bn

    def body(w_ref, x_ref, *refs):
        acc = lax.dot_general(w_ref[...], x_ref[...], NT_DIMS, preferred_element_type=F32)
        (out,) = epilogue(acc, *[r[...] for r in refs[:-1]])
        o_ref = refs[-1]
        for t in range(bm // tb):
            o_ref[t] = out[:, t * tb:(t + 1) * tb].astype(o_ref.dtype)

    pipelined = [_nbytes((bn, K), wt.dtype), _nbytes((bm, K), x.dtype), _nbytes((bn, bm), out_dtype)]
    pipelined += [_nbytes(bs, a.dtype) for (a, bs, _) in extras]
    return pl.pallas_call(
        body,
        grid=(M // bm, N // bn),
        in_specs=[pl.BlockSpec((bn, K), lambda i, j: (j0 + j, 0)),
                  pl.BlockSpec((bm, K), lambda i, j: (i, 0))] + [pl.BlockSpec(bs, im) for (_, bs, im) in extras],
        out_specs=pl.BlockSpec((bm // tb, bn, tb), lambda i, j: (i, j, 0)),
        out_shape=jax.ShapeDtypeStruct((M // tb, N, tb), out_dtype),
        compiler_params=pltpu.CompilerParams(
            dimension_semantics=("parallel", "parallel"),
            vmem_limit_bytes=_vmem_limit(pipelined, 3 * _nbytes((bn, bm), F32))),
        name=name,
    )(wt, x, *[a for (a, _, _) in extras])


def _ep_scale(scale, acc):
    return (acc * scale,)


def _ep_colscale(acc, scale_row):
    return (acc * scale_row,)


def _ep_sigmoid(acc):
    return (0.5 * jnp.tanh(0.5 * acc) + 0.5,)


def _ep_identity(acc):
    return (acc,)


def _ep_relu_sq(acc):
    r = jnp.maximum(acc, 0.0)
    return (r * r,)


def _ep_rope(acc, cos, sin_signed):
    width = acc.shape[1]
    half = A_HD // 2
    lane = lax.broadcasted_iota(jnp.int32, acc.shape, 1)
    first_half = (lane % A_HD) < half
    partner = jnp.where(first_half, pltpu.roll(acc, width - half, 1), pltpu.roll(acc, half, 1))
    return (acc * cos + partner * sin_signed,)


def _ep_rope_t(acc, cos, sin_signed):
    rows = acc.shape[0]
    half = A_HD // 2
    row = lax.broadcasted_iota(jnp.int32, acc.shape, 0)
    first_half = (row % A_HD) < half
    partner = jnp.where(first_half, pltpu.roll(acc, rows - half, 0), pltpu.roll(acc, half, 0))
    return (acc * cos + partner * sin_signed,)


def _ep_residual(alpha, acc, res):
    return (alpha * res + acc,)


def _merge(ya, yb, wa, wb, gates, g0_col, g1_col, *, bm, bn):
    M, Ka = ya.shape
    Kb = yb.shape[1]
    N = wa.shape[1]
    assert M % bm == 0 and N % bn == 0 and g0_col % bn == 0 and g1_col % bn == 0
    o0, o1 = g0_col // bn, g1_col // bn

    def body(ya_ref, yb_ref, wa_ref, wb_ref, g0_ref, g1_ref, o_ref):
        a = jnp.dot(ya_ref[...], wa_ref[...], preferred_element_type=F32)
        b = jnp.dot(yb_ref[...], wb_ref[...], preferred_element_type=F32)
        o_ref[...] = (g0_ref[...].astype(F32) * a + g1_ref[...].astype(F32) * b).astype(o_ref.dtype)

    pipelined = [_nbytes((bm, Ka), BF16), _nbytes((bm, Kb), BF16), _nbytes((Ka, bn), BF16),
                 _nbytes((Kb, bn), BF16), 3 * _nbytes((bm, bn), BF16)]
    return pl.pallas_call(
        body,
        grid=(M // bm, N // bn),
        in_specs=[pl.BlockSpec((bm, Ka), lambda i, j: (i, 0)),
                  pl.BlockSpec((bm, Kb), lambda i, j: (i, 0)),
                  pl.BlockSpec((Ka, bn), lambda i, j: (0, j)),
                  pl.BlockSpec((Kb, bn), lambda i, j: (0, j)),
                  pl.BlockSpec((bm, bn), lambda i, j: (i, o0 + j)),
                  pl.BlockSpec((bm, bn), lambda i, j: (i, o1 + j))],
        out_specs=pl.BlockSpec((bm, bn), lambda i, j: (i, j)),
        out_shape=jax.ShapeDtypeStruct((M, N), BF16),
        compiler_params=pltpu.CompilerParams(
            dimension_semantics=("parallel", "parallel"),
            vmem_limit_bytes=_vmem_limit(pipelined, 2 * _nbytes((bm, bn), F32))),
        name="branch_merge",
    )(ya, yb, wa, wb, gates, gates)


def _layer_norm(x, g, b, *, bm, out_dtypes):
    M, D = x.shape
    assert M % bm == 0

    def body(x_ref, g_ref, b_ref, *o_refs):
        xf = x_ref[...]
        mu = jnp.mean(xf, axis=-1, keepdims=True)
        xc = xf - mu
        var = jnp.mean(xc * xc, axis=-1, keepdims=True)
        y = xc * lax.rsqrt(var + LN_EPS) * g_ref[...] + b_ref[...]
        for o_ref in o_refs:
            o_ref[...] = y.astype(o_ref.dtype)

    pipelined = [_nbytes((bm, D), F32)] + [_nbytes((bm, D), dt) for dt in out_dtypes]
    return pl.pallas_call(
        body,
        grid=(M // bm,),
        in_specs=[pl.BlockSpec((bm, D), lambda i: (i, 0)),
                  pl.BlockSpec((1, D), lambda i: (0, 0)),
                  pl.BlockSpec((1, D), lambda i: (0, 0))],
        out_specs=[pl.BlockSpec((bm, D), lambda i: (i, 0)) for _ in out_dtypes],
        out_shape=[jax.ShapeDtypeStruct((M, D), dt) for dt in out_dtypes],
        compiler_params=pltpu.CompilerParams(
            dimension_semantics=("parallel",),
            vmem_limit_bytes=_vmem_limit(pipelined, 2 * _nbytes((bm, D), F32))),
        name="layer_norm",
    )(x, g.reshape(1, D), b.reshape(1, D))


MLSTM_HEADS_PER_STEP = 4


def _mlstm(qkv, og, gates_t, b_ig, b_fg, w_mnorm, C0, n0, m0, *, L):
    B, S, _ = qkv.shape
    nC = S // L
    nCp = gates_t.shape[2]
    H, dk, dv = M_HEADS, M_DQK, M_DV
    hb = MLSTM_HEADS_PER_STEP
    ng = H // hb
    k_blk0 = M_QK_W // (hb * dk)
    v_blk0 = (2 * M_QK_W) // (hb * dv)

    def body(big_ref, bfg_ref, q_ref, k_ref, v_ref, og_ref, ig_ref, fg_ref, wm_ref, C0_ref, n0_ref, m0_ref,
             ya_ref, C_ref, n_ref, m_ref, b_s, ig_s):
        g = pl.program_id(1)
        r_i = lax.broadcasted_iota(jnp.int32, (L, L), 0)
        c_i = lax.broadcasted_iota(jnp.int32, (L, L), 1)
        tri_incl = (r_i <= c_i).astype(F32)
        for j in range(hb):
            ig_s[j] = ig_ref[0, j] + big_ref[g * hb + j]
            lf_all = jax.nn.log_sigmoid(fg_ref[0, j] + bfg_ref[g * hb + j])
            b_s[j] = jnp.dot(lf_all, tri_incl, precision=lax.Precision.HIGHEST, preferred_element_type=F32)
        C_ref[...] = C0_ref[...]
        n_ref[...] = n0_ref[...]
        m_ref[...] = m0_ref[...]
        eye = r_i == c_i
        causal = c_i <= r_i

        def col_of(row):
            return jnp.sum(jnp.where(eye, jnp.broadcast_to(row, (L, L)), 0.0), axis=1, keepdims=True)

        def head_chunk(j, c, rows):
            q = q_ref[0, rows, j * dk:(j + 1) * dk]
            k = k_ref[0, rows, j * dk:(j + 1) * dk]
            v = v_ref[0, rows, j * dv:(j + 1) * dv]
            b_row = b_s[j, pl.ds(c, 1), :]
            ig_row = ig_s[j, pl.ds(c, 1), :]
            m_prev = m_ref[0, j]
            C_prev = C_ref[0, j]
            n_prev = n_ref[0, j]

            b_col = col_of(b_row)
            dlog = b_col - jnp.broadcast_to(b_row, (L, L)) + jnp.broadcast_to(ig_row, (L, L))
            dlog = jnp.where(causal, dlog, -jnp.inf)
            inter = b_col + m_prev
            m_t = jnp.maximum(inter, jnp.max(dlog, axis=1, keepdims=True))
            qk = lax.dot_general(q, k, NT_DIMS, preferred_element_type=F32)
            s = qk * jnp.exp(dlog - m_t)
            a = jnp.exp(inter - m_t)
            qC = lax.dot_general(q, C_prev.astype(BF16), NT_DIMS, preferred_element_type=F32)
            sv = jnp.dot(s.astype(BF16), v, preferred_element_type=F32)
            num = a * qC + sv
            qn = jnp.sum(q.astype(F32) * n_prev, axis=1, keepdims=True)
            den = a * qn + jnp.sum(s, axis=1, keepdims=True)
            hid = num / jnp.maximum(jnp.abs(den), jnp.exp(-m_t))
            hn = hid * lax.rsqrt(jnp.mean(hid * hid, axis=1, keepdims=True) + RMS_EPS)
            hn = hn * wm_ref[:, j * dv:(j + 1) * dv]
            gate = og_ref[0, rows, j * dv:(j + 1) * dv].astype(F32)
            ya_ref[0, rows, j * dv:(j + 1) * dv] = (gate * hn).astype(ya_ref.dtype)

            bL = b_row[:, L - 1:L]
            g_row = bL - b_row + ig_row
            m_new = jnp.maximum(bL + m_prev, jnp.max(g_row, axis=1, keepdims=True))
            wk_col = col_of(jnp.exp(g_row - m_new))
            decay = jnp.exp(bL + m_prev - m_new)
            kw = k.astype(F32) * wk_col
            vk = lax.dot_general(v, kw.astype(BF16), TN_DIMS, preferred_element_type=F32)
            C_ref[0, j] = decay * C_prev + vk
            n_ref[0, j] = decay * n_prev + jnp.sum(kw, axis=0, keepdims=True)
            m_ref[0, j] = m_new

        def chunk(c, carry):
            rows = pl.ds(pl.multiple_of(c * L, L), L)
            for j in range(hb):
                head_chunk(j, c, rows)
            return carry

        lax.fori_loop(0, nC, chunk, 0)

    smem = pl.BlockSpec(memory_space=pltpu.SMEM)
    pipelined = [2 * _nbytes((S, hb * dk), BF16), 3 * _nbytes((S, hb * dv), BF16),
                 2 * _nbytes((hb, nCp, L), F32), 2 * _nbytes((hb, dv, dk), F32)]
    ya, C, n, m = pl.pallas_call(
        body,
        grid=(B, ng),
        in_specs=[smem, smem,
                  pl.BlockSpec((1, S, hb * dk), lambda b, g: (b, 0, g)),
                  pl.BlockSpec((1, S, hb * dk), lambda b, g: (b, 0, k_blk0 + g)),
                  pl.BlockSpec((1, S, hb * dv), lambda b, g: (b, 0, v_blk0 + g)),
                  pl.BlockSpec((1, S, hb * dv), lambda b, g: (b, 0, g)),
                  pl.BlockSpec((1, hb, nCp, L), lambda b, g: (b, g, 0, 0)),
                  pl.BlockSpec((1, hb, nCp, L), lambda b, g: (b, ng + g, 0, 0)),
                  pl.BlockSpec((1, hb * dv), lambda b, g: (0, g)),
                  pl.BlockSpec((1, hb, dv, dk), lambda b, g: (b, g, 0, 0)),
                  pl.BlockSpec((1, hb, 1, dk), lambda b, g: (b, g, 0, 0)),
                  pl.BlockSpec((1, hb, 1, 1), lambda b, g: (b, g, 0, 0))],
        out_specs=[pl.BlockSpec((1, S, hb * dv), lambda b, g: (b, 0, g)),
                   pl.BlockSpec((1, hb, dv, dk), lambda b, g: (b, g, 0, 0)),
                   pl.BlockSpec((1, hb, 1, dk), lambda b, g: (b, g, 0, 0)),
                   pl.BlockSpec((1, hb, 1, 1), lambda b, g: (b, g, 0, 0))],
        out_shape=[jax.ShapeDtypeStruct((B, S, M_V_W), BF16),
                   jax.ShapeDtypeStruct((B, H, dv, dk), F32),
                   jax.ShapeDtypeStruct((B, H, 1, dk), F32),
                   jax.ShapeDtypeStruct((B, H, 1, 1), F32)],
        scratch_shapes=[pltpu.VMEM((hb, nCp, L), F32), pltpu.VMEM((hb, nCp, L), F32)],
        compiler_params=pltpu.CompilerParams(
            dimension_semantics=("parallel", "parallel"),
            vmem_limit_bytes=_vmem_limit(pipelined, 4 << 20)),
        name="mlstm_chunks",
    )(b_ig, b_fg, qkv, qkv, qkv, og, gates_t, gates_t, w_mnorm.reshape(1, M_V_W), C0, n0, m0)
    return ya, C, n, m


def _mlstm_prompt(qt, k, vt, ogt, gates_p, b_ig, b_fg, wm_b, *, B, S):
    TB = TOKEN_BLOCK
    assert TB == 2 * CHUNK and S % TB == 0
    nP = S // TB
    nPp = gates_p.shape[2]
    H, dk, dv = M_HEADS, M_DQK, M_DV
    hb = MLSTM_HEADS_PER_STEP
    ng = H // hb
    dva = dv + V7X_SUBLANES
    NEG = -jnp.inf

    def body(big_ref, bfg_ref, q_ref, k_ref, v_ref, og_ref, ig_ref, fg_ref, wm_ref,
             ya_ref, C_ref, n_ref, m_ref, caug_s, row_s, u_s):
        g = pl.program_id(1)
        s_i = lax.broadcasted_iota(jnp.int32, (TB, TB), 0)
        t_i = lax.broadcasted_iota(jnp.int32, (TB, TB), 1)
        causal = s_i <= t_i
        same_chunk = (s_i // CHUNK) == (t_i // CHUNK)
        intra = jnp.logical_and(causal, same_chunk)
        eye = s_i == t_i
        tri_chunk = intra.astype(F32)
        lane_p = lax.broadcasted_iota(jnp.int32, (nPp, TB), 1)
        first_p = lane_p < CHUNK
        first_1 = lax.broadcasted_iota(jnp.int32, (1, TB), 1) < CHUNK
        for j in range(hb):
            ig = ig_ref[0, j] + big_ref[g * hb + j]
            lf = jax.nn.log_sigmoid(fg_ref[0, j] + bfg_ref[g * hb + j])
            bc = jnp.dot(lf, tri_chunk, precision=lax.Precision.HIGHEST, preferred_element_type=F32)
            b_a = jnp.broadcast_to(bc[:, CHUNK - 1:CHUNK], (nPp, TB))
            b_b = jnp.broadcast_to(bc[:, TB - 1:TB], (nPp, TB))
            bp = bc + jnp.where(first_p, 0.0, b_a)
            gl = jnp.where(first_p, b_a, b_b) - bc + ig
            gmax_a = jnp.max(jnp.where(first_p, gl, NEG), axis=1, keepdims=True)
            gmax_b = jnp.max(jnp.where(first_p, NEG, gl), axis=1, keepdims=True)
            row_s[j, 0] = bc
            row_s[j, 1] = bp
            row_s[j, 2] = ig - bp
            row_s[j, 6] = b_a + b_b
            row_s[j, 3] = b_a
            row_s[j, 4] = b_b
            row_s[j, 5] = jnp.broadcast_to(gmax_a, (nPp, TB))
            row_s[j, 7] = jnp.broadcast_to(gmax_b, (nPp, TB))
            caug_s[j] = jnp.zeros((dva, dk), F32)

        def scan(p, m_rows):
            nxt = []
            for j in range(hb):
                m0 = m_rows[j]
                r = pl.ds(p, 1)
                m1 = jnp.maximum(row_s[j, 3, r, :] + m0, row_s[j, 5, r, :])
                m2 = jnp.maximum(row_s[j, 4, r, :] + m1, row_s[j, 7, r, :])
                row_s[j, 3, r, :] = jnp.where(first_1, m0, m1)
                row_s[j, 4, r, :] = m0
                row_s[j, 5, r, :] = m2
                u_row = row_s[j, 2, r, :]
                u_col = jnp.sum(jnp.where(eye, jnp.broadcast_to(u_row, (TB, TB)), 0.0), axis=1, keepdims=True)
                u_s[j, p] = jnp.broadcast_to(u_col, (TB, TB))
                nxt.append(m2)
            return tuple(nxt)

        m_fin = lax.fori_loop(0, nP, scan, tuple(jnp.zeros((1, TB), F32) for _ in range(hb)))

        def head_block(j, p, rows):
            kp = k_ref[0, rows, j * dk:(j + 1) * dk]
            q_t = q_ref[p, j * dk:(j + 1) * dk, :]
            v_t = v_ref[p, j * dv:(j + 1) * dv, :]
            r = pl.ds(p, 1)
            bc, bp, u_row = row_s[j, 0, r, :], row_s[j, 1, r, :], row_s[j, 2, r, :]
            m_chunk, m_start, m_end, b_tot = row_s[j, 3, r, :], row_s[j, 4, r, :], row_s[j, 5, r, :], row_s[j, 6, r, :]
            d_t = bp + u_s[j, p]
            m_t = jnp.maximum(bc + m_chunk, jnp.max(jnp.where(intra, d_t, NEG), axis=0, keepdims=True))
            e_t = jnp.exp(jnp.where(causal, d_t, NEG) - m_t)
            p_t = jnp.dot(kp, q_t, preferred_element_type=F32) * e_t
            a = jnp.exp(bp + m_start - m_t)
            caug = caug_s[j]
            cq = jnp.dot(caug.astype(BF16), q_t, preferred_element_type=F32)
            num = a * cq[:dv] + jnp.dot(v_t, p_t.astype(BF16), preferred_element_type=F32)
            den = a * cq[dv:dv + 1] + jnp.sum(p_t, axis=0, keepdims=True)
            hid = num / jnp.maximum(jnp.abs(den), jnp.exp(-m_t))
            hn = hid * lax.rsqrt(jnp.mean(hid * hid, axis=0, keepdims=True) + RMS_EPS)
            hn = hn * wm_ref[j * dv:(j + 1) * dv, :] * og_ref[p, j * dv:(j + 1) * dv, :].astype(F32)
            for c0 in range(0, dv, TB):
                ya_ref[0, rows, j * dv + c0:j * dv + c0 + TB] = hn[c0:c0 + TB].T.astype(ya_ref.dtype)

            wk = jnp.exp(b_tot + u_row - m_end)
            decay = jnp.exp(b_tot + m_start - m_end)
            vw = (v_t.astype(F32) * wk).astype(BF16)
            vw = jnp.concatenate([vw, jnp.broadcast_to(wk, (V7X_SUBLANES, TB)).astype(BF16)], axis=0)
            caug_s[j] = decay * caug + jnp.dot(vw, kp, preferred_element_type=F32)

        def block(p, carry):
            rows = pl.ds(pl.multiple_of(p * TB, TB), TB)
            for j in range(hb):
                head_block(j, p, rows)
            return carry

        lax.fori_loop(0, nP, block, 0)
        for j in range(hb):
            C_ref[0, j] = caug_s[j, :dv, :]
            n_ref[0, j] = caug_s[j, dv:dv + 1, :]
            m_ref[0, j] = m_fin[j][:, :1]

    smem = pl.BlockSpec(memory_space=pltpu.SMEM)
    pipelined = [_nbytes((nP, hb * dk, TB), BF16), _nbytes((S, hb * dk), BF16), 3 * _nbytes((S, hb * dv), BF16),
                 2 * _nbytes((hb, nPp, TB), F32), _nbytes((hb * dv, TB), F32), _nbytes((hb, dv, dk), F32)]
    scratch = [pltpu.VMEM((hb, dva, dk), F32), pltpu.VMEM((hb, 8, nPp, TB), F32), pltpu.VMEM((hb, nP, TB, TB), F32)]
    resident = _nbytes((hb, dva, dk), F32) + _nbytes((hb, 8, nPp, TB), F32) + _nbytes((hb, nP, TB, TB), F32)
    ya, C, n, m = pl.pallas_call(
        body,
        grid=(B, ng),
        in_specs=[smem, smem,
                  pl.BlockSpec((nP, hb * dk, TB), lambda b, g: (b, g, 0)),
                  pl.BlockSpec((1, S, hb * dk), lambda b, g: (b, 0, g)),
                  pl.BlockSpec((nP, hb * dv, TB), lambda b, g: (b, g, 0)),
                  pl.BlockSpec((nP, hb * dv, TB), lambda b, g: (b, g, 0)),
                  pl.BlockSpec((1, hb, nPp, TB), lambda b, g: (b, g, 0, 0)),
                  pl.BlockSpec((1, hb, nPp, TB), lambda b, g: (b, ng + g, 0, 0)),
                  pl.BlockSpec((hb * dv, TB), lambda b, g: (g, 0))],
        out_specs=[pl.BlockSpec((1, S, hb * dv), lambda b, g: (b, 0, g)),
                   pl.BlockSpec((1, hb, dv, dk), lambda b, g: (b, g, 0, 0)),
                   pl.BlockSpec((1, hb, 1, dk), lambda b, g: (b, g, 0, 0)),
                   pl.BlockSpec((1, hb, 1, 1), lambda b, g: (b, g, 0, 0))],
        out_shape=[jax.ShapeDtypeStruct((B, S, M_V_W), BF16),
                   jax.ShapeDtypeStruct((B, H, dv, dk), F32),
                   jax.ShapeDtypeStruct((B, H, 1, dk), F32),
                   jax.ShapeDtypeStruct((B, H, 1, 1), F32)],
        scratch_shapes=scratch,
        compiler_params=pltpu.CompilerParams(
            dimension_semantics=("parallel", "parallel"),
            vmem_limit_bytes=_vmem_limit(pipelined, resident + (4 << 20))),
        name="mlstm_prompt_blocks",
    )(b_ig, b_fg, qt, k, vt, ogt, gates_p, gates_p, wm_b)
    return ya, C, n, m


def _sink_row(sink_ref, kvh, width):
    return jnp.concatenate([jnp.full((1, width), sink_ref[kvh * A_GROUP + g], F32) for g in range(A_GROUP)], axis=1)


def _swa_prompt(qt, kv, vt, sink, *, B, S):
    TB = 2 * CHUNK
    assert TB == V7X_LANES and S % TB == 0 and WIN_CHUNKS == 2
    nblk = S // TB
    G, KVH = A_GROUP, A_KV_HEADS

    def body(sink_ref, q_ref, kp_ref, kc_ref, vp_ref, vc_ref, o_ref):
        p_id = pl.program_id(1)
        key_chunk = lax.broadcasted_iota(jnp.int32, (2 * TB, TB), 0) // CHUNK
        q_half = lax.broadcasted_iota(jnp.int32, (2 * TB, TB), 1) // CHUNK
        valid = jnp.logical_and(key_chunk >= q_half, key_chunk <= q_half + WIN_CHUNKS)
        valid = jnp.logical_and(valid, jnp.logical_or(key_chunk >= WIN_CHUNKS, p_id > 0))
        bias = jnp.where(valid, 0.0, -jnp.inf).astype(F32)
        bias = jnp.concatenate([bias] * G, axis=1)
        zeros = jnp.zeros((A_HD, TB), q_ref.dtype)
        for kvh in range(KVH):
            pr = kvh // 2
            lanes = slice(pr * V7X_LANES, (pr + 1) * V7X_LANES)
            kc = jnp.concatenate([kp_ref[:, lanes], kc_ref[:, lanes]], axis=0).astype(BF16)
            rows = slice(kvh * A_HD, (kvh + 1) * A_HD)
            vt_w = jnp.concatenate([vp_ref[0, rows, :], vc_ref[0, rows, :]], axis=1)
            pieces = []
            for g in range(G):
                head = kvh * G + g
                qg = q_ref[0, head * A_HD:(head + 1) * A_HD, :]
                pieces.append(jnp.concatenate([zeros, qg] if kvh % 2 else [qg, zeros], axis=0))
            qz = jnp.concatenate(pieces, axis=1)
            s_t = jnp.dot(kc, qz, preferred_element_type=F32) + bias
            sk = _sink_row(sink_ref, kvh, TB)
            mx = jnp.maximum(jnp.max(s_t, axis=0, keepdims=True), sk)
            p_t = jnp.exp(s_t - mx)
            den = jnp.sum(p_t, axis=0, keepdims=True) + jnp.exp(sk - mx)
            o_t = jnp.dot(vt_w, p_t.astype(BF16), preferred_element_type=F32) / den
            for j in range(G // 2):
                two = jnp.concatenate([o_t[:, (2 * j) * TB:(2 * j + 1) * TB],
                                       o_t[:, (2 * j + 1) * TB:(2 * j + 2) * TB]], axis=0)
                col0 = (kvh * G + 2 * j) * A_HD
                o_ref[:, col0:col0 + V7X_LANES] = two.T.astype(o_ref.dtype)

    prev = lambda b, p: b * nblk + jnp.maximum(p - 1, 0)
    cur = lambda b, p: b * nblk + p
    pipelined = [2 * _nbytes((A_Q_W, TB), BF16), 2 * _nbytes((TB, A_KV_W), F32), 2 * _nbytes((A_KV_W, TB), BF16)]
    return pl.pallas_call(
        body,
        grid=(B, nblk),
        in_specs=[pl.BlockSpec(memory_space=pltpu.SMEM),
                  pl.BlockSpec((1, A_Q_W, TB), lambda b, p: (cur(b, p), 0, 0)),
                  pl.BlockSpec((TB, A_KV_W), lambda b, p: (prev(b, p), 0)),
                  pl.BlockSpec((TB, A_KV_W), lambda b, p: (cur(b, p), 0)),
                  pl.BlockSpec((1, A_KV_W, TB), lambda b, p: (prev(b, p), 0, 0)),
                  pl.BlockSpec((1, A_KV_W, TB), lambda b, p: (cur(b, p), 0, 0))],
        out_specs=pl.BlockSpec((TB, A_Q_W), lambda b, p: (cur(b, p), 0)),
        out_shape=jax.ShapeDtypeStruct((B * S, A_Q_W), BF16),
        compiler_params=pltpu.CompilerParams(
            dimension_semantics=("parallel", "arbitrary"),
            vmem_limit_bytes=_vmem_limit(pipelined, 16 << 20)),
        name="swa_prompt_attention",
    )(sink, qt, kv, kv, vt, vt)


def _swa_sample(q, kv, sink):
    B, T, _ = q.shape
    Skv = kv.shape[1]
    G, KVH, LN = A_GROUP, A_KV_HEADS, V7X_LANES
    assert 2 * A_HD == LN and T % 16 == 0

    def body(sink_ref, q_ref, kv_ref, o_ref):
        kvw = kv_ref[0]
        lo_q = lax.broadcasted_iota(jnp.int32, (T, LN), 1) < A_HD
        lo_k = lax.broadcasted_iota(jnp.int32, (Skv, LN), 1) < A_HD
        zero = jnp.zeros((T, LN), q_ref.dtype)
        for kvh in range(KVH):
            pair, odd = kvh // 2, kvh % 2
            kx = kvw[:, pair * LN:(pair + 1) * LN]
            vx = kvw[:, A_KV_W + pair * LN:A_KV_W + (pair + 1) * LN]
            kr = pltpu.roll(kx, A_HD, 1)
            vr = pltpu.roll(vx, A_HD, 1)
            k2 = (jnp.where(lo_k, kr, kx) if odd else jnp.where(lo_k, kx, kr)).astype(BF16)
            v2 = (jnp.where(lo_k, vr, vx) if odd else jnp.where(lo_k, vx, vr)).astype(BF16)
            pieces, sinks = [], []
            for g in range(G):
                head = kvh * G + g
                q2 = q_ref[0, :, (head // 2) * LN:(head // 2 + 1) * LN]
                pieces.append(jnp.where(lo_q, zero, q2) if head % 2 else jnp.where(lo_q, q2, zero))
                sinks.append(jnp.full((T, 1), sink_ref[head], F32))
            qs = jnp.concatenate(pieces, axis=0)
            sk = jnp.concatenate(sinks, axis=0)
            s = lax.dot_general(qs, k2, NT_DIMS, preferred_element_type=F32)
            mx = jnp.maximum(jnp.max(s, axis=1, keepdims=True), sk)
            p = jnp.exp(s - mx)
            den = jnp.sum(p, axis=1, keepdims=True) + jnp.exp(sk - mx)
            o = jnp.dot(p.astype(BF16), v2, preferred_element_type=F32) / den
            for j in range(G // 2):
                pair_o = jnp.where(lo_q, o[(2 * j) * T:(2 * j + 1) * T], o[(2 * j + 1) * T:(2 * j + 2) * T])
                col0 = (kvh * G + 2 * j) * A_HD
                o_ref[0, :, col0:col0 + LN] = pair_o.astype(o_ref.dtype)

    pipelined = [2 * _nbytes((T, A_Q_W), BF16), _nbytes((Skv, 2 * A_KV_W), F32)]
    return pl.pallas_call(
        body,
        grid=(B,),
        in_specs=[pl.BlockSpec(memory_space=pltpu.SMEM),
                  pl.BlockSpec((1, T, A_Q_W), lambda b: (b, 0, 0)),
                  pl.BlockSpec((1, Skv, 2 * A_KV_W), lambda b: (b, 0, 0))],
        out_specs=pl.BlockSpec((1, T, A_Q_W), lambda b: (b, 0, 0)),
        out_shape=jax.ShapeDtypeStruct((B, T, A_Q_W), BF16),
        compiler_params=pltpu.CompilerParams(
            dimension_semantics=("parallel",),
            vmem_limit_bytes=_vmem_limit(pipelined, 8 << 20)),
        name="swa_sample_attention",
    )(sink, q, kv)


def _rope_tables(pos, n_rot_heads, n_plain_cols, scale):
    half = A_HD // 2
    inv = ROPE_THETA ** (-jnp.arange(half, dtype=F32) / half)
    ang = pos.astype(F32)[:, None] * inv[None, :]
    cos, sin = jnp.cos(ang) * scale, jnp.sin(ang) * scale
    cos_h = jnp.concatenate([cos, cos], -1)
    sin_h = jnp.concatenate([-sin, sin], -1)
    n = pos.shape[0]
    cos_t = jnp.concatenate([jnp.tile(cos_h, (1, n_rot_heads)), jnp.ones((n, n_plain_cols), F32)], -1)
    sin_t = jnp.concatenate([jnp.tile(sin_h, (1, n_rot_heads)), jnp.zeros((n, n_plain_cols), F32)], -1)
    return cos_t, sin_t


W_IN_BLOCK = 512


def _in_proj_layout(d_model):
    widths = [("mqkv", 2 * M_QK_W + M_V_W), ("sig", M_V_W + N_BRANCH * d_model), ("aq", A_Q_W),
              ("kv", 2 * A_KV_W), ("gate", W_IN_BLOCK)]
    off, col = {}, 0
    for name, w in widths:
        off[name] = col
        col += w
    off["end"] = col
    return off


def _regroup_w_in_t(w_in_t):
    D = w_in_t.shape[1]
    off = _in_proj_layout(D)
    bw = W_IN_BLOCK
    n_gate = 2 * M_HEADS
    src_gate = 2 * M_QK_W + 2 * M_V_W
    src_aq = src_gate + n_gate
    src_gp = src_aq + A_Q_W + 2 * A_KV_W
    assert src_gate % bw == 0 and off["sig"] % bw == 0 and (N_BRANCH * D) % bw == 0 and n_gate <= bw
    assert src_aq % V7X_SUBLANES == 0
    j_gp = (off["sig"] + M_V_W) // bw
    j_aq = off["aq"] // bw
    j_gate = off["gate"] // bw

    sl = V7X_SUBLANES

    def src_row(j):
        tile = jnp.where(j < j_gp, j * (bw // sl),
                         jnp.where(j < j_aq, src_gp // sl + (j - j_gp) * (bw // sl),
                                   jnp.where(j < j_gate, src_aq // sl + (j - j_aq) * (bw // sl), src_gate // sl)))
        return tile * sl

    def body(a_ref, o_ref):
        j = pl.program_id(0)
        a = a_ref[...]
        row = lax.broadcasted_iota(jnp.int32, a.shape, 0)
        keep = jnp.logical_or(j < j_gate, row < n_gate)
        o_ref[...] = jnp.where(keep, a, 0.0).astype(o_ref.dtype)

    pipelined = [_nbytes((bw, D), F32), _nbytes((bw, D), BF16)]
    return pl.pallas_call(
        body,
        grid=(off["end"] // bw,),
        in_specs=[pl.BlockSpec((pl.Element(bw), pl.Element(D)), lambda j: (src_row(j), 0))],
        out_specs=pl.BlockSpec((bw, D), lambda j: (j, 0)),
        out_shape=jax.ShapeDtypeStruct((off["end"], D), BF16),
        compiler_params=pltpu.CompilerParams(
            dimension_semantics=("parallel",),
            vmem_limit_bytes=_vmem_limit(pipelined, 2 * _nbytes((bw, D), F32))),
        name="regroup_w_in",
    )(w_in_t)


def _prep_weights(w_in, w_branch_a, w_branch_b, w_out, w_up, w_down):
    return dict(
        w_in_t=_regroup_w_in_t(w_in.T),
        br_a=w_branch_a.astype(BF16), br_b=w_branch_b.astype(BF16), out=w_out.astype(BF16),
        up=w_up.astype(BF16), down=w_down.astype(BF16))


def _layer(x, pos, W, b_ig, b_fg, w_mnorm, sink, ln1_g, ln1_b, ln2_g, ln2_b, alpha, state, cache, *, bm):
    B, S, D = x.shape
    M = B * S
    L = min(CHUNK, S)
    nC = S // L
    xf = x.reshape(M, D)
    xb = xf.astype(BF16)
    bn_big = min(1024, D)
    off = _in_proj_layout(D)
    w_in_t = W["w_in_t"]

    def pos_extra(table, width):
        if bm <= S:
            per = S // bm
            return (table, (bm, width), lambda i, j, k: (i % per, 0))
        return (jnp.tile(table, (bm // S, 1)), (bm, width), lambda i, j, k: (0, 0))

    q_scale_m = M_DQK ** -0.5
    (gates,) = _matmul(xb, w_in_t, w_nk=True, bm=bm, bn=V7X_LANES, bk=D, epilogue=_ep_identity, out_dtypes=[F32],
                       w_col0=off["gate"], n_cols=V7X_LANES, name="in_proj_if_gates")
    wkv = 2 * A_KV_W
    cos_k, sin_k = _rope_tables(pos, A_KV_HEADS, A_KV_W, 1.0)
    (kv_new,) = _matmul(xb, w_in_t, w_nk=True, bm=bm, bn=wkv, bk=D, epilogue=_ep_rope, out_dtypes=[F32],
                        extras=[pos_extra(cos_k, wkv), pos_extra(sin_k, wkv)],
                        w_col0=off["kv"], n_cols=wkv, name="in_proj_attn_kv")

    def gate_rows(lanes):
        n = S // lanes
        t = gates[:, :2 * M_HEADS].reshape(B, n, lanes, 2 * M_HEADS).transpose(0, 3, 1, 2)
        return jnp.pad(t, ((0, 0), (0, 0), (0, max(n, V7X_SUBLANES) - n), (0, 0)))

    o_mo, o_gp = off["sig"], off["sig"] + M_V_W
    if state is None:
        (mk,) = _matmul(xb, w_in_t, w_nk=True, bm=bm, bn=bn_big, bk=D, epilogue=_ep_identity, out_dtypes=[BF16],
                        w_col0=M_QK_W, n_cols=M_QK_W, name="in_proj_mlstm_k")
        bn_t = min(1024, M_QK_W)
        mq_t = _matmul_nt(w_in_t, xb, bm=bm, bn=bn_t, epilogue=functools.partial(_ep_scale, q_scale_m),
                          out_dtype=BF16, w_row0=0, n_rows=M_QK_W, name="in_proj_mlstm_q_t")
        mv_t = _matmul_nt(w_in_t, xb, bm=bm, bn=bn_t, epilogue=_ep_identity, out_dtype=BF16,
                          w_row0=2 * M_QK_W, n_rows=M_V_W, name="in_proj_mlstm_v_t")
        og_t = _matmul_nt(w_in_t, xb, bm=bm, bn=bn_t, epilogue=_ep_sigmoid, out_dtype=BF16,
                          w_row0=o_mo, n_rows=M_V_W, name="in_proj_mlstm_o_t")
        (br_gates,) = _matmul(xb, w_in_t, w_nk=True, bm=bm, bn=bn_big, bk=D, epilogue=_ep_sigmoid,
                              out_dtypes=[BF16], w_col0=o_gp, n_cols=N_BRANCH * D, name="in_proj_branch_gates")
        g_cols = (0, D)
        wm_b = jnp.broadcast_to(w_mnorm[:, None], (M_V_W, TOKEN_BLOCK))
        ya, C, n_st, m = _mlstm_prompt(mq_t, mk.reshape(B, S, M_QK_W), mv_t, og_t, gate_rows(TOKEN_BLOCK),
                                       b_ig, b_fg, wm_b, B=B, S=S)
    else:
        n_qkv = 2 * M_QK_W + M_V_W
        colscale = jnp.concatenate([jnp.full((1, M_QK_W), q_scale_m, F32), jnp.ones((1, M_QK_W + M_V_W), F32)], 1)
        (qkv,) = _matmul(xb, w_in_t, w_nk=True, bm=bm, bn=bn_big, bk=D, epilogue=_ep_colscale, out_dtypes=[BF16],
                         extras=[(colscale, (1, bn_big), lambda i, j, k: (0, j))],
                         w_col0=off["mqkv"], n_cols=n_qkv, name="in_proj_mlstm_qkv")
        (br_gates,) = _matmul(xb, w_in_t, w_nk=True, bm=bm, bn=bn_big, bk=D, epilogue=_ep_sigmoid,
                              out_dtypes=[BF16], w_col0=o_mo, n_cols=M_V_W + N_BRANCH * D,
                              name="in_proj_sigmoid_gates")
        g_cols = (M_V_W, M_V_W + D)
        C0 = state[0].astype(F32)
        n0 = state[1].astype(F32).reshape(B, M_HEADS, 1, M_DQK)
        m0 = state[2].astype(F32).reshape(B, M_HEADS, 1, 1)
        ya, C, n_st, m = _mlstm(qkv.reshape(B, S, -1), br_gates.reshape(B, S, -1), gate_rows(L), b_ig, b_fg,
                                w_mnorm, C0, n0, m0, L=L)

    kv3 = kv_new.reshape(B, S, wkv)
    k_new = kv3[:, :, :A_KV_W].reshape(B, S, A_KV_HEADS, A_HD)
    v_new = kv3[:, :, A_KV_W:].reshape(B, S, A_KV_HEADS, A_HD)
    q_scale = A_HD ** -0.5
    if cache is None:
        wq = 512
        cos_q, sin_q = _rope_tables(pos, wq // A_HD, 0, q_scale)
        if bm <= S:
            tab = lambda t: (t.T, (wq, bm), lambda i, j: (0, i % (S // bm)))
        else:
            tab = lambda t: (jnp.tile(t.T, (1, bm // S)), (wq, bm), lambda i, j: (0, 0))
        qt = _matmul_nt(w_in_t, xb, bm=bm, bn=wq, epilogue=_ep_rope_t, out_dtype=BF16,
                        extras=[tab(cos_q), tab(sin_q)], w_row0=off["aq"], n_rows=A_Q_W,
                        name="in_proj_attn_q_t")
        vt = _matmul_nt(w_in_t, xb, bm=bm, bn=A_KV_W, epilogue=_ep_identity, out_dtype=BF16,
                        w_row0=off["kv"] + A_KV_W, n_rows=A_KV_W, name="in_proj_attn_v_t")
        yb = _swa_prompt(qt, kv_new, vt, sink, B=B, S=S)
    else:
        wq = 512
        cos_q, sin_q = _rope_tables(pos, wq // A_HD, 0, q_scale)
        (aq,) = _matmul(xb, w_in_t, w_nk=True,bm=bm, bn=wq, bk=D, epilogue=_ep_rope, out_dtypes=[BF16],
                        extras=[pos_extra(cos_q, wq), pos_extra(sin_q, wq)],
                        w_col0=off["aq"], n_cols=A_Q_W, name="in_proj_attn_q")
        ck = cache[0].astype(F32).reshape(B, WINDOW, A_KV_W)
        cv = cache[1].astype(F32).reshape(B, WINDOW, A_KV_W)
        kv_all = jnp.concatenate([jnp.concatenate([ck, cv], -1), kv3], axis=1)
        yb = _swa_sample(aq.reshape(B, S, A_Q_W), kv_all, sink).reshape(M, A_Q_W)

    merged = _merge(ya.reshape(M, M_V_W), yb, W["br_a"], W["br_b"], br_gates,
                    g_cols[0], g_cols[1], bm=bm, bn=bn_big)
    res1 = (xf, (bm, bn_big), lambda i, j, k: (i, j))
    (h_pre,) = _matmul(merged, W["out"], bm=bm, bn=bn_big, bk=D, epilogue=functools.partial(_ep_residual, alpha),
                       out_dtypes=[F32], extras=[res1], name="out_proj_residual")
    bm_ln = min(256, M)
    h32, h16 = _layer_norm(h_pre, ln1_g, ln1_b, bm=bm_ln, out_dtypes=[F32, BF16])
    (act,) = _matmul(h16, W["up"], bm=bm, bn=bn_big, bk=D, epilogue=_ep_relu_sq, out_dtypes=[BF16],
                     name="mlp_up_relu_sq")
    res2 = (h32, (bm, bn_big), lambda i, j, k: (i, j))
    (y_pre,) = _matmul(act, W["down"], bm=bm, bn=bn_big, bk=min(2048, act.shape[1]),
                       epilogue=functools.partial(_ep_residual, alpha), out_dtypes=[F32], extras=[res2],
                       name="mlp_down_residual")
    (y,) = _layer_norm(y_pre, ln2_g, ln2_b, bm=bm_ln, out_dtypes=[F32])
    return (y.reshape(B, S, D), k_new, v_new, C, n_st.reshape(B, M_HEADS, M_DQK), m.reshape(B, M_HEADS))


def kernel(x_prompt, x_sample, cache_swa_k, cache_swa_v, state_mlstm_C, state_mlstm_n, state_mlstm_m,
           w_in, b_igate, b_fgate, w_mnorm, attn_sink, w_branch_a, w_branch_b, w_out,
           ln1_g, ln1_b, w_up, w_down, ln2_g, ln2_b):
    depth = w_in.shape[0]
    alpha = (2.0 * depth) ** 0.25
    S, T = x_prompt.shape[1], x_sample.shape[1]
    pos_p = jnp.arange(S, dtype=jnp.int32)
    pos_s = PAST_LEN + jnp.arange(T, dtype=jnp.int32)
    xp, xs = x_prompt, x_sample
    outs_p, outs_s = [], []
    for l in range(depth):
        W = _prep_weights(w_in[l], w_branch_a[l], w_branch_b[l], w_out[l], w_up[l], w_down[l])
        shared = (W, b_igate[l], b_fgate[l], w_mnorm[l], attn_sink[l], ln1_g[l], ln1_b[l], ln2_g[l], ln2_b[l], alpha)
        xp, *rest_p = _layer(xp, pos_p, *shared, None, None, bm=min(1024, xp.shape[0] * S))
        outs_p.append(rest_p)
        xs, *rest_s = _layer(xs, pos_s, *shared,
                             (state_mlstm_C[l], state_mlstm_n[l], state_mlstm_m[l]),
                             (cache_swa_k[l], cache_swa_v[l]), bm=xs.shape[0] * T)
        outs_s.append(rest_s)
    stack = lambda outs, i: jnp.stack([o[i] for o in outs])
    pk = jnp.stack([o[0][:, -WINDOW:] for o in outs_p])
    pv = jnp.stack([o[1][:, -WINDOW:] for o in outs_p])
    return (xp, xs, pk, pv, stack(outs_p, 2), stack(outs_p, 3), stack(outs_p, 4),
            stack(outs_s, 0), stack(outs_s, 1), stack(outs_s, 2), stack(outs_s, 3), stack(outs_s, 4))
```

```python
import functools

import jax
import jax.numpy as jnp
from jax import lax
from jax.experimental import pallas as pl
from jax.experimental.pallas import tpu as pltpu

CHUNK = 64
M_HEADS = 8
M_DQK = 128
M_DV = 256
A_HEADS = 32
A_KV_HEADS = 4
A_GROUP = A_HEADS // A_KV_HEADS
A_HD = 64
WINDOW = 128
WIN_CHUNKS = WINDOW // CHUNK
ROPE_THETA = 10000.0
PAST_LEN = 1024
N_BRANCH = 2
LN_EPS = 1e-5
RMS_EPS = 1e-6

M_QK_W = M_HEADS * M_DQK
M_V_W = M_HEADS * M_DV
A_Q_W = A_HEADS * A_HD
A_KV_W = A_KV_HEADS * A_HD

V7X_VMEM_BYTES = 64 * 1024 * 1024
V7X_LANES = 128
V7X_SUBLANES = 8
VMEM_REQUEST_CAP = (V7X_VMEM_BYTES * 7) // 8

F32 = jnp.float32
BF16 = jnp.bfloat16
NT_DIMS = (((1,), (1,)), ((), ()))
TN_DIMS = (((0,), (0,)), ((), ()))


def _nbytes(shape, dtype):
    n = 1
    for s in shape:
        n *= s
    return n * jnp.dtype(dtype).itemsize


def _vmem_limit(pipelined, resident=0):
    return int(min(VMEM_REQUEST_CAP, 2 * sum(pipelined) + resident + (4 << 20)))


def _side_cast_specs(side_casts, n_steps, step_of):
    specs, shapes, nbytes = [], [], []
    for a in side_casts:
        rows, cols = a.shape
        assert rows % n_steps == 0 and (rows // n_steps) % (2 * V7X_SUBLANES) == 0, (a.shape, n_steps)
        slab = rows // n_steps
        specs.append(pl.BlockSpec((slab, cols), lambda *ids: (step_of(*ids), 0)))
        shapes.append(jax.ShapeDtypeStruct(a.shape, BF16))
        nbytes.append(_nbytes((slab, cols), F32) + _nbytes((slab, cols), BF16))
    return specs, shapes, nbytes


def _matmul(x, w, *, bm, bn, bk, epilogue, out_dtypes, extras=(), w_nk=False, w_col0=0, n_cols=None,
            emit_x_bf16=False, side_casts=(), name):
    M, K = x.shape
    N = (w.shape[0] if w_nk else w.shape[1]) if n_cols is None else n_cols
    assert M % bm == 0 and N % bn == 0 and K % bk == 0 and w_col0 % bn == 0, (M, N, K, bm, bn, bk, w_col0)
    nm, nn, nk = M // bm, N // bn, K // bk
    assert not emit_x_bf16 or nk == 1
    j0 = w_col0 // bn
    n_ex, n_out, n_sc = len(extras), len(out_dtypes), len(side_casts)
    n_xo = 1 if emit_x_bf16 else 0

    def body(*refs):
        x_ref, w_ref = refs[0], refs[1]
        ex_refs = refs[2:2 + n_ex]
        sc_in = refs[2 + n_ex:2 + n_ex + n_sc]
        o0 = 2 + n_ex + n_sc
        out_refs = refs[o0:o0 + n_out]
        sc_out = refs[o0 + n_out + n_xo:o0 + n_out + n_xo + n_sc]
        for src, dst in zip(sc_in, sc_out):
            dst[...] = src[...].astype(dst.dtype)

        def product():
            xv = x_ref[...]
            if emit_x_bf16:
                xv = xv.astype(BF16)
                refs[o0 + n_out][...] = xv
            if w_nk:
                return lax.dot_general(xv, w_ref[...], NT_DIMS, preferred_element_type=F32)
            return jnp.dot(xv, w_ref[...], preferred_element_type=F32)

        def finish(acc):
            outs = epilogue(acc, *[r[...] for r in ex_refs])
            for o_ref, o in zip(out_refs, outs):
                o_ref[...] = o.astype(o_ref.dtype)

        if nk == 1:
            finish(product())
        else:
            acc_ref = refs[-1]
            k = pl.program_id(2)

            @pl.when(k == 0)
            def _():
                acc_ref[...] = product()

            @pl.when(jnp.logical_and(k > 0, k < nk - 1))
            def _():
                acc_ref[...] = acc_ref[...] + product()

            @pl.when(k == nk - 1)
            def _():
                finish(acc_ref[...] + product())

    w_spec = (pl.BlockSpec((bn, bk), lambda i, j, k: (j0 + j, k)) if w_nk
              else pl.BlockSpec((bk, bn), lambda i, j, k: (k, j0 + j)))
    sc_specs, sc_shapes, sc_bytes = _side_cast_specs(side_casts, nm * nn * nk, lambda i, j, k: (i * nn + j) * nk + k)
    in_specs = [pl.BlockSpec((bm, bk), lambda i, j, k: (i, k)), w_spec]
    in_specs += [pl.BlockSpec(bs, im) for (_, bs, im) in extras] + sc_specs
    out_specs = [pl.BlockSpec((bm, bn), lambda i, j, k: (i, j)) for _ in out_dtypes]
    out_shape = [jax.ShapeDtypeStruct((M, N), dt) for dt in out_dtypes]
    if emit_x_bf16:
        out_specs.append(pl.BlockSpec((bm, bk), lambda i, j, k: (i, k)))
        out_shape.append(jax.ShapeDtypeStruct((M, K), BF16))
    out_specs += sc_specs
    out_shape += sc_shapes
    scratch = [pltpu.VMEM((bm, bn), F32)] if nk > 1 else []
    pipelined = [_nbytes((bm, bk), x.dtype), _nbytes((bk, bn), w.dtype)]
    pipelined += [_nbytes(bs, a.dtype) for (a, bs, _) in extras]
    pipelined += [_nbytes((bm, bn), dt) for dt in out_dtypes] + sc_bytes
    resident = _nbytes((bm, bn), F32) * (4 if nk > 1 else 3)
    if emit_x_bf16:
        pipelined.append(_nbytes((bm, bk), BF16))
        resident += _nbytes((bm, bk), BF16)
    semantics = ("arbitrary",) * 3 if (n_sc or emit_x_bf16) else ("parallel", "parallel", "arbitrary")
    outs = pl.pallas_call(
        body,
        grid=(nm, nn, nk),
        in_specs=in_specs,
        out_specs=out_specs,
        out_shape=out_shape,
        scratch_shapes=scratch,
        compiler_params=pltpu.CompilerParams(
            dimension_semantics=semantics,
            vmem_limit_bytes=_vmem_limit(pipelined, resident)),
        name=name,
    )(x, w, *[a for (a, _, _) in extras], *side_casts)
    return outs


TOKEN_BLOCK = V7X_LANES


def _matmul_nt(wt, x, *, bm, bn, epilogue, out_dtype, extras=(), w_row0=0, n_rows=None, side_casts=(), name):
    K = wt.shape[1]
    N = wt.shape[0] if n_rows is None else n_rows
    M = x.shape[0]
    tb = TOKEN_BLOCK
    assert M % bm == 0 and N % bn == 0 and w_row0 % bn == 0 and bm % tb == 0
    j0 = w_row0 // bn
    nm, nn = M // bm, N // bn
    n_ex, n_sc = len(extras), len(side_casts)

    def body(w_ref, x_ref, *refs):
        ex_refs = refs[:n_ex]
        sc_in = refs[n_ex:n_ex + n_sc]
        o_ref = refs[n_ex + n_sc]
        sc_out = refs[n_ex + n_sc + 1:]
        for src, dst in zip(sc_in, sc_out):
            dst[...] = src[...].astype(dst.dtype)
        acc = lax.dot_general(w_ref[...], x_ref[...], NT_DIMS, preferred_element_type=F32)
        (out,) = epilogue(acc, *[r[...] for r in ex_refs])
        for t in range(bm // tb):
            o_ref[t] = out[:, t * tb:(t + 1) * tb].astype(o_ref.dtype)

    sc_specs, sc_shapes, sc_bytes = _side_cast_specs(side_casts, nm * nn, lambda i, j: i * nn + j)
    pipelined = [_nbytes((bn, K), wt.dtype), _nbytes((bm, K), x.dtype), _nbytes((bn, bm), out_dtype)]
    pipelined += [_nbytes(bs, a.dtype) for (a, bs, _) in extras] + sc_bytes
    outs = pl.pallas_call(
        body,
        grid=(nm, nn),
        in_specs=[pl.BlockSpec((bn, K), lambda i, j: (j0 + j, 0)),
                  pl.BlockSpec((bm, K), lambda i, j: (i, 0))]
        + [pl.BlockSpec(bs, im) for (_, bs, im) in extras] + sc_specs,
        out_specs=[pl.BlockSpec((bm // tb, bn, tb), lambda i, j: (i, j, 0))] + sc_specs,
        out_shape=[jax.ShapeDtypeStruct((M // tb, N, tb), out_dtype)] + sc_shapes,
        compiler_params=pltpu.CompilerParams(
            dimension_semantics=("arbitrary", "arbitrary") if n_sc else ("parallel", "parallel"),
            vmem_limit_bytes=_vmem_limit(pipelined, 3 * _nbytes((bn, bm), F32))),
        name=name,
    )(wt, x, *[a for (a, _, _) in extras], *side_casts)
    return outs


def _ep_scale(scale, acc):
    return (acc * scale,)


def _ep_colscale(acc, scale_row):
    return (acc * scale_row,)


def _ep_sigmoid(acc):
    return (0.5 * jnp.tanh(0.5 * acc) + 0.5,)


def _ep_identity(acc):
    return (acc,)


def _ep_relu_sq(acc):
    r = jnp.maximum(acc, 0.0)
    return (r * r,)


def _ep_rope(acc, cos, sin_signed):
    width = acc.shape[1]
    half = A_HD // 2
    lane = lax.broadcasted_iota(jnp.int32, acc.shape, 1)
    first_half = (lane % A_HD) < half
    partner = jnp.where(first_half, pltpu.roll(acc, width - half, 1), pltpu.roll(acc, half, 1))
    return (acc * cos + partner * sin_signed,)


def _ep_rope_t(acc, cos, sin_signed):
    rows = acc.shape[0]
    half = A_HD // 2
    row = lax.broadcasted_iota(jnp.int32, acc.shape, 0)
    first_half = (row % A_HD) < half
    partner = jnp.where(first_half, pltpu.roll(acc, rows - half, 0), pltpu.roll(acc, half, 0))
    return (acc * cos + partner * sin_signed,)


def _ep_residual(alpha, acc, res):
    return (alpha * res + acc,)


def _merge(ya, yb, wa, wb, gates, g0_col, g1_col, *, bm, bn):
    M, Ka = ya.shape
    Kb = yb.shape[1]
    N = wa.shape[1]
    assert M % bm == 0 and N % bn == 0 and g0_col % bn == 0 and g1_col % bn == 0
    o0, o1 = g0_col // bn, g1_col // bn

    def body(ya_ref, yb_ref, wa_ref, wb_ref, g0_ref, g1_ref, o_ref):
        a = jnp.dot(ya_ref[...], wa_ref[...], preferred_element_type=F32)
        b = jnp.dot(yb_ref[...], wb_ref[...], preferred_element_type=F32)
        o_ref[...] = (g0_ref[...].astype(F32) * a + g1_ref[...].astype(F32) * b).astype(o_ref.dtype)

    pipelined = [_nbytes((bm, Ka), BF16), _nbytes((bm, Kb), BF16), _nbytes((Ka, bn), BF16),
                 _nbytes((Kb, bn), BF16), 3 * _nbytes((bm, bn), BF16)]
    return pl.pallas_call(
        body,
        grid=(M // bm, N // bn),
        in_specs=[pl.BlockSpec((bm, Ka), lambda i, j: (i, 0)),
                  pl.BlockSpec((bm, Kb), lambda i, j: (i, 0)),
                  pl.BlockSpec((Ka, bn), lambda i, j: (0, j)),
                  pl.BlockSpec((Kb, bn), lambda i, j: (0, j)),
                  pl.BlockSpec((bm, bn), lambda i, j: (i, o0 + j)),
                  pl.BlockSpec((bm, bn), lambda i, j: (i, o1 + j))],
        out_specs=pl.BlockSpec((bm, bn), lambda i, j: (i, j)),
        out_shape=jax.ShapeDtypeStruct((M, N), BF16),
        compiler_params=pltpu.CompilerParams(
            dimension_semantics=("parallel", "parallel"),
            vmem_limit_bytes=_vmem_limit(pipelined, 2 * _nbytes((bm, bn), F32))),
        name="branch_merge",
    )(ya, yb, wa, wb, gates, gates)


def _layer_norm(x, g, b, *, bm, out_dtypes):
    M, D = x.shape
    assert M % bm == 0

    def body(x_ref, g_ref, b_ref, *o_refs):
        xf = x_ref[...]
        mu = jnp.mean(xf, axis=-1, keepdims=True)
        xc = xf - mu
        var = jnp.mean(xc * xc, axis=-1, keepdims=True)
        y = xc * lax.rsqrt(var + LN_EPS) * g_ref[...] + b_ref[...]
        for o_ref in o_refs:
            o_ref[...] = y.astype(o_ref.dtype)

    pipelined = [_nbytes((bm, D), F32)] + [_nbytes((bm, D), dt) for dt in out_dtypes]
    return pl.pallas_call(
        body,
        grid=(M // bm,),
        in_specs=[pl.BlockSpec((bm, D), lambda i: (i, 0)),
                  pl.BlockSpec((1, D), lambda i: (0, 0)),
                  pl.BlockSpec((1, D), lambda i: (0, 0))],
        out_specs=[pl.BlockSpec((bm, D), lambda i: (i, 0)) for _ in out_dtypes],
        out_shape=[jax.ShapeDtypeStruct((M, D), dt) for dt in out_dtypes],
        compiler_params=pltpu.CompilerParams(
            dimension_semantics=("parallel",),
            vmem_limit_bytes=_vmem_limit(pipelined, 2 * _nbytes((bm, D), F32))),
        name="layer_norm",
    )(x, g.reshape(1, D), b.reshape(1, D))


MLSTM_HEADS_PER_STEP = 4


def _mlstm(qkv, og, gates_t, b_ig, b_fg, w_mnorm, C0, n0, m0, *, L):
    B, S, _ = qkv.shape
    nC = S // L
    nCp = gates_t.shape[2]
    H, dk, dv = M_HEADS, M_DQK, M_DV
    hb = MLSTM_HEADS_PER_STEP
    ng = H // hb
    k_blk0 = M_QK_W // (hb * dk)
    v_blk0 = (2 * M_QK_W) // (hb * dv)

    def body(big_ref, bfg_ref, q_ref, k_ref, v_ref, og_ref, ig_ref, fg_ref, wm_ref, C0_ref, n0_ref, m0_ref,
             ya_ref, C_ref, n_ref, m_ref, b_s, ig_s):
        g = pl.program_id(1)
        r_i = lax.broadcasted_iota(jnp.int32, (L, L), 0)
        c_i = lax.broadcasted_iota(jnp.int32, (L, L), 1)
        tri_incl = (r_i <= c_i).astype(F32)
        for j in range(hb):
            ig_s[j] = ig_ref[0, j] + big_ref[g * hb + j]
            lf_all = jax.nn.log_sigmoid(fg_ref[0, j] + bfg_ref[g * hb + j])
            b_s[j] = jnp.dot(lf_all, tri_incl, precision=lax.Precision.HIGHEST, preferred_element_type=F32)
        C_ref[...] = C0_ref[...]
        n_ref[...] = n0_ref[...]
        m_ref[...] = m0_ref[...]
        eye = r_i == c_i
        causal = c_i <= r_i

        def col_of(row):
            return jnp.sum(jnp.where(eye, jnp.broadcast_to(row, (L, L)), 0.0), axis=1, keepdims=True)

        def head_chunk(j, c, rows):
            q = q_ref[0, rows, j * dk:(j + 1) * dk]
            k = k_ref[0, rows, j * dk:(j + 1) * dk]
            v = v_ref[0, rows, j * dv:(j + 1) * dv]
            b_row = b_s[j, pl.ds(c, 1), :]
            ig_row = ig_s[j, pl.ds(c, 1), :]
            m_prev = m_ref[0, j]
            C_prev = C_ref[0, j]
            n_prev = n_ref[0, j]

            b_col = col_of(b_row)
            dlog = b_col - jnp.broadcast_to(b_row, (L, L)) + jnp.broadcast_to(ig_row, (L, L))
            dlog = jnp.where(causal, dlog, -jnp.inf)
            inter = b_col + m_prev
            m_t = jnp.maximum(inter, jnp.max(dlog, axis=1, keepdims=True))
            qk = lax.dot_general(q, k, NT_DIMS, preferred_element_type=F32)
            s = qk * jnp.exp(dlog - m_t)
            a = jnp.exp(inter - m_t)
            qC = lax.dot_general(q, C_prev.astype(BF16), NT_DIMS, preferred_element_type=F32)
            sv = jnp.dot(s.astype(BF16), v, preferred_element_type=F32)
            num = a * qC + sv
            qn = jnp.sum(q.astype(F32) * n_prev, axis=1, keepdims=True)
            den = a * qn + jnp.sum(s, axis=1, keepdims=True)
            hid = num / jnp.maximum(jnp.abs(den), jnp.exp(-m_t))
            hn = hid * lax.rsqrt(jnp.mean(hid * hid, axis=1, keepdims=True) + RMS_EPS)
            hn = hn * wm_ref[:, j * dv:(j + 1) * dv]
            gate = og_ref[0, rows, j * dv:(j + 1) * dv].astype(F32)
            ya_ref[0, rows, j * dv:(j + 1) * dv] = (gate * hn).astype(ya_ref.dtype)

            bL = b_row[:, L - 1:L]
            g_row = bL - b_row + ig_row
            m_new = jnp.maximum(bL + m_prev, jnp.max(g_row, axis=1, keepdims=True))
            wk_col = col_of(jnp.exp(g_row - m_new))
            decay = jnp.exp(bL + m_prev - m_new)
            kw = k.astype(F32) * wk_col
            vk = lax.dot_general(v, kw.astype(BF16), TN_DIMS, preferred_element_type=F32)
            C_ref[0, j] = decay * C_prev + vk
            n_ref[0, j] = decay * n_prev + jnp.sum(kw, axis=0, keepdims=True)
            m_ref[0, j] = m_new

        def chunk(c, carry):
            rows = pl.ds(pl.multiple_of(c * L, L), L)
            for j in range(hb):
                head_chunk(j, c, rows)
            return carry

        lax.fori_loop(0, nC, chunk, 0)

    smem = pl.BlockSpec(memory_space=pltpu.SMEM)
    pipelined = [2 * _nbytes((S, hb * dk), BF16), 3 * _nbytes((S, hb * dv), BF16),
                 2 * _nbytes((hb, nCp, L), F32), 2 * _nbytes((hb, dv, dk), F32)]
    ya, C, n, m = pl.pallas_call(
        body,
        grid=(B, ng),
        in_specs=[smem, smem,
                  pl.BlockSpec((1, S, hb * dk), lambda b, g: (b, 0, g)),
                  pl.BlockSpec((1, S, hb * dk), lambda b, g: (b, 0, k_blk0 + g)),
                  pl.BlockSpec((1, S, hb * dv), lambda b, g: (b, 0, v_blk0 + g)),
                  pl.BlockSpec((1, S, hb * dv), lambda b, g: (b, 0, g)),
                  pl.BlockSpec((1, hb, nCp, L), lambda b, g: (b, g, 0, 0)),
                  pl.BlockSpec((1, hb, nCp, L), lambda b, g: (b, ng + g, 0, 0)),
                  pl.BlockSpec((1, hb * dv), lambda b, g: (0, g)),
                  pl.BlockSpec((1, hb, dv, dk), lambda b, g: (b, g, 0, 0)),
                  pl.BlockSpec((1, hb, 1, dk), lambda b, g: (b, g, 0, 0)),
                  pl.BlockSpec((1, hb, 1, 1), lambda b, g: (b, g, 0, 0))],
        out_specs=[pl.BlockSpec((1, S, hb * dv), lambda b, g: (b, 0, g)),
                   pl.BlockSpec((1, hb, dv, dk), lambda b, g: (b, g, 0, 0)),
                   pl.BlockSpec((1, hb, 1, dk), lambda b, g: (b, g, 0, 0)),
                   pl.BlockSpec((1, hb, 1, 1), lambda b, g: (b, g, 0, 0))],
        out_shape=[jax.ShapeDtypeStruct((B, S, M_V_W), BF16),
                   jax.ShapeDtypeStruct((B, H, dv, dk), F32),
                   jax.ShapeDtypeStruct((B, H, 1, dk), F32),
                   jax.ShapeDtypeStruct((B, H, 1, 1), F32)],
        scratch_shapes=[pltpu.VMEM((hb, nCp, L), F32), pltpu.VMEM((hb, nCp, L), F32)],
        compiler_params=pltpu.CompilerParams(
            dimension_semantics=("parallel", "parallel"),
            vmem_limit_bytes=_vmem_limit(pipelined, 4 << 20)),
        name="mlstm_chunks",
    )(b_ig, b_fg, qkv, qkv, qkv, og, gates_t, gates_t, w_mnorm.reshape(1, M_V_W), C0, n0, m0)
    return ya, C, n, m


def _mlstm_prompt(qt, k, vt, ogt, gates_p, b_ig, b_fg, wm_b, *, B, S):
    TB = TOKEN_BLOCK
    assert TB == 2 * CHUNK and S % TB == 0
    nP = S // TB
    nPp = gates_p.shape[2]
    H, dk, dv = M_HEADS, M_DQK, M_DV
    hb = MLSTM_HEADS_PER_STEP
    ng = H // hb
    dva = dv + V7X_SUBLANES
    NEG = -jnp.inf

    def body(big_ref, bfg_ref, q_ref, k_ref, v_ref, og_ref, ig_ref, fg_ref, wm_ref,
             ya_ref, C_ref, n_ref, m_ref, caug_s, row_s, u_s):
        g = pl.program_id(1)
        s_i = lax.broadcasted_iota(jnp.int32, (TB, TB), 0)
        t_i = lax.broadcasted_iota(jnp.int32, (TB, TB), 1)
        causal = s_i <= t_i
        same_chunk = (s_i // CHUNK) == (t_i // CHUNK)
        intra = jnp.logical_and(causal, same_chunk)
        eye = s_i == t_i
        tri_chunk = intra.astype(F32)
        lane_p = lax.broadcasted_iota(jnp.int32, (nPp, TB), 1)
        first_p = lane_p < CHUNK
        first_1 = lax.broadcasted_iota(jnp.int32, (1, TB), 1) < CHUNK
        for j in range(hb):
            ig = ig_ref[0, j] + big_ref[g * hb + j]
            lf = jax.nn.log_sigmoid(fg_ref[0, j] + bfg_ref[g * hb + j])
            bc = jnp.dot(lf, tri_chunk, precision=lax.Precision.HIGHEST, preferred_element_type=F32)
            b_a = jnp.broadcast_to(bc[:, CHUNK - 1:CHUNK], (nPp, TB))
            b_b = jnp.broadcast_to(bc[:, TB - 1:TB], (nPp, TB))
            bp = bc + jnp.where(first_p, 0.0, b_a)
            gl = jnp.where(first_p, b_a, b_b) - bc + ig
            gmax_a = jnp.max(jnp.where(first_p, gl, NEG), axis=1, keepdims=True)
            gmax_b = jnp.max(jnp.where(first_p, NEG, gl), axis=1, keepdims=True)
            row_s[j, 0] = bc
            row_s[j, 1] = bp
            row_s[j, 2] = ig - bp
            row_s[j, 6] = b_a + b_b
            row_s[j, 3] = b_a
            row_s[j, 4] = b_b
            row_s[j, 5] = jnp.broadcast_to(gmax_a, (nPp, TB))
            row_s[j, 7] = jnp.broadcast_to(gmax_b, (nPp, TB))
            caug_s[j] = jnp.zeros((dva, dk), F32)

        def scan(p, m_rows):
            nxt = []
            for j in range(hb):
                m0 = m_rows[j]
                r = pl.ds(p, 1)
                m1 = jnp.maximum(row_s[j, 3, r, :] + m0, row_s[j, 5, r, :])
                m2 = jnp.maximum(row_s[j, 4, r, :] + m1, row_s[j, 7, r, :])
                row_s[j, 3, r, :] = jnp.where(first_1, m0, m1)
                row_s[j, 4, r, :] = m0
                row_s[j, 5, r, :] = m2
                u_row = row_s[j, 2, r, :]
                u_col = jnp.sum(jnp.where(eye, jnp.broadcast_to(u_row, (TB, TB)), 0.0), axis=1, keepdims=True)
                u_s[j, p] = jnp.broadcast_to(u_col, (TB, TB))
                nxt.append(m2)
            return tuple(nxt)

        m_fin = lax.fori_loop(0, nP, scan, tuple(jnp.zeros((1, TB), F32) for _ in range(hb)))

        def head_block(j, p, rows):
            kp = k_ref[0, rows, j * dk:(j + 1) * dk]
            q_t = q_ref[p, j * dk:(j + 1) * dk, :]
            v_t = v_ref[p, j * dv:(j + 1) * dv, :]
            r = pl.ds(p, 1)
            bc, bp, u_row = row_s[j, 0, r, :], row_s[j, 1, r, :], row_s[j, 2, r, :]
            m_chunk, m_start, m_end, b_tot = row_s[j, 3, r, :], row_s[j, 4, r, :], row_s[j, 5, r, :], row_s[j, 6, r, :]
            d_t = bp + u_s[j, p]
            m_t = jnp.maximum(bc + m_chunk, jnp.max(jnp.where(intra, d_t, NEG), axis=0, keepdims=True))
            e_t = jnp.exp(jnp.where(causal, d_t, NEG) - m_t)
            p_t = jnp.dot(kp, q_t, preferred_element_type=F32) * e_t
            a = jnp.exp(bp + m_start - m_t)
            caug = caug_s[j]
            cq = jnp.dot(caug.astype(BF16), q_t, preferred_element_type=F32)
            num = a * cq[:dv] + jnp.dot(v_t, p_t.astype(BF16), preferred_element_type=F32)
            den = a * cq[dv:dv + 1] + jnp.sum(p_t, axis=0, keepdims=True)
            hid = num / jnp.maximum(jnp.abs(den), jnp.exp(-m_t))
            hn = hid * lax.rsqrt(jnp.mean(hid * hid, axis=0, keepdims=True) + RMS_EPS)
            hn = hn * wm_ref[j * dv:(j + 1) * dv, :] * og_ref[p, j * dv:(j + 1) * dv, :].astype(F32)
            for c0 in range(0, dv, TB):
                ya_ref[0, rows, j * dv + c0:j * dv + c0 + TB] = hn[c0:c0 + TB].T.astype(ya_ref.dtype)

            wk = jnp.exp(b_tot + u_row - m_end)
            decay = jnp.exp(b_tot + m_start - m_end)
            vw = (v_t.astype(F32) * wk).astype(BF16)
            vw = jnp.concatenate([vw, jnp.broadcast_to(wk, (V7X_SUBLANES, TB)).astype(BF16)], axis=0)
            caug_s[j] = decay * caug + jnp.dot(vw, kp, preferred_element_type=F32)

        def block(p, carry):
            rows = pl.ds(pl.multiple_of(p * TB, TB), TB)
            for j in range(hb):
                head_block(j, p, rows)
            return carry

        lax.fori_loop(0, nP, block, 0)
        for j in range(hb):
            C_ref[0, j] = caug_s[j, :dv, :]
            n_ref[0, j] = caug_s[j, dv:dv + 1, :]
            m_ref[0, j] = m_fin[j][:, :1]

    smem = pl.BlockSpec(memory_space=pltpu.SMEM)
    pipelined = [_nbytes((nP, hb * dk, TB), BF16), _nbytes((S, hb * dk), BF16), 3 * _nbytes((S, hb * dv), BF16),
                 2 * _nbytes((hb, nPp, TB), F32), _nbytes((hb * dv, TB), F32), _nbytes((hb, dv, dk), F32)]
    scratch = [pltpu.VMEM((hb, dva, dk), F32), pltpu.VMEM((hb, 8, nPp, TB), F32), pltpu.VMEM((hb, nP, TB, TB), F32)]
    resident = _nbytes((hb, dva, dk), F32) + _nbytes((hb, 8, nPp, TB), F32) + _nbytes((hb, nP, TB, TB), F32)
    ya, C, n, m = pl.pallas_call(
        body,
        grid=(B, ng),
        in_specs=[smem, smem,
                  pl.BlockSpec((nP, hb * dk, TB), lambda b, g: (b, g, 0)),
                  pl.BlockSpec((1, S, hb * dk), lambda b, g: (b, 0, g)),
                  pl.BlockSpec((nP, hb * dv, TB), lambda b, g: (b, g, 0)),
                  pl.BlockSpec((nP, hb * dv, TB), lambda b, g: (b, g, 0)),
                  pl.BlockSpec((1, hb, nPp, TB), lambda b, g: (b, g, 0, 0)),
                  pl.BlockSpec((1, hb, nPp, TB), lambda b, g: (b, ng + g, 0, 0)),
                  pl.BlockSpec((hb * dv, TB), lambda b, g: (g, 0))],
        out_specs=[pl.BlockSpec((1, S, hb * dv), lambda b, g: (b, 0, g)),
                   pl.BlockSpec((1, hb, dv, dk), lambda b, g: (b, g, 0, 0)),
                   pl.BlockSpec((1, hb, 1, dk), lambda b, g: (b, g, 0, 0)),
                   pl.BlockSpec((1, hb, 1, 1), lambda b, g: (b, g, 0, 0))],
        out_shape=[jax.ShapeDtypeStruct((B, S, M_V_W), BF16),
                   jax.ShapeDtypeStruct((B, H, dv, dk), F32),
                   jax.ShapeDtypeStruct((B, H, 1, dk), F32),
                   jax.ShapeDtypeStruct((B, H, 1, 1), F32)],
        scratch_shapes=scratch,
        compiler_params=pltpu.CompilerParams(
            dimension_semantics=("parallel", "parallel"),
            vmem_limit_bytes=_vmem_limit(pipelined, resident + (4 << 20))),
        name="mlstm_prompt_blocks",
    )(b_ig, b_fg, qt, k, vt, ogt, gates_p, gates_p, wm_b)
    return ya, C, n, m


def _sink_row(sink_ref, kvh, width):
    return jnp.concatenate([jnp.full((1, width), sink_ref[kvh * A_GROUP + g], F32) for g in range(A_GROUP)], axis=1)


def _swa_prompt(qt, kv, vt, sink, *, B, S):
    TB = 2 * CHUNK
    assert TB == V7X_LANES and S % TB == 0 and WIN_CHUNKS == 2
    nblk = S // TB
    G, KVH = A_GROUP, A_KV_HEADS

    def body(sink_ref, q_ref, kp_ref, kc_ref, vp_ref, vc_ref, o_ref):
        p_id = pl.program_id(1)
        key_chunk = lax.broadcasted_iota(jnp.int32, (2 * TB, TB), 0) // CHUNK
        q_half = lax.broadcasted_iota(jnp.int32, (2 * TB, TB), 1) // CHUNK
        valid = jnp.logical_and(key_chunk >= q_half, key_chunk <= q_half + WIN_CHUNKS)
        valid = jnp.logical_and(valid, jnp.logical_or(key_chunk >= WIN_CHUNKS, p_id > 0))
        bias = jnp.where(valid, 0.0, -jnp.inf).astype(F32)
        bias = jnp.concatenate([bias] * G, axis=1)
        zeros = jnp.zeros((A_HD, TB), q_ref.dtype)
        for kvh in range(KVH):
            pr = kvh // 2
            lanes = slice(pr * V7X_LANES, (pr + 1) * V7X_LANES)
            kc = jnp.concatenate([kp_ref[:, lanes], kc_ref[:, lanes]], axis=0).astype(BF16)
            rows = slice(kvh * A_HD, (kvh + 1) * A_HD)
            vt_w = jnp.concatenate([vp_ref[0, rows, :], vc_ref[0, rows, :]], axis=1)
            pieces = []
            for g in range(G):
                head = kvh * G + g
                qg = q_ref[0, head * A_HD:(head + 1) * A_HD, :]
                pieces.append(jnp.concatenate([zeros, qg] if kvh % 2 else [qg, zeros], axis=0))
            qz = jnp.concatenate(pieces, axis=1)
            s_t = jnp.dot(kc, qz, preferred_element_type=F32) + bias
            sk = _sink_row(sink_ref, kvh, TB)
            mx = jnp.maximum(jnp.max(s_t, axis=0, keepdims=True), sk)
            p_t = jnp.exp(s_t - mx)
            den = jnp.sum(p_t, axis=0, keepdims=True) + jnp.exp(sk - mx)
            o_t = jnp.dot(vt_w, p_t.astype(BF16), preferred_element_type=F32) / den
            for j in range(G // 2):
                two = jnp.concatenate([o_t[:, (2 * j) * TB:(2 * j + 1) * TB],
                                       o_t[:, (2 * j + 1) * TB:(2 * j + 2) * TB]], axis=0)
                col0 = (kvh * G + 2 * j) * A_HD
                o_ref[:, col0:col0 + V7X_LANES] = two.T.astype(o_ref.dtype)

    prev = lambda b, p: b * nblk + jnp.maximum(p - 1, 0)
    cur = lambda b, p: b * nblk + p
    pipelined = [2 * _nbytes((A_Q_W, TB), BF16), 2 * _nbytes((TB, A_KV_W), F32), 2 * _nbytes((A_KV_W, TB), BF16)]
    return pl.pallas_call(
        body,
        grid=(B, nblk),
        in_specs=[pl.BlockSpec(memory_space=pltpu.SMEM),
                  pl.BlockSpec((1, A_Q_W, TB), lambda b, p: (cur(b, p), 0, 0)),
                  pl.BlockSpec((TB, A_KV_W), lambda b, p: (prev(b, p), 0)),
                  pl.BlockSpec((TB, A_KV_W), lambda b, p: (cur(b, p), 0)),
                  pl.BlockSpec((1, A_KV_W, TB), lambda b, p: (prev(b, p), 0, 0)),
                  pl.BlockSpec((1, A_KV_W, TB), lambda b, p: (cur(b, p), 0, 0))],
        out_specs=pl.BlockSpec((TB, A_Q_W), lambda b, p: (cur(b, p), 0)),
        out_shape=jax.ShapeDtypeStruct((B * S, A_Q_W), BF16),
        compiler_params=pltpu.CompilerParams(
            dimension_semantics=("parallel", "arbitrary"),
            vmem_limit_bytes=_vmem_limit(pipelined, 16 << 20)),
        name="swa_prompt_attention",
    )(sink, qt, kv, kv, vt, vt)


def _swa_sample(q, kv, sink):
    B, T, _ = q.shape
    Skv = kv.shape[1]
    G, KVH, LN = A_GROUP, A_KV_HEADS, V7X_LANES
    assert 2 * A_HD == LN and T % 16 == 0

    def body(sink_ref, q_ref, kv_ref, o_ref):
        kvw = kv_ref[0]
        lo_q = lax.broadcasted_iota(jnp.int32, (T, LN), 1) < A_HD
        lo_k = lax.broadcasted_iota(jnp.int32, (Skv, LN), 1) < A_HD
        zero = jnp.zeros((T, LN), q_ref.dtype)
        for kvh in range(KVH):
            pair, odd = kvh // 2, kvh % 2
            kx = kvw[:, pair * LN:(pair + 1) * LN]
            vx = kvw[:, A_KV_W + pair * LN:A_KV_W + (pair + 1) * LN]
            kr = pltpu.roll(kx, A_HD, 1)
            vr = pltpu.roll(vx, A_HD, 1)
            k2 = (jnp.where(lo_k, kr, kx) if odd else jnp.where(lo_k, kx, kr)).astype(BF16)
            v2 = (jnp.where(lo_k, vr, vx) if odd else jnp.where(lo_k, vx, vr)).astype(BF16)
            pieces, sinks = [], []
            for g in range(G):
                head = kvh * G + g
                q2 = q_ref[0, :, (head // 2) * LN:(head // 2 + 1) * LN]
                pieces.append(jnp.where(lo_q, zero, q2) if head % 2 else jnp.where(lo_q, q2, zero))
                sinks.append(jnp.full((T, 1), sink_ref[head], F32))
            qs = jnp.concatenate(pieces, axis=0)
            sk = jnp.concatenate(sinks, axis=0)
            s = lax.dot_general(qs, k2, NT_DIMS, preferred_element_type=F32)
            mx = jnp.maximum(jnp.max(s, axis=1, keepdims=True), sk)
            p = jnp.exp(s - mx)
            den = jnp.sum(p, axis=1, keepdims=True) + jnp.exp(sk - mx)
            o = jnp.dot(p.astype(BF16), v2, preferred_element_type=F32) / den
            for j in range(G // 2):
                pair_o = jnp.where(lo_q, o[(2 * j) * T:(2 * j + 1) * T], o[(2 * j + 1) * T:(2 * j + 2) * T])
                col0 = (kvh * G + 2 * j) * A_HD
                o_ref[0, :, col0:col0 + LN] = pair_o.astype(o_ref.dtype)

    pipelined = [2 * _nbytes((T, A_Q_W), BF16), _nbytes((Skv, 2 * A_KV_W), F32)]
    return pl.pallas_call(
        body,
        grid=(B,),
        in_specs=[pl.BlockSpec(memory_space=pltpu.SMEM),
                  pl.BlockSpec((1, T, A_Q_W), lambda b: (b, 0, 0)),
                  pl.BlockSpec((1, Skv, 2 * A_KV_W), lambda b: (b, 0, 0))],
        out_specs=pl.BlockSpec((1, T, A_Q_W), lambda b: (b, 0, 0)),
        out_shape=jax.ShapeDtypeStruct((B, T, A_Q_W), BF16),
        compiler_params=pltpu.CompilerParams(
            dimension_semantics=("parallel",),
            vmem_limit_bytes=_vmem_limit(pipelined, 8 << 20)),
        name="swa_sample_attention",
    )(sink, q, kv)


def _rope_tables(pos, n_rot_heads, n_plain_cols, scale):
    half = A_HD // 2
    inv = ROPE_THETA ** (-jnp.arange(half, dtype=F32) / half)
    ang = pos.astype(F32)[:, None] * inv[None, :]
    cos, sin = jnp.cos(ang) * scale, jnp.sin(ang) * scale
    cos_h = jnp.concatenate([cos, cos], -1)
    sin_h = jnp.concatenate([-sin, sin], -1)
    n = pos.shape[0]
    cos_t = jnp.concatenate([jnp.tile(cos_h, (1, n_rot_heads)), jnp.ones((n, n_plain_cols), F32)], -1)
    sin_t = jnp.concatenate([jnp.tile(sin_h, (1, n_rot_heads)), jnp.zeros((n, n_plain_cols), F32)], -1)
    return cos_t, sin_t


W_IN_BLOCK = 512


def _in_proj_layout(d_model):
    widths = [("mqkv", 2 * M_QK_W + M_V_W), ("sig", M_V_W + N_BRANCH * d_model), ("aq", A_Q_W),
              ("kv", 2 * A_KV_W), ("gate", W_IN_BLOCK)]
    off, col = {}, 0
    for name, w in widths:
        off[name] = col
        col += w
    off["end"] = col
    return off


def _regroup_w_in_t(w_in_t):
    D = w_in_t.shape[1]
    off = _in_proj_layout(D)
    bw = W_IN_BLOCK
    n_gate = 2 * M_HEADS
    src_gate = 2 * M_QK_W + 2 * M_V_W
    src_aq = src_gate + n_gate
    src_gp = src_aq + A_Q_W + 2 * A_KV_W
    assert src_gate % bw == 0 and off["sig"] % bw == 0 and (N_BRANCH * D) % bw == 0 and n_gate <= bw
    assert src_aq % V7X_SUBLANES == 0
    j_gp = (off["sig"] + M_V_W) // bw
    j_aq = off["aq"] // bw
    j_gate = off["gate"] // bw

    sl = V7X_SUBLANES

    def src_row(j):
        tile = jnp.where(j < j_gp, j * (bw // sl),
                         jnp.where(j < j_aq, src_gp // sl + (j - j_gp) * (bw // sl),
                                   jnp.where(j < j_gate, src_aq // sl + (j - j_aq) * (bw // sl), src_gate // sl)))
        return tile * sl

    def body(a_ref, o_ref):
        j = pl.program_id(0)
        a = a_ref[...]
        row = lax.broadcasted_iota(jnp.int32, a.shape, 0)
        keep = jnp.logical_or(j < j_gate, row < n_gate)
        o_ref[...] = jnp.where(keep, a, 0.0).astype(o_ref.dtype)

    pipelined = [_nbytes((bw, D), F32), _nbytes((bw, D), BF16)]
    return pl.pallas_call(
        body,
        grid=(off["end"] // bw,),
        in_specs=[pl.BlockSpec((pl.Element(bw), pl.Element(D)), lambda j: (src_row(j), 0))],
        out_specs=pl.BlockSpec((bw, D), lambda j: (j, 0)),
        out_shape=jax.ShapeDtypeStruct((off["end"], D), BF16),
        compiler_params=pltpu.CompilerParams(
            dimension_semantics=("parallel",),
            vmem_limit_bytes=_vmem_limit(pipelined, 2 * _nbytes((bw, D), F32))),
        name="regroup_w_in",
    )(w_in_t)


DENSE_WEIGHTS = ("br_a", "br_b", "out", "up", "down")


def _layer(x, pos, W, b_ig, b_fg, w_mnorm, sink, ln1_g, ln1_b, ln2_g, ln2_b, alpha, state, cache, *, bm):
    B, S, D = x.shape
    M = B * S
    L = min(CHUNK, S)
    xf = x.reshape(M, D)
    bn_big = min(1024, D)
    off = _in_proj_layout(D)
    w_in_t = W["w_in_t"]
    Wb = {name: W[name] for name in DENSE_WEIGHTS if W[name].dtype == BF16}

    def pos_extra(table, rows, width):
        if rows <= S:
            per = S // rows
            return (table, (rows, width), lambda i, j, k: (i % per, j))
        return (jnp.tile(table, (rows // S, 1)), (rows, width), lambda i, j, k: (0, j))

    q_scale_m = M_DQK ** -0.5
    wkv = 2 * A_KV_W
    bm0, bn0 = min(512, M), 2 * V7X_LANES
    n_first = wkv + bn0
    cos_k, sin_k = _rope_tables(pos, A_KV_HEADS, n_first - A_KV_W, 1.0)
    kvg, xb = _matmul(xf, w_in_t, w_nk=True, bm=bm0, bn=bn0, bk=D, epilogue=_ep_rope, out_dtypes=[F32],
                      extras=[pos_extra(cos_k, bm0, bn0), pos_extra(sin_k, bm0, bn0)],
                      w_col0=off["kv"], n_cols=n_first, emit_x_bf16=True, name="in_proj_attn_kv_if_gates")
    gates = kvg[:, wkv:wkv + 2 * M_HEADS]

    def gate_rows(lanes):
        n = S // lanes
        t = gates.reshape(B, n, lanes, 2 * M_HEADS).transpose(0, 3, 1, 2)
        return jnp.pad(t, ((0, 0), (0, 0), (0, max(n, V7X_SUBLANES) - n), (0, 0)))

    def side(name):
        return [] if name in Wb else [W[name]]

    def keep(name, casts):
        if casts:
            Wb[name] = casts[0]

    o_mo, o_gp = off["sig"], off["sig"] + M_V_W
    if state is None:
        mk, *c = _matmul(xb, w_in_t, w_nk=True, bm=bm, bn=bn_big, bk=D, epilogue=_ep_identity, out_dtypes=[BF16],
                         w_col0=M_QK_W, n_cols=M_QK_W, side_casts=side("br_a"), name="in_proj_mlstm_k")
        keep("br_a", c)
        bn_t = min(1024, M_QK_W)
        mq_t, *c = _matmul_nt(w_in_t, xb, bm=bm, bn=bn_t, epilogue=functools.partial(_ep_scale, q_scale_m),
                              out_dtype=BF16, w_row0=0, n_rows=M_QK_W, side_casts=side("br_b"),
                              name="in_proj_mlstm_q_t")
        keep("br_b", c)
        mv_t, *c = _matmul_nt(w_in_t, xb, bm=bm, bn=bn_t, epilogue=_ep_identity, out_dtype=BF16,
                              w_row0=2 * M_QK_W, n_rows=M_V_W, side_casts=side("out"),
                              name="in_proj_mlstm_v_t")
        keep("out", c)
        (og_t,) = _matmul_nt(w_in_t, xb, bm=bm, bn=bn_t, epilogue=_ep_sigmoid, out_dtype=BF16,
                             w_row0=o_mo, n_rows=M_V_W, name="in_proj_mlstm_o_t")
        br_gates, *c = _matmul(xb, w_in_t, w_nk=True, bm=bm, bn=bn_big, bk=D, epilogue=_ep_sigmoid,
                               out_dtypes=[BF16], w_col0=o_gp, n_cols=N_BRANCH * D, side_casts=side("up"),
                               name="in_proj_branch_gates")
        keep("up", c)
        g_cols = (0, D)
        wm_b = jnp.broadcast_to(w_mnorm[:, None], (M_V_W, TOKEN_BLOCK))
        ya, C, n_st, m = _mlstm_prompt(mq_t, mk.reshape(B, S, M_QK_W), mv_t, og_t, gate_rows(TOKEN_BLOCK),
                                       b_ig, b_fg, wm_b, B=B, S=S)
    else:
        n_qkv = 2 * M_QK_W + M_V_W
        colscale = jnp.concatenate([jnp.full((1, M_QK_W), q_scale_m, F32), jnp.ones((1, M_QK_W + M_V_W), F32)], 1)
        (qkv,) = _matmul(xb, w_in_t, w_nk=True, bm=bm, bn=bn_big, bk=D, epilogue=_ep_colscale, out_dtypes=[BF16],
                         extras=[(colscale, (1, bn_big), lambda i, j, k: (0, j))],
                         w_col0=off["mqkv"], n_cols=n_qkv, name="in_proj_mlstm_qkv")
        (br_gates,) = _matmul(xb, w_in_t, w_nk=True, bm=bm, bn=bn_big, bk=D, epilogue=_ep_sigmoid,
                              out_dtypes=[BF16], w_col0=o_mo, n_cols=M_V_W + N_BRANCH * D,
                              name="in_proj_sigmoid_gates")
        g_cols = (M_V_W, M_V_W + D)
        C0 = state[0].astype(F32)
        n0 = state[1].astype(F32).reshape(B, M_HEADS, 1, M_DQK)
        m0 = state[2].astype(F32).reshape(B, M_HEADS, 1, 1)
        ya, C, n_st, m = _mlstm(qkv.reshape(B, S, -1), br_gates.reshape(B, S, -1), gate_rows(L), b_ig, b_fg,
                                w_mnorm, C0, n0, m0, L=L)

    kv3 = kvg.reshape(B, S, n_first)
    k_new = kv3[:, :, :A_KV_W].reshape(B, S, A_KV_HEADS, A_HD)
    v_new = kv3[:, :, A_KV_W:wkv].reshape(B, S, A_KV_HEADS, A_HD)
    q_scale = A_HD ** -0.5
    wq = 512
    if cache is None:
        cos_q, sin_q = _rope_tables(pos, wq // A_HD, 0, q_scale)
        if bm <= S:
            tab = lambda t: (t.T, (wq, bm), lambda i, j: (0, i % (S // bm)))
        else:
            tab = lambda t: (jnp.tile(t.T, (1, bm // S)), (wq, bm), lambda i, j: (0, 0))
        (qt,) = _matmul_nt(w_in_t, xb, bm=bm, bn=wq, epilogue=_ep_rope_t, out_dtype=BF16,
                           extras=[tab(cos_q), tab(sin_q)], w_row0=off["aq"], n_rows=A_Q_W,
                           name="in_proj_attn_q_t")
        (vt,) = _matmul_nt(w_in_t, xb, bm=bm, bn=A_KV_W, epilogue=_ep_identity, out_dtype=BF16,
                           w_row0=off["kv"] + A_KV_W, n_rows=A_KV_W, name="in_proj_attn_v_t")
        yb = _swa_prompt(qt, kvg, vt, sink, B=B, S=S)
    else:
        cos_q, sin_q = _rope_tables(pos, A_HEADS, 0, q_scale)
        (aq,) = _matmul(xb, w_in_t, w_nk=True, bm=bm, bn=wq, bk=D, epilogue=_ep_rope, out_dtypes=[BF16],
                        extras=[pos_extra(cos_q, bm, wq), pos_extra(sin_q, bm, wq)],
                        w_col0=off["aq"], n_cols=A_Q_W, name="in_proj_attn_q")
        ck = cache[0].astype(F32).reshape(B, WINDOW, A_KV_W)
        cv = cache[1].astype(F32).reshape(B, WINDOW, A_KV_W)
        kv_all = jnp.concatenate([jnp.concatenate([ck, cv], -1), kv3[:, :, :wkv]], axis=1)
        yb = _swa_sample(aq.reshape(B, S, A_Q_W), kv_all, sink).reshape(M, A_Q_W)

    for name in DENSE_WEIGHTS[:3]:
        Wb.setdefault(name, W[name].astype(BF16))
    merged = _merge(ya.reshape(M, M_V_W), yb, Wb["br_a"], Wb["br_b"], br_gates,
                    g_cols[0], g_cols[1], bm=bm, bn=bn_big)
    res1 = (xf, (bm, bn_big), lambda i, j, k: (i, j))
    (h_pre,) = _matmul(merged, Wb["out"], bm=bm, bn=bn_big, bk=D, epilogue=functools.partial(_ep_residual, alpha),
                       out_dtypes=[F32], extras=[res1], name="out_proj_residual")
    bm_ln = min(256, M)
    h32, h16 = _layer_norm(h_pre, ln1_g, ln1_b, bm=bm_ln, out_dtypes=[F32, BF16])
    Wb.setdefault("up", W["up"].astype(BF16))
    act, *c = _matmul(h16, Wb["up"], bm=bm, bn=bn_big, bk=D, epilogue=_ep_relu_sq, out_dtypes=[BF16],
                      side_casts=side("down"), name="mlp_up_relu_sq")
    keep("down", c)
    res2 = (h32, (bm, bn_big), lambda i, j, k: (i, j))
    (y_pre,) = _matmul(act, Wb["down"], bm=bm, bn=bn_big, bk=min(2048, act.shape[1]),
                       epilogue=functools.partial(_ep_residual, alpha), out_dtypes=[F32], extras=[res2],
                       name="mlp_down_residual")
    (y,) = _layer_norm(y_pre, ln2_g, ln2_b, bm=bm_ln, out_dtypes=[F32])
    return (y.reshape(B, S, D), k_new, v_new, C, n_st.reshape(B, M_HEADS, M_DQK), m.reshape(B, M_HEADS), Wb)


def kernel(x_prompt, x_sample, cache_swa_k, cache_swa_v, state_mlstm_C, state_mlstm_n, state_mlstm_m,
           w_in, b_igate, b_fgate, w_mnorm, attn_sink, w_branch_a, w_branch_b, w_out,
           ln1_g, ln1_b, w_up, w_down, ln2_g, ln2_b):
    depth = w_in.shape[0]
    alpha = (2.0 * depth) ** 0.25
    S, T = x_prompt.shape[1], x_sample.shape[1]
    pos_p = jnp.arange(S, dtype=jnp.int32)
    pos_s = PAST_LEN + jnp.arange(T, dtype=jnp.int32)
    xp, xs = x_prompt, x_sample
    outs_p, outs_s = [], []
    for l in range(depth):
        W = dict(w_in_t=_regroup_w_in_t(w_in[l].T), br_a=w_branch_a[l], br_b=w_branch_b[l], out=w_out[l],
                 up=w_up[l], down=w_down[l])
        shared = (b_igate[l], b_fgate[l], w_mnorm[l], attn_sink[l], ln1_g[l], ln1_b[l], ln2_g[l], ln2_b[l], alpha)
        xp, *rest_p, Wb = _layer(xp, pos_p, W, *shared, None, None, bm=min(1024, xp.shape[0] * S))
        outs_p.append(rest_p)
        xs, *rest_s, _ = _layer(xs, pos_s, dict(W, **Wb), *shared,
                                (state_mlstm_C[l], state_mlstm_n[l], state_mlstm_m[l]),
                                (cache_swa_k[l], cache_swa_v[l]), bm=xs.shape[0] * T)
        outs_s.append(rest_s)
    stack = lambda outs, i: jnp.stack([o[i] for o in outs])
    pk = jnp.stack([o[0][:, -WINDOW:] for o in outs_p])
    pv = jnp.stack([o[1][:, -WINDOW:] for o in outs_p])
    return (xp, xs, pk, pv, stack(outs_p, 2), stack(outs_p, 3), stack(outs_p, 4),
            stack(outs_s, 0), stack(outs_s, 1), stack(outs_s, 2), stack(outs_s, 3), stack(outs_s, 4))
```

```python
import functools

import jax
import jax.numpy as jnp
from jax import lax
from jax.experimental import pallas as pl
from jax.experimental.pallas import tpu as pltpu

CHUNK = 64
M_HEADS = 8
M_DQK = 128
M_DV = 256
A_HEADS = 32
A_KV_HEADS = 4
A_GROUP = A_HEADS // A_KV_HEADS
A_HD = 64
WINDOW = 128
WIN_CHUNKS = WINDOW // CHUNK
ROPE_THETA = 10000.0
PAST_LEN = 1024
N_BRANCH = 2
LN_EPS = 1e-5
RMS_EPS = 1e-6
LOG2_E = 1.4426950408889634

M_QK_W = M_HEADS * M_DQK
M_V_W = M_HEADS * M_DV
A_Q_W = A_HEADS * A_HD
A_KV_W = A_KV_HEADS * A_HD

V7X_VMEM_BYTES = 64 * 1024 * 1024
V7X_LANES = 128
V7X_SUBLANES = 8
VMEM_REQUEST_CAP = (V7X_VMEM_BYTES * 7) // 8

F32 = jnp.float32
BF16 = jnp.bfloat16
NT_DIMS = (((1,), (1,)), ((), ()))
TN_DIMS = (((0,), (0,)), ((), ()))


def _nbytes(shape, dtype):
    n = 1
    for s in shape:
        n *= s
    return n * jnp.dtype(dtype).itemsize


def _vmem_limit(pipelined, resident=0):
    return int(min(VMEM_REQUEST_CAP, 2 * sum(pipelined) + resident + (4 << 20)))


def _side_cast_specs(side_casts, n_steps, step_of):
    specs, shapes, nbytes = [], [], []
    for a in side_casts:
        rows, cols = a.shape
        assert rows % n_steps == 0 and (rows // n_steps) % (2 * V7X_SUBLANES) == 0, (a.shape, n_steps)
        slab = rows // n_steps
        specs.append(pl.BlockSpec((slab, cols), lambda *ids: (step_of(*ids), 0)))
        shapes.append(jax.ShapeDtypeStruct(a.shape, BF16))
        nbytes.append(_nbytes((slab, cols), F32) + _nbytes((slab, cols), BF16))
    return specs, shapes, nbytes


def _matmul(x, w, *, bm, bn, bk, epilogue, out_dtypes, extras=(), w_nk=False, w_col0=0, n_cols=None,
            emit_x_bf16=False, side_casts=(), name):
    M, K = x.shape
    N = (w.shape[0] if w_nk else w.shape[1]) if n_cols is None else n_cols
    assert M % bm == 0 and N % bn == 0 and K % bk == 0 and w_col0 % bn == 0, (M, N, K, bm, bn, bk, w_col0)
    nm, nn, nk = M // bm, N // bn, K // bk
    assert not emit_x_bf16 or nk == 1
    j0 = w_col0 // bn
    n_ex, n_out, n_sc = len(extras), len(out_dtypes), len(side_casts)
    n_xo = 1 if emit_x_bf16 else 0

    def body(*refs):
        x_ref, w_ref = refs[0], refs[1]
        ex_refs = refs[2:2 + n_ex]
        sc_in = refs[2 + n_ex:2 + n_ex + n_sc]
        o0 = 2 + n_ex + n_sc
        out_refs = refs[o0:o0 + n_out]
        sc_out = refs[o0 + n_out + n_xo:o0 + n_out + n_xo + n_sc]
        for src, dst in zip(sc_in, sc_out):
            dst[...] = src[...].astype(dst.dtype)

        def product():
            xv = x_ref[...]
            if emit_x_bf16:
                xv = xv.astype(BF16)
                refs[o0 + n_out][...] = xv
            if w_nk:
                return lax.dot_general(xv, w_ref[...], NT_DIMS, preferred_element_type=F32)
            return jnp.dot(xv, w_ref[...], preferred_element_type=F32)

        def finish(acc):
            outs = epilogue(acc, *[r[...] for r in ex_refs])
            for o_ref, o in zip(out_refs, outs):
                o_ref[...] = o.astype(o_ref.dtype)

        if nk == 1:
            finish(product())
        else:
            acc_ref = refs[-1]
            k = pl.program_id(2)

            @pl.when(k == 0)
            def _():
                acc_ref[...] = product()

            @pl.when(jnp.logical_and(k > 0, k < nk - 1))
            def _():
                acc_ref[...] = acc_ref[...] + product()

            @pl.when(k == nk - 1)
            def _():
                finish(acc_ref[...] + product())

    w_spec = (pl.BlockSpec((bn, bk), lambda i, j, k: (j0 + j, k)) if w_nk
              else pl.BlockSpec((bk, bn), lambda i, j, k: (k, j0 + j)))
    sc_specs, sc_shapes, sc_bytes = _side_cast_specs(side_casts, nm * nn * nk, lambda i, j, k: (i * nn + j) * nk + k)
    in_specs = [pl.BlockSpec((bm, bk), lambda i, j, k: (i, k)), w_spec]
    in_specs += [pl.BlockSpec(bs, im) for (_, bs, im) in extras] + sc_specs
    out_specs = [pl.BlockSpec((bm, bn), lambda i, j, k: (i, j)) for _ in out_dtypes]
    out_shape = [jax.ShapeDtypeStruct((M, N), dt) for dt in out_dtypes]
    if emit_x_bf16:
        out_specs.append(pl.BlockSpec((bm, bk), lambda i, j, k: (i, k)))
        out_shape.append(jax.ShapeDtypeStruct((M, K), BF16))
    out_specs += sc_specs
    out_shape += sc_shapes
    scratch = [pltpu.VMEM((bm, bn), F32)] if nk > 1 else []
    pipelined = [_nbytes((bm, bk), x.dtype), _nbytes((bk, bn), w.dtype)]
    pipelined += [_nbytes(bs, a.dtype) for (a, bs, _) in extras]
    pipelined += [_nbytes((bm, bn), dt) for dt in out_dtypes] + sc_bytes
    resident = _nbytes((bm, bn), F32) * (4 if nk > 1 else 3)
    if emit_x_bf16:
        pipelined.append(_nbytes((bm, bk), BF16))
        resident += _nbytes((bm, bk), BF16)
    semantics = ("arbitrary",) * 3 if (n_sc or emit_x_bf16) else ("parallel", "parallel", "arbitrary")
    outs = pl.pallas_call(
        body,
        grid=(nm, nn, nk),
        in_specs=in_specs,
        out_specs=out_specs,
        out_shape=out_shape,
        scratch_shapes=scratch,
        compiler_params=pltpu.CompilerParams(
            dimension_semantics=semantics,
            vmem_limit_bytes=_vmem_limit(pipelined, resident)),
        name=name,
    )(x, w, *[a for (a, _, _) in extras], *side_casts)
    return outs


TOKEN_BLOCK = V7X_LANES


def _matmul_nt(wt, x, *, bm, bn, epilogue, out_dtype, extras=(), w_row0=0, n_rows=None, side_casts=(), name):
    K = wt.shape[1]
    N = wt.shape[0] if n_rows is None else n_rows
    M = x.shape[0]
    tb = TOKEN_BLOCK
    assert M % bm == 0 and N % bn == 0 and w_row0 % bn == 0 and bm % tb == 0
    j0 = w_row0 // bn
    nm, nn = M // bm, N // bn
    n_ex, n_sc = len(extras), len(side_casts)

    def body(w_ref, x_ref, *refs):
        ex_refs = refs[:n_ex]
        sc_in = refs[n_ex:n_ex + n_sc]
        o_ref = refs[n_ex + n_sc]
        sc_out = refs[n_ex + n_sc + 1:]
        for src, dst in zip(sc_in, sc_out):
            dst[...] = src[...].astype(dst.dtype)
        acc = lax.dot_general(w_ref[...], x_ref[...], NT_DIMS, preferred_element_type=F32)
        (out,) = epilogue(acc, *[r[...] for r in ex_refs])
        for t in range(bm // tb):
            o_ref[t] = out[:, t * tb:(t + 1) * tb].astype(o_ref.dtype)

    sc_specs, sc_shapes, sc_bytes = _side_cast_specs(side_casts, nm * nn, lambda i, j: i * nn + j)
    pipelined = [_nbytes((bn, K), wt.dtype), _nbytes((bm, K), x.dtype), _nbytes((bn, bm), out_dtype)]
    pipelined += [_nbytes(bs, a.dtype) for (a, bs, _) in extras] + sc_bytes
    outs = pl.pallas_call(
        body,
        grid=(nm, nn),
        in_specs=[pl.BlockSpec((bn, K), lambda i, j: (j0 + j, 0)),
                  pl.BlockSpec((bm, K), lambda i, j: (i, 0))]
        + [pl.BlockSpec(bs, im) for (_, bs, im) in extras] + sc_specs,
        out_specs=[pl.BlockSpec((bm // tb, bn, tb), lambda i, j: (i, j, 0))] + sc_specs,
        out_shape=[jax.ShapeDtypeStruct((M // tb, N, tb), out_dtype)] + sc_shapes,
        compiler_params=pltpu.CompilerParams(
            dimension_semantics=("arbitrary", "arbitrary") if n_sc else ("parallel", "parallel"),
            vmem_limit_bytes=_vmem_limit(pipelined, 3 * _nbytes((bn, bm), F32))),
        name=name,
    )(wt, x, *[a for (a, _, _) in extras], *side_casts)
    return outs


def _ep_scale(scale, acc):
    return (acc * scale,)


def _ep_colscale(acc, scale_row):
    return (acc * scale_row,)


def _ep_sigmoid(acc):
    return (0.5 * jnp.tanh(0.5 * acc) + 0.5,)


def _ep_identity(acc):
    return (acc,)


def _ep_relu_sq(acc):
    r = jnp.maximum(acc, 0.0)
    return (r * r,)


def _ep_rope(acc, cos, sin_signed):
    width = acc.shape[1]
    half = A_HD // 2
    lane = lax.broadcasted_iota(jnp.int32, acc.shape, 1)
    first_half = (lane % A_HD) < half
    partner = jnp.where(first_half, pltpu.roll(acc, width - half, 1), pltpu.roll(acc, half, 1))
    return (acc * cos + partner * sin_signed,)


def _ep_rope_lead(acc, cos, sin_signed):
    w = cos.shape[1]
    (lead,) = _ep_rope(acc[:, :w], cos, sin_signed)
    return (jnp.concatenate([lead, acc[:, w:]], axis=1),)


def _ep_rope_t(acc, cos, sin_signed):
    rows = acc.shape[0]
    half = A_HD // 2
    row = lax.broadcasted_iota(jnp.int32, acc.shape, 0)
    first_half = (row % A_HD) < half
    partner = jnp.where(first_half, pltpu.roll(acc, rows - half, 0), pltpu.roll(acc, half, 0))
    return (acc * cos + partner * sin_signed,)


def _ep_residual(alpha, acc, res):
    return (alpha * res + acc,)


def _merge(ya, yb, wa, wb, gates, g0_col, g1_col, *, bm, bn):
    M, Ka = ya.shape
    Kb = yb.shape[1]
    N = wa.shape[1]
    assert M % bm == 0 and N % bn == 0 and g0_col % bn == 0 and g1_col % bn == 0
    o0, o1 = g0_col // bn, g1_col // bn

    def body(ya_ref, yb_ref, wa_ref, wb_ref, g0_ref, g1_ref, o_ref):
        a = jnp.dot(ya_ref[...], wa_ref[...], preferred_element_type=F32)
        b = jnp.dot(yb_ref[...], wb_ref[...], preferred_element_type=F32)
        o_ref[...] = (g0_ref[...].astype(F32) * a + g1_ref[...].astype(F32) * b).astype(o_ref.dtype)

    pipelined = [_nbytes((bm, Ka), BF16), _nbytes((bm, Kb), BF16), _nbytes((Ka, bn), BF16),
                 _nbytes((Kb, bn), BF16), 3 * _nbytes((bm, bn), BF16)]
    return pl.pallas_call(
        body,
        grid=(M // bm, N // bn),
        in_specs=[pl.BlockSpec((bm, Ka), lambda i, j: (i, 0)),
                  pl.BlockSpec((bm, Kb), lambda i, j: (i, 0)),
                  pl.BlockSpec((Ka, bn), lambda i, j: (0, j)),
                  pl.BlockSpec((Kb, bn), lambda i, j: (0, j)),
                  pl.BlockSpec((bm, bn), lambda i, j: (i, o0 + j)),
                  pl.BlockSpec((bm, bn), lambda i, j: (i, o1 + j))],
        out_specs=pl.BlockSpec((bm, bn), lambda i, j: (i, j)),
        out_shape=jax.ShapeDtypeStruct((M, N), BF16),
        compiler_params=pltpu.CompilerParams(
            dimension_semantics=("parallel", "parallel"),
            vmem_limit_bytes=_vmem_limit(pipelined, 2 * _nbytes((bm, bn), F32))),
        name="branch_merge",
    )(ya, yb, wa, wb, gates, gates)


def _layer_norm(x, g, b, *, bm, out_dtypes):
    M, D = x.shape
    assert M % bm == 0

    def body(x_ref, g_ref, b_ref, *o_refs):
        xf = x_ref[...]
        mu = jnp.mean(xf, axis=-1, keepdims=True)
        xc = xf - mu
        var = jnp.mean(xc * xc, axis=-1, keepdims=True)
        y = xc * lax.rsqrt(var + LN_EPS) * g_ref[...] + b_ref[...]
        for o_ref in o_refs:
            o_ref[...] = y.astype(o_ref.dtype)

    pipelined = [_nbytes((bm, D), F32)] + [_nbytes((bm, D), dt) for dt in out_dtypes]
    return pl.pallas_call(
        body,
        grid=(M // bm,),
        in_specs=[pl.BlockSpec((bm, D), lambda i: (i, 0)),
                  pl.BlockSpec((1, D), lambda i: (0, 0)),
                  pl.BlockSpec((1, D), lambda i: (0, 0))],
        out_specs=[pl.BlockSpec((bm, D), lambda i: (i, 0)) for _ in out_dtypes],
        out_shape=[jax.ShapeDtypeStruct((M, D), dt) for dt in out_dtypes],
        compiler_params=pltpu.CompilerParams(
            dimension_semantics=("parallel",),
            vmem_limit_bytes=_vmem_limit(pipelined, 2 * _nbytes((bm, D), F32))),
        name="layer_norm",
    )(x, g.reshape(1, D), b.reshape(1, D))


MLSTM_HEADS_PER_STEP = 4


def _mlstm(qkv, og, gates_t, b_ig, b_fg, w_mnorm, C0, n0, m0, *, L):
    B, S, _ = qkv.shape
    nC = S // L
    nCp = gates_t.shape[2]
    H, dk, dv = M_HEADS, M_DQK, M_DV
    hb = MLSTM_HEADS_PER_STEP
    ng = H // hb
    k_blk0 = M_QK_W // (hb * dk)
    v_blk0 = (2 * M_QK_W) // (hb * dv)

    def body(big_ref, bfg_ref, q_ref, k_ref, v_ref, og_ref, ig_ref, fg_ref, wm_ref, C0_ref, n0_ref, m0_ref,
             ya_ref, C_ref, n_ref, m_ref, b_s, ig_s):
        g = pl.program_id(1)
        r_i = lax.broadcasted_iota(jnp.int32, (L, L), 0)
        c_i = lax.broadcasted_iota(jnp.int32, (L, L), 1)
        tri_incl = (r_i <= c_i).astype(F32)
        for j in range(hb):
            ig_s[j] = ig_ref[0, j] + big_ref[g * hb + j]
            lf_all = jax.nn.log_sigmoid(fg_ref[0, j] + bfg_ref[g * hb + j])
            b_s[j] = jnp.dot(lf_all, tri_incl, precision=lax.Precision.HIGHEST, preferred_element_type=F32)
        C_ref[...] = C0_ref[...]
        n_ref[...] = n0_ref[...]
        m_ref[...] = m0_ref[...]
        eye = r_i == c_i
        causal = c_i <= r_i

        def col_of(row):
            return jnp.sum(jnp.where(eye, jnp.broadcast_to(row, (L, L)), 0.0), axis=1, keepdims=True)

        def head_chunk(j, c, rows):
            q = q_ref[0, rows, j * dk:(j + 1) * dk]
            k = k_ref[0, rows, j * dk:(j + 1) * dk]
            v = v_ref[0, rows, j * dv:(j + 1) * dv]
            b_row = b_s[j, pl.ds(c, 1), :]
            ig_row = ig_s[j, pl.ds(c, 1), :]
            m_prev = m_ref[0, j]
            C_prev = C_ref[0, j]
            n_prev = n_ref[0, j]

            b_col = col_of(b_row)
            dlog = b_col - jnp.broadcast_to(b_row, (L, L)) + jnp.broadcast_to(ig_row, (L, L))
            dlog = jnp.where(causal, dlog, -jnp.inf)
            inter = b_col + m_prev
            m_t = jnp.maximum(inter, jnp.max(dlog, axis=1, keepdims=True))
            qk = lax.dot_general(q, k, NT_DIMS, preferred_element_type=F32)
            s = qk * jnp.exp(dlog - m_t)
            a = jnp.exp(inter - m_t)
            qC = lax.dot_general(q, C_prev.astype(BF16), NT_DIMS, preferred_element_type=F32)
            sv = jnp.dot(s.astype(BF16), v, preferred_element_type=F32)
            num = a * qC + sv
            qn = jnp.sum(q.astype(F32) * n_prev, axis=1, keepdims=True)
            den = a * qn + jnp.sum(s, axis=1, keepdims=True)
            hid = num / jnp.maximum(jnp.abs(den), jnp.exp(-m_t))
            hn = hid * lax.rsqrt(jnp.mean(hid * hid, axis=1, keepdims=True) + RMS_EPS)
            hn = hn * wm_ref[:, j * dv:(j + 1) * dv]
            gate = og_ref[0, rows, j * dv:(j + 1) * dv].astype(F32)
            ya_ref[0, rows, j * dv:(j + 1) * dv] = (gate * hn).astype(ya_ref.dtype)

            bL = b_row[:, L - 1:L]
            g_row = bL - b_row + ig_row
            m_new = jnp.maximum(bL + m_prev, jnp.max(g_row, axis=1, keepdims=True))
            wk_col = col_of(jnp.exp(g_row - m_new))
            decay = jnp.exp(bL + m_prev - m_new)
            kw = k.astype(F32) * wk_col
            vk = lax.dot_general(v, kw.astype(BF16), TN_DIMS, preferred_element_type=F32)
            C_ref[0, j] = decay * C_prev + vk
            n_ref[0, j] = decay * n_prev + jnp.sum(kw, axis=0, keepdims=True)
            m_ref[0, j] = m_new

        def chunk(c, carry):
            rows = pl.ds(pl.multiple_of(c * L, L), L)
            for j in range(hb):
                head_chunk(j, c, rows)
            return carry

        lax.fori_loop(0, nC, chunk, 0)

    smem = pl.BlockSpec(memory_space=pltpu.SMEM)
    pipelined = [2 * _nbytes((S, hb * dk), BF16), 3 * _nbytes((S, hb * dv), BF16),
                 2 * _nbytes((hb, nCp, L), F32), 2 * _nbytes((hb, dv, dk), F32)]
    ya, C, n, m = pl.pallas_call(
        body,
        grid=(B, ng),
        in_specs=[smem, smem,
                  pl.BlockSpec((1, S, hb * dk), lambda b, g: (b, 0, g)),
                  pl.BlockSpec((1, S, hb * dk), lambda b, g: (b, 0, k_blk0 + g)),
                  pl.BlockSpec((1, S, hb * dv), lambda b, g: (b, 0, v_blk0 + g)),
                  pl.BlockSpec((1, S, hb * dv), lambda b, g: (b, 0, g)),
                  pl.BlockSpec((1, hb, nCp, L), lambda b, g: (b, g, 0, 0)),
                  pl.BlockSpec((1, hb, nCp, L), lambda b, g: (b, ng + g, 0, 0)),
                  pl.BlockSpec((1, hb * dv), lambda b, g: (0, g)),
                  pl.BlockSpec((1, hb, dv, dk), lambda b, g: (b, g, 0, 0)),
                  pl.BlockSpec((1, hb, 1, dk), lambda b, g: (b, g, 0, 0)),
                  pl.BlockSpec((1, hb, 1, 1), lambda b, g: (b, g, 0, 0))],
        out_specs=[pl.BlockSpec((1, S, hb * dv), lambda b, g: (b, 0, g)),
                   pl.BlockSpec((1, hb, dv, dk), lambda b, g: (b, g, 0, 0)),
                   pl.BlockSpec((1, hb, 1, dk), lambda b, g: (b, g, 0, 0)),
                   pl.BlockSpec((1, hb, 1, 1), lambda b, g: (b, g, 0, 0))],
        out_shape=[jax.ShapeDtypeStruct((B, S, M_V_W), BF16),
                   jax.ShapeDtypeStruct((B, H, dv, dk), F32),
                   jax.ShapeDtypeStruct((B, H, 1, dk), F32),
                   jax.ShapeDtypeStruct((B, H, 1, 1), F32)],
        scratch_shapes=[pltpu.VMEM((hb, nCp, L), F32), pltpu.VMEM((hb, nCp, L), F32)],
        compiler_params=pltpu.CompilerParams(
            dimension_semantics=("parallel", "parallel"),
            vmem_limit_bytes=_vmem_limit(pipelined, 4 << 20)),
        name="mlstm_chunks",
    )(b_ig, b_fg, qkv, qkv, qkv, og, gates_t, gates_t, w_mnorm.reshape(1, M_V_W), C0, n0, m0)
    return ya, C, n, m


def _mlstm_prompt(qt, k, vt, ogt, gates_p, b_ig, b_fg, wm_b, *, B, S):
    TB = TOKEN_BLOCK
    assert TB == 2 * CHUNK and S % TB == 0
    nP = S // TB
    nPp = gates_p.shape[2]
    H, dk, dv = M_HEADS, M_DQK, M_DV
    hb = MLSTM_HEADS_PER_STEP
    ng = H // hb
    dva = dv + V7X_SUBLANES
    NEG = -jnp.inf

    def body(big_ref, bfg_ref, q_ref, k_ref, v_ref, og_ref, ig_ref, fg_ref, wm_ref,
             ya_ref, C_ref, n_ref, m_ref, caug_s, row_s, u_s):
        g = pl.program_id(1)
        s_i = lax.broadcasted_iota(jnp.int32, (TB, TB), 0)
        t_i = lax.broadcasted_iota(jnp.int32, (TB, TB), 1)
        causal = s_i <= t_i
        same_chunk = (s_i // CHUNK) == (t_i // CHUNK)
        intra = jnp.logical_and(causal, same_chunk)
        eye = s_i == t_i
        tri_chunk = intra.astype(F32)
        lane_p = lax.broadcasted_iota(jnp.int32, (nPp, TB), 1)
        first_p = lane_p < CHUNK
        first_1 = lax.broadcasted_iota(jnp.int32, (1, TB), 1) < CHUNK
        for j in range(hb):
            ig = ig_ref[0, j] + big_ref[g * hb + j]
            lf = jax.nn.log_sigmoid(fg_ref[0, j] + bfg_ref[g * hb + j])
            bc = jnp.dot(lf, tri_chunk, precision=lax.Precision.HIGHEST, preferred_element_type=F32)
            b_a = jnp.broadcast_to(bc[:, CHUNK - 1:CHUNK], (nPp, TB))
            b_b = jnp.broadcast_to(bc[:, TB - 1:TB], (nPp, TB))
            bp = bc + jnp.where(first_p, 0.0, b_a)
            gl = jnp.where(first_p, b_a, b_b) - bc + ig
            gmax_a = jnp.max(jnp.where(first_p, gl, NEG), axis=1, keepdims=True)
            gmax_b = jnp.max(jnp.where(first_p, NEG, gl), axis=1, keepdims=True)
            row_s[j, 0] = bc
            row_s[j, 1] = bp
            row_s[j, 2] = ig - bp
            row_s[j, 6] = b_a + b_b
            row_s[j, 3] = b_a
            row_s[j, 4] = b_b
            row_s[j, 5] = jnp.broadcast_to(gmax_a, (nPp, TB))
            row_s[j, 7] = jnp.broadcast_to(gmax_b, (nPp, TB))
            caug_s[j] = jnp.zeros((dva, dk), F32)

        def scan(p, m_rows):
            nxt = []
            for j in range(hb):
                m0 = m_rows[j]
                r = pl.ds(p, 1)
                m1 = jnp.maximum(row_s[j, 3, r, :] + m0, row_s[j, 5, r, :])
                m2 = jnp.maximum(row_s[j, 4, r, :] + m1, row_s[j, 7, r, :])
                row_s[j, 3, r, :] = jnp.where(first_1, m0, m1)
                row_s[j, 4, r, :] = m0
                row_s[j, 5, r, :] = m2
                u_row = row_s[j, 2, r, :]
                u_col = jnp.sum(jnp.where(eye, jnp.broadcast_to(u_row, (TB, TB)), 0.0), axis=1, keepdims=True)
                u_s[j, p] = jnp.broadcast_to(u_col, (TB, TB))
                nxt.append(m2)
            return tuple(nxt)

        m_fin = lax.fori_loop(0, nP, scan, tuple(jnp.zeros((1, TB), F32) for _ in range(hb)))

        def head_block(j, p, rows):
            kp = k_ref[0, rows, j * dk:(j + 1) * dk]
            q_t = q_ref[p, j * dk:(j + 1) * dk, :]
            v_t = v_ref[p, j * dv:(j + 1) * dv, :]
            r = pl.ds(p, 1)
            bc, bp, u_row = row_s[j, 0, r, :], row_s[j, 1, r, :], row_s[j, 2, r, :]
            m_chunk, m_start, m_end, b_tot = row_s[j, 3, r, :], row_s[j, 4, r, :], row_s[j, 5, r, :], row_s[j, 6, r, :]
            d_t = bp + u_s[j, p]
            m_t = jnp.maximum(bc + m_chunk, jnp.max(jnp.where(intra, d_t, NEG), axis=0, keepdims=True))
            e_t = jnp.exp(jnp.where(causal, d_t, NEG) - m_t)
            p_t = jnp.dot(kp, q_t, preferred_element_type=F32) * e_t
            a = jnp.exp(bp + m_start - m_t)
            caug = caug_s[j]
            cq = jnp.dot(caug.astype(BF16), q_t, preferred_element_type=F32)
            num = a * cq[:dv] + jnp.dot(v_t, p_t.astype(BF16), preferred_element_type=F32)
            den = a * cq[dv:dv + 1] + jnp.sum(p_t, axis=0, keepdims=True)
            hid = num / jnp.maximum(jnp.abs(den), jnp.exp(-m_t))
            hn = hid * lax.rsqrt(jnp.mean(hid * hid, axis=0, keepdims=True) + RMS_EPS)
            hn = hn * wm_ref[j * dv:(j + 1) * dv, :] * og_ref[p, j * dv:(j + 1) * dv, :].astype(F32)
            for c0 in range(0, dv, TB):
                ya_ref[0, rows, j * dv + c0:j * dv + c0 + TB] = hn[c0:c0 + TB].T.astype(ya_ref.dtype)

            wk = jnp.exp(b_tot + u_row - m_end)
            decay = jnp.exp(b_tot + m_start - m_end)
            vw = (v_t.astype(F32) * wk).astype(BF16)
            vw = jnp.concatenate([vw, jnp.broadcast_to(wk, (V7X_SUBLANES, TB)).astype(BF16)], axis=0)
            caug_s[j] = decay * caug + jnp.dot(vw, kp, preferred_element_type=F32)

        def block(p, carry):
            rows = pl.ds(pl.multiple_of(p * TB, TB), TB)
            for j in range(hb):
                head_block(j, p, rows)
            return carry

        lax.fori_loop(0, nP, block, 0)
        for j in range(hb):
            C_ref[0, j] = caug_s[j, :dv, :]
            n_ref[0, j] = caug_s[j, dv:dv + 1, :]
            m_ref[0, j] = m_fin[j][:, :1]

    smem = pl.BlockSpec(memory_space=pltpu.SMEM)
    pipelined = [_nbytes((nP, hb * dk, TB), BF16), _nbytes((S, hb * dk), BF16), 3 * _nbytes((S, hb * dv), BF16),
                 2 * _nbytes((hb, nPp, TB), F32), _nbytes((hb * dv, TB), F32), _nbytes((hb, dv, dk), F32)]
    scratch = [pltpu.VMEM((hb, dva, dk), F32), pltpu.VMEM((hb, 8, nPp, TB), F32), pltpu.VMEM((hb, nP, TB, TB), F32)]
    resident = _nbytes((hb, dva, dk), F32) + _nbytes((hb, 8, nPp, TB), F32) + _nbytes((hb, nP, TB, TB), F32)
    ya, C, n, m = pl.pallas_call(
        body,
        grid=(B, ng),
        in_specs=[smem, smem,
                  pl.BlockSpec((nP, hb * dk, TB), lambda b, g: (b, g, 0)),
                  pl.BlockSpec((1, S, hb * dk), lambda b, g: (b, 0, g)),
                  pl.BlockSpec((nP, hb * dv, TB), lambda b, g: (b, g, 0)),
                  pl.BlockSpec((nP, hb * dv, TB), lambda b, g: (b, g, 0)),
                  pl.BlockSpec((1, hb, nPp, TB), lambda b, g: (b, g, 0, 0)),
                  pl.BlockSpec((1, hb, nPp, TB), lambda b, g: (b, ng + g, 0, 0)),
                  pl.BlockSpec((hb * dv, TB), lambda b, g: (g, 0))],
        out_specs=[pl.BlockSpec((1, S, hb * dv), lambda b, g: (b, 0, g)),
                   pl.BlockSpec((1, hb, dv, dk), lambda b, g: (b, g, 0, 0)),
                   pl.BlockSpec((1, hb, 1, dk), lambda b, g: (b, g, 0, 0)),
                   pl.BlockSpec((1, hb, 1, 1), lambda b, g: (b, g, 0, 0))],
        out_shape=[jax.ShapeDtypeStruct((B, S, M_V_W), BF16),
                   jax.ShapeDtypeStruct((B, H, dv, dk), F32),
                   jax.ShapeDtypeStruct((B, H, 1, dk), F32),
                   jax.ShapeDtypeStruct((B, H, 1, 1), F32)],
        scratch_shapes=scratch,
        compiler_params=pltpu.CompilerParams(
            dimension_semantics=("parallel", "parallel"),
            vmem_limit_bytes=_vmem_limit(pipelined, resident + (4 << 20))),
        name="mlstm_prompt_blocks",
    )(b_ig, b_fg, qt, k, vt, ogt, gates_p, gates_p, wm_b)
    return ya, C, n, m


def _swa_prompt(qt, kv, vt, sink, *, B, S):
    TB = 2 * CHUNK
    assert TB == V7X_LANES and S % TB == 0 and WIN_CHUNKS == 2
    nblk = S // TB
    G, KVH = A_GROUP, A_KV_HEADS

    def body(sink_ref, q_ref, kp_ref, kc_ref, vp_ref, vc_ref, o_ref):
        p_id = pl.program_id(1)
        key_chunk = lax.broadcasted_iota(jnp.int32, (2 * TB, TB), 0) // CHUNK
        q_half = lax.broadcasted_iota(jnp.int32, (2 * TB, TB), 1) // CHUNK
        valid = jnp.logical_and(key_chunk >= q_half, key_chunk <= q_half + WIN_CHUNKS)
        valid = jnp.logical_and(valid, jnp.logical_or(key_chunk >= WIN_CHUNKS, p_id > 0))
        bias = jnp.where(valid, 0.0, -jnp.inf).astype(F32)
        bias = jnp.concatenate([bias] * G, axis=1)
        zeros = jnp.zeros((A_HD, TB), q_ref.dtype)
        for kvh in range(KVH):
            pr = kvh // 2
            lanes = slice(pr * V7X_LANES, (pr + 1) * V7X_LANES)
            kc = jnp.concatenate([kp_ref[:, lanes], kc_ref[:, lanes]], axis=0).astype(BF16)
            rows = slice(kvh * A_HD, (kvh + 1) * A_HD)
            vt_w = jnp.concatenate([vp_ref[0, rows, :], vc_ref[0, rows, :]], axis=1)
            pieces, sinks = [], []
            for g in range(G):
                head = kvh * G + g
                qg = q_ref[0, head * A_HD:(head + 1) * A_HD, :]
                pieces.append(jnp.concatenate([zeros, qg] if kvh % 2 else [qg, zeros], axis=0))
                sinks.append(jnp.full((1, TB), sink_ref[head] * LOG2_E, F32))
            qz = jnp.concatenate(pieces, axis=1)
            sk = jnp.concatenate(sinks, axis=1)
            s_t = jnp.dot(kc, qz, preferred_element_type=F32) + bias
            mx = jnp.maximum(jnp.max(s_t, axis=0, keepdims=True), sk)
            p_t = jnp.exp2(s_t - mx)
            den = jnp.sum(p_t, axis=0, keepdims=True) + jnp.exp2(sk - mx)
            o_t = jnp.dot(vt_w, p_t.astype(BF16), preferred_element_type=F32) / den
            for j in range(G // 2):
                two = jnp.concatenate([o_t[:, (2 * j) * TB:(2 * j + 1) * TB],
                                       o_t[:, (2 * j + 1) * TB:(2 * j + 2) * TB]], axis=0)
                col0 = (kvh * G + 2 * j) * A_HD
                o_ref[:, col0:col0 + V7X_LANES] = two.T.astype(o_ref.dtype)

    prev = lambda b, p: b * nblk + jnp.maximum(p - 1, 0)
    cur = lambda b, p: b * nblk + p
    pipelined = [2 * _nbytes((A_Q_W, TB), BF16), 2 * _nbytes((TB, A_KV_W), F32), 2 * _nbytes((A_KV_W, TB), BF16)]
    return pl.pallas_call(
        body,
        grid=(B, nblk),
        in_specs=[pl.BlockSpec(memory_space=pltpu.SMEM),
                  pl.BlockSpec((1, A_Q_W, TB), lambda b, p: (cur(b, p), 0, 0)),
                  pl.BlockSpec((TB, A_KV_W), lambda b, p: (prev(b, p), 0)),
                  pl.BlockSpec((TB, A_KV_W), lambda b, p: (cur(b, p), 0)),
                  pl.BlockSpec((1, A_KV_W, TB), lambda b, p: (prev(b, p), 0, 0)),
                  pl.BlockSpec((1, A_KV_W, TB), lambda b, p: (cur(b, p), 0, 0))],
        out_specs=pl.BlockSpec((TB, A_Q_W), lambda b, p: (cur(b, p), 0)),
        out_shape=jax.ShapeDtypeStruct((B * S, A_Q_W), BF16),
        compiler_params=pltpu.CompilerParams(
            dimension_semantics=("parallel", "arbitrary"),
            vmem_limit_bytes=_vmem_limit(pipelined, 16 << 20)),
        name="swa_prompt_attention",
    )(sink, qt, kv, kv, vt, vt)


def _swa_sample(q, kv, sink):
    B, T, _ = q.shape
    Skv = kv.shape[1]
    G, KVH, LN = A_GROUP, A_KV_HEADS, V7X_LANES
    assert 2 * A_HD == LN and T % 16 == 0

    def body(sink_ref, q_ref, kv_ref, o_ref):
        kvw = kv_ref[0]
        lo_q = lax.broadcasted_iota(jnp.int32, (T, LN), 1) < A_HD
        lo_k = lax.broadcasted_iota(jnp.int32, (Skv, LN), 1) < A_HD
        zero = jnp.zeros((T, LN), q_ref.dtype)
        for kvh in range(KVH):
            pair, odd = kvh // 2, kvh % 2
            kx = kvw[:, pair * LN:(pair + 1) * LN]
            vx = kvw[:, A_KV_W + pair * LN:A_KV_W + (pair + 1) * LN]
            kr = pltpu.roll(kx, A_HD, 1)
            vr = pltpu.roll(vx, A_HD, 1)
            k2 = (jnp.where(lo_k, kr, kx) if odd else jnp.where(lo_k, kx, kr)).astype(BF16)
            v2 = (jnp.where(lo_k, vr, vx) if odd else jnp.where(lo_k, vx, vr)).astype(BF16)
            pieces, sinks = [], []
            for g in range(G):
                head = kvh * G + g
                q2 = q_ref[0, :, (head // 2) * LN:(head // 2 + 1) * LN]
                pieces.append(jnp.where(lo_q, zero, q2) if head % 2 else jnp.where(lo_q, q2, zero))
                sinks.append(jnp.full((T, 1), sink_ref[head], F32))
            qs = jnp.concatenate(pieces, axis=0)
            sk = jnp.concatenate(sinks, axis=0)
            s = lax.dot_general(qs, k2, NT_DIMS, preferred_element_type=F32)
            mx = jnp.maximum(jnp.max(s, axis=1, keepdims=True), sk)
            p = jnp.exp(s - mx)
            den = jnp.sum(p, axis=1, keepdims=True) + jnp.exp(sk - mx)
            o = jnp.dot(p.astype(BF16), v2, preferred_element_type=F32) / den
            for j in range(G // 2):
                pair_o = jnp.where(lo_q, o[(2 * j) * T:(2 * j + 1) * T], o[(2 * j + 1) * T:(2 * j + 2) * T])
                col0 = (kvh * G + 2 * j) * A_HD
                o_ref[0, :, col0:col0 + LN] = pair_o.astype(o_ref.dtype)

    pipelined = [2 * _nbytes((T, A_Q_W), BF16), _nbytes((Skv, 2 * A_KV_W), F32)]
    return pl.pallas_call(
        body,
        grid=(B,),
        in_specs=[pl.BlockSpec(memory_space=pltpu.SMEM),
                  pl.BlockSpec((1, T, A_Q_W), lambda b: (b, 0, 0)),
                  pl.BlockSpec((1, Skv, 2 * A_KV_W), lambda b: (b, 0, 0))],
        out_specs=pl.BlockSpec((1, T, A_Q_W), lambda b: (b, 0, 0)),
        out_shape=jax.ShapeDtypeStruct((B, T, A_Q_W), BF16),
        compiler_params=pltpu.CompilerParams(
            dimension_semantics=("parallel",),
            vmem_limit_bytes=_vmem_limit(pipelined, 8 << 20)),
        name="swa_sample_attention",
    )(sink, q, kv)


def _rope_tables(pos, n_rot_heads, n_plain_cols, scale):
    half = A_HD // 2
    inv = ROPE_THETA ** (-jnp.arange(half, dtype=F32) / half)
    ang = pos.astype(F32)[:, None] * inv[None, :]
    cos, sin = jnp.cos(ang) * scale, jnp.sin(ang) * scale
    cos_h = jnp.concatenate([cos, cos], -1)
    sin_h = jnp.concatenate([-sin, sin], -1)
    n = pos.shape[0]
    cos_t = jnp.concatenate([jnp.tile(cos_h, (1, n_rot_heads)), jnp.ones((n, n_plain_cols), F32)], -1)
    sin_t = jnp.concatenate([jnp.tile(sin_h, (1, n_rot_heads)), jnp.zeros((n, n_plain_cols), F32)], -1)
    return cos_t, sin_t


W_IN_BLOCK = 512


def _in_proj_layout(d_model):
    widths = [("mqkv", 2 * M_QK_W + M_V_W), ("sig", M_V_W + N_BRANCH * d_model), ("aq", A_Q_W),
              ("kv", 2 * A_KV_W), ("gate", W_IN_BLOCK)]
    off, col = {}, 0
    for name, w in widths:
        off[name] = col
        col += w
    off["end"] = col
    return off


def _regroup_w_in_t(w_in_t):
    D = w_in_t.shape[1]
    off = _in_proj_layout(D)
    bw = W_IN_BLOCK
    n_gate = 2 * M_HEADS
    src_gate = 2 * M_QK_W + 2 * M_V_W
    src_aq = src_gate + n_gate
    src_gp = src_aq + A_Q_W + 2 * A_KV_W
    assert src_gate % bw == 0 and off["sig"] % bw == 0 and (N_BRANCH * D) % bw == 0 and n_gate <= bw
    assert src_aq % V7X_SUBLANES == 0
    j_gp = (off["sig"] + M_V_W) // bw
    j_aq = off["aq"] // bw
    j_gate = off["gate"] // bw

    sl = V7X_SUBLANES

    def src_row(j):
        tile = jnp.where(j < j_gp, j * (bw // sl),
                         jnp.where(j < j_aq, src_gp // sl + (j - j_gp) * (bw // sl),
                                   jnp.where(j < j_gate, src_aq // sl + (j - j_aq) * (bw // sl), src_gate // sl)))
        return tile * sl

    def body(a_ref, o_ref):
        j = pl.program_id(0)
        a = a_ref[...]
        row = lax.broadcasted_iota(jnp.int32, a.shape, 0)
        keep = jnp.logical_or(j < j_gate, row < n_gate)
        o_ref[...] = jnp.where(keep, a, 0.0).astype(o_ref.dtype)

    pipelined = [_nbytes((bw, D), F32), _nbytes((bw, D), BF16)]
    return pl.pallas_call(
        body,
        grid=(off["end"] // bw,),
        in_specs=[pl.BlockSpec((pl.Element(bw), pl.Element(D)), lambda j: (src_row(j), 0))],
        out_specs=pl.BlockSpec((bw, D), lambda j: (j, 0)),
        out_shape=jax.ShapeDtypeStruct((off["end"], D), BF16),
        compiler_params=pltpu.CompilerParams(
            dimension_semantics=("parallel",),
            vmem_limit_bytes=_vmem_limit(pipelined, 2 * _nbytes((bw, D), F32))),
        name="regroup_w_in",
    )(w_in_t)


DENSE_WEIGHTS = ("br_a", "br_b", "out", "up", "down")


def _layer(x, pos, W, b_ig, b_fg, w_mnorm, sink, ln1_g, ln1_b, ln2_g, ln2_b, alpha, state, cache, *, bm):
    B, S, D = x.shape
    M = B * S
    L = min(CHUNK, S)
    xf = x.reshape(M, D)
    bn_big = min(1024, D)
    off = _in_proj_layout(D)
    w_in_t = W["w_in_t"]
    Wb = {name: W[name] for name in DENSE_WEIGHTS if W[name].dtype == BF16}

    def pos_extra(table, rows, width):
        if rows <= S:
            per = S // rows
            return (table, (rows, width), lambda i, j, k: (i % per, j))
        return (jnp.tile(table, (rows // S, 1)), (rows, width), lambda i, j, k: (0, j))

    q_scale_m = M_DQK ** -0.5
    wkv = 2 * A_KV_W
    bm0 = min(256, M)
    n_first = wkv + W_IN_BLOCK
    cos_k, sin_k = _rope_tables(pos, A_KV_HEADS, 0, 1.0)
    kvg, xb = _matmul(xf, w_in_t, w_nk=True, bm=bm0, bn=n_first, bk=D, epilogue=_ep_rope_lead, out_dtypes=[F32],
                      extras=[pos_extra(cos_k, bm0, A_KV_W), pos_extra(sin_k, bm0, A_KV_W)],
                      w_col0=off["kv"], n_cols=n_first, emit_x_bf16=True, name="in_proj_attn_kv_if_gates")
    gates = kvg[:, wkv:wkv + 2 * M_HEADS]

    def gate_rows(lanes):
        n = S // lanes
        t = gates.reshape(B, n, lanes, 2 * M_HEADS).transpose(0, 3, 1, 2)
        return jnp.pad(t, ((0, 0), (0, 0), (0, max(n, V7X_SUBLANES) - n), (0, 0)))

    def side(name):
        return [] if name in Wb else [W[name]]

    def keep(name, casts):
        if casts:
            Wb[name] = casts[0]

    o_mo, o_gp = off["sig"], off["sig"] + M_V_W
    if state is None:
        mk, *c = _matmul(xb, w_in_t, w_nk=True, bm=bm, bn=bn_big, bk=D, epilogue=_ep_identity, out_dtypes=[BF16],
                         w_col0=M_QK_W, n_cols=M_QK_W, side_casts=side("br_a"), name="in_proj_mlstm_k")
        keep("br_a", c)
        bn_t = min(1024, M_QK_W)
        mq_t, *c = _matmul_nt(w_in_t, xb, bm=bm, bn=bn_t, epilogue=functools.partial(_ep_scale, q_scale_m),
                              out_dtype=BF16, w_row0=0, n_rows=M_QK_W, side_casts=side("br_b"),
                              name="in_proj_mlstm_q_t")
        keep("br_b", c)
        mv_t, *c = _matmul_nt(w_in_t, xb, bm=bm, bn=bn_t, epilogue=_ep_identity, out_dtype=BF16,
                              w_row0=2 * M_QK_W, n_rows=M_V_W, side_casts=side("out"),
                              name="in_proj_mlstm_v_t")
        keep("out", c)
        (og_t,) = _matmul_nt(w_in_t, xb, bm=bm, bn=bn_t, epilogue=_ep_sigmoid, out_dtype=BF16,
                             w_row0=o_mo, n_rows=M_V_W, name="in_proj_mlstm_o_t")
        br_gates, *c = _matmul(xb, w_in_t, w_nk=True, bm=bm, bn=bn_big, bk=D, epilogue=_ep_sigmoid,
                               out_dtypes=[BF16], w_col0=o_gp, n_cols=N_BRANCH * D, side_casts=side("up"),
                               name="in_proj_branch_gates")
        keep("up", c)
        g_cols = (0, D)
        wm_b = jnp.broadcast_to(w_mnorm[:, None], (M_V_W, TOKEN_BLOCK))
        ya, C, n_st, m = _mlstm_prompt(mq_t, mk.reshape(B, S, M_QK_W), mv_t, og_t, gate_rows(TOKEN_BLOCK),
                                       b_ig, b_fg, wm_b, B=B, S=S)
    else:
        n_qkv = 2 * M_QK_W + M_V_W
        colscale = jnp.concatenate([jnp.full((1, M_QK_W), q_scale_m, F32), jnp.ones((1, M_QK_W + M_V_W), F32)], 1)
        (qkv,) = _matmul(xb, w_in_t, w_nk=True, bm=bm, bn=bn_big, bk=D, epilogue=_ep_colscale, out_dtypes=[BF16],
                         extras=[(colscale, (1, bn_big), lambda i, j, k: (0, j))],
                         w_col0=off["mqkv"], n_cols=n_qkv, name="in_proj_mlstm_qkv")
        (br_gates,) = _matmul(xb, w_in_t, w_nk=True, bm=bm, bn=bn_big, bk=D, epilogue=_ep_sigmoid,
                              out_dtypes=[BF16], w_col0=o_mo, n_cols=M_V_W + N_BRANCH * D,
                              name="in_proj_sigmoid_gates")
        g_cols = (M_V_W, M_V_W + D)
        C0 = state[0].astype(F32)
        n0 = state[1].astype(F32).reshape(B, M_HEADS, 1, M_DQK)
        m0 = state[2].astype(F32).reshape(B, M_HEADS, 1, 1)
        ya, C, n_st, m = _mlstm(qkv.reshape(B, S, -1), br_gates.reshape(B, S, -1), gate_rows(L), b_ig, b_fg,
                                w_mnorm, C0, n0, m0, L=L)

    kv3 = kvg.reshape(B, S, n_first)
    k_new = kv3[:, :, :A_KV_W].reshape(B, S, A_KV_HEADS, A_HD)
    v_new = kv3[:, :, A_KV_W:wkv].reshape(B, S, A_KV_HEADS, A_HD)
    q_scale = A_HD ** -0.5
    wq = 512
    if cache is None:
        cos_q, sin_q = _rope_tables(pos, wq // A_HD, 0, q_scale * LOG2_E)
        if bm <= S:
            tab = lambda t: (t.T, (wq, bm), lambda i, j: (0, i % (S // bm)))
        else:
            tab = lambda t: (jnp.tile(t.T, (1, bm // S)), (wq, bm), lambda i, j: (0, 0))
        (qt,) = _matmul_nt(w_in_t, xb, bm=bm, bn=wq, epilogue=_ep_rope_t, out_dtype=BF16,
                           extras=[tab(cos_q), tab(sin_q)], w_row0=off["aq"], n_rows=A_Q_W,
                           name="in_proj_attn_q_t")
        (vt,) = _matmul_nt(w_in_t, xb, bm=bm, bn=A_KV_W, epilogue=_ep_identity, out_dtype=BF16,
                           w_row0=off["kv"] + A_KV_W, n_rows=A_KV_W, name="in_proj_attn_v_t")
        yb = _swa_prompt(qt, kvg, vt, sink, B=B, S=S)
    else:
        cos_q, sin_q = _rope_tables(pos, A_HEADS, 0, q_scale)
        (aq,) = _matmul(xb, w_in_t, w_nk=True, bm=bm, bn=wq, bk=D, epilogue=_ep_rope, out_dtypes=[BF16],
                        extras=[pos_extra(cos_q, bm, wq), pos_extra(sin_q, bm, wq)],
                        w_col0=off["aq"], n_cols=A_Q_W, name="in_proj_attn_q")
        ck = cache[0].astype(F32).reshape(B, WINDOW, A_KV_W)
        cv = cache[1].astype(F32).reshape(B, WINDOW, A_KV_W)
        kv_all = jnp.concatenate([jnp.concatenate([ck, cv], -1), kv3[:, :, :wkv]], axis=1)
        yb = _swa_sample(aq.reshape(B, S, A_Q_W), kv_all, sink).reshape(M, A_Q_W)

    for name in DENSE_WEIGHTS[:3]:
        Wb.setdefault(name, W[name].astype(BF16))
    merged = _merge(ya.reshape(M, M_V_W), yb, Wb["br_a"], Wb["br_b"], br_gates,
                    g_cols[0], g_cols[1], bm=bm, bn=bn_big)
    res1 = (xf, (bm, bn_big), lambda i, j, k: (i, j))
    (h_pre,) = _matmul(merged, Wb["out"], bm=bm, bn=bn_big, bk=D, epilogue=functools.partial(_ep_residual, alpha),
                       out_dtypes=[F32], extras=[res1], name="out_proj_residual")
    bm_ln = min(256, M)
    h32, h16 = _layer_norm(h_pre, ln1_g, ln1_b, bm=bm_ln, out_dtypes=[F32, BF16])
    Wb.setdefault("up", W["up"].astype(BF16))
    act, *c = _matmul(h16, Wb["up"], bm=bm, bn=bn_big, bk=D, epilogue=_ep_relu_sq, out_dtypes=[BF16],
                      side_casts=side("down"), name="mlp_up_relu_sq")
    keep("down", c)
    res2 = (h32, (bm, bn_big), lambda i, j, k: (i, j))
    (y_pre,) = _matmul(act, Wb["down"], bm=bm, bn=bn_big, bk=min(2048, act.shape[1]),
                       epilogue=functools.partial(_ep_residual, alpha), out_dtypes=[F32], extras=[res2],
                       name="mlp_down_residual")
    (y,) = _layer_norm(y_pre, ln2_g, ln2_b, bm=bm_ln, out_dtypes=[F32])
    return (y.reshape(B, S, D), k_new, v_new, C, n_st.reshape(B, M_HEADS, M_DQK), m.reshape(B, M_HEADS), Wb)


def kernel(x_prompt, x_sample, cache_swa_k, cache_swa_v, state_mlstm_C, state_mlstm_n, state_mlstm_m,
           w_in, b_igate, b_fgate, w_mnorm, attn_sink, w_branch_a, w_branch_b, w_out,
           ln1_g, ln1_b, w_up, w_down, ln2_g, ln2_b):
    depth = w_in.shape[0]
    alpha = (2.0 * depth) ** 0.25
    S, T = x_prompt.shape[1], x_sample.shape[1]
    pos_p = jnp.arange(S, dtype=jnp.int32)
    pos_s = PAST_LEN + jnp.arange(T, dtype=jnp.int32)
    xp, xs = x_prompt, x_sample
    outs_p, outs_s = [], []
    for l in range(depth):
        W = dict(w_in_t=_regroup_w_in_t(w_in[l].T), br_a=w_branch_a[l], br_b=w_branch_b[l], out=w_out[l],
                 up=w_up[l], down=w_down[l])
        shared = (b_igate[l], b_fgate[l], w_mnorm[l], attn_sink[l], ln1_g[l], ln1_b[l], ln2_g[l], ln2_b[l], alpha)
        xp, *rest_p, Wb = _layer(xp, pos_p, W, *shared, None, None, bm=min(1024, xp.shape[0] * S))
        outs_p.append(rest_p)
        xs, *rest_s, _ = _layer(xs, pos_s, dict(W, **Wb), *shared,
                                (state_mlstm_C[l], state_mlstm_n[l], state_mlstm_m[l]),
                                (cache_swa_k[l], cache_swa_v[l]), bm=xs.shape[0] * T)
        outs_s.append(rest_s)
    stack = lambda outs, i: jnp.stack([o[i] for o in outs])
    pk = jnp.stack([o[0][:, -WINDOW:] for o in outs_p])
    pv = jnp.stack([o[1][:, -WINDOW:] for o in outs_p])
    return (xp, xs, pk, pv, stack(outs_p, 2), stack(outs_p, 3), stack(outs_p, 4),
            stack(outs_s, 0), stack(outs_s, 1), stack(outs_s, 2), stack(outs_s, 3), stack(outs_s, 4))
```

```python
import functools

import jax
import jax.numpy as jnp
from jax import lax
from jax.experimental import pallas as pl
from jax.experimental.pallas import tpu as pltpu

CHUNK = 64
M_HEADS = 8
M_DQK = 128
M_DV = 256
A_HEADS = 32
A_KV_HEADS = 4
A_GROUP = A_HEADS // A_KV_HEADS
A_HD = 64
WINDOW = 128
WIN_CHUNKS = WINDOW // CHUNK
ROPE_THETA = 10000.0
PAST_LEN = 1024
N_BRANCH = 2
LN_EPS = 1e-5
RMS_EPS = 1e-6
LOG2_E = 1.4426950408889634

M_QK_W = M_HEADS * M_DQK
M_V_W = M_HEADS * M_DV
A_Q_W = A_HEADS * A_HD
A_KV_W = A_KV_HEADS * A_HD

V7X_VMEM_BYTES = 64 * 1024 * 1024
V7X_LANES = 128
V7X_SUBLANES = 8
VMEM_REQUEST_CAP = (V7X_VMEM_BYTES * 7) // 8

F32 = jnp.float32
BF16 = jnp.bfloat16
NT_DIMS = (((1,), (1,)), ((), ()))
TN_DIMS = (((0,), (0,)), ((), ()))


def _nbytes(shape, dtype):
    n = 1
    for s in shape:
        n *= s
    return n * jnp.dtype(dtype).itemsize


def _vmem_limit(pipelined, resident=0):
    return int(min(VMEM_REQUEST_CAP, 2 * sum(pipelined) + resident + (4 << 20)))


def _side_cast_specs(side_casts, n_steps, step_of):
    specs, shapes, nbytes = [], [], []
    for a in side_casts:
        rows, cols = a.shape
        assert rows % n_steps == 0 and (rows // n_steps) % (2 * V7X_SUBLANES) == 0, (a.shape, n_steps)
        slab = rows // n_steps
        specs.append(pl.BlockSpec((slab, cols), lambda *ids: (step_of(*ids), 0)))
        shapes.append(jax.ShapeDtypeStruct(a.shape, BF16))
        nbytes.append(_nbytes((slab, cols), F32) + _nbytes((slab, cols), BF16))
    return specs, shapes, nbytes


def _matmul(x, w, *, bm, bn, bk, epilogue, out_dtypes, extras=(), w_nk=False, w_col0=0, n_cols=None,
            emit_x_bf16=False, side_casts=(), name):
    M, K = x.shape
    N = (w.shape[0] if w_nk else w.shape[1]) if n_cols is None else n_cols
    assert M % bm == 0 and N % bn == 0 and K % bk == 0 and w_col0 % bn == 0, (M, N, K, bm, bn, bk, w_col0)
    nm, nn, nk = M // bm, N // bn, K // bk
    assert not emit_x_bf16 or nk == 1
    j0 = w_col0 // bn
    n_ex, n_out, n_sc = len(extras), len(out_dtypes), len(side_casts)
    n_xo = 1 if emit_x_bf16 else 0

    def body(*refs):
        x_ref, w_ref = refs[0], refs[1]
        ex_refs = refs[2:2 + n_ex]
        sc_in = refs[2 + n_ex:2 + n_ex + n_sc]
        o0 = 2 + n_ex + n_sc
        out_refs = refs[o0:o0 + n_out]
        sc_out = refs[o0 + n_out + n_xo:o0 + n_out + n_xo + n_sc]
        for src, dst in zip(sc_in, sc_out):
            dst[...] = src[...].astype(dst.dtype)

        def product():
            xv = x_ref[...]
            if emit_x_bf16:
                xv = xv.astype(BF16)
                refs[o0 + n_out][...] = xv
            if w_nk:
                return lax.dot_general(xv, w_ref[...], NT_DIMS, preferred_element_type=F32)
            return jnp.dot(xv, w_ref[...], preferred_element_type=F32)

        def finish(acc):
            outs = epilogue(acc, *[r[...] for r in ex_refs])
            for o_ref, o in zip(out_refs, outs):
                o_ref[...] = o.astype(o_ref.dtype)

        if nk == 1:
            finish(product())
        else:
            acc_ref = out_refs[0]
            k = pl.program_id(2)

            @pl.when(k == 0)
            def _():
                acc_ref[...] = product()

            @pl.when(jnp.logical_and(k > 0, k < nk - 1))
            def _():
                acc_ref[...] = acc_ref[...] + product()

            @pl.when(k == nk - 1)
            def _():
                finish(acc_ref[...] + product())

    w_spec = (pl.BlockSpec((bn, bk), lambda i, j, k: (j0 + j, k)) if w_nk
              else pl.BlockSpec((bk, bn), lambda i, j, k: (k, j0 + j)))
    sc_specs, sc_shapes, sc_bytes = _side_cast_specs(side_casts, nm * nn * nk, lambda i, j, k: (i * nn + j) * nk + k)
    in_specs = [pl.BlockSpec((bm, bk), lambda i, j, k: (i, k)), w_spec]
    in_specs += [pl.BlockSpec(bs, im) for (_, bs, im) in extras] + sc_specs
    out_specs = [pl.BlockSpec((bm, bn), lambda i, j, k: (i, j)) for _ in out_dtypes]
    out_shape = [jax.ShapeDtypeStruct((M, N), dt) for dt in out_dtypes]
    if emit_x_bf16:
        out_specs.append(pl.BlockSpec((bm, bk), lambda i, j, k: (i, k)))
        out_shape.append(jax.ShapeDtypeStruct((M, K), BF16))
    out_specs += sc_specs
    out_shape += sc_shapes
    assert nk == 1 or list(out_dtypes) == [F32]
    pipelined = [_nbytes((bm, bk), x.dtype), _nbytes((bk, bn), w.dtype)]
    pipelined += [_nbytes(bs, a.dtype) for (a, bs, _) in extras]
    pipelined += [_nbytes((bm, bn), dt) for dt in out_dtypes] + sc_bytes
    resident = _nbytes((bm, bn), F32) * (1 if nk > 1 else 3)
    if emit_x_bf16:
        pipelined.append(_nbytes((bm, bk), BF16))
        resident += _nbytes((bm, bk), BF16)
    semantics = ("arbitrary",) * 3 if (n_sc or emit_x_bf16) else ("parallel", "parallel", "arbitrary")
    outs = pl.pallas_call(
        body,
        grid=(nm, nn, nk),
        in_specs=in_specs,
        out_specs=out_specs,
        out_shape=out_shape,
        compiler_params=pltpu.CompilerParams(
            dimension_semantics=semantics,
            vmem_limit_bytes=_vmem_limit(pipelined, resident)),
        name=name,
    )(x, w, *[a for (a, _, _) in extras], *side_casts)
    return outs


TOKEN_BLOCK = V7X_LANES


def _matmul_nt(wt, x, *, bm, bn, epilogue, out_dtype, extras=(), w_row0=0, n_rows=None, side_casts=(), name):
    K = wt.shape[1]
    N = wt.shape[0] if n_rows is None else n_rows
    M = x.shape[0]
    tb = TOKEN_BLOCK
    assert M % bm == 0 and N % bn == 0 and w_row0 % bn == 0 and bm % tb == 0
    j0 = w_row0 // bn
    nm, nn = M // bm, N // bn
    n_ex, n_sc = len(extras), len(side_casts)

    def body(w_ref, x_ref, *refs):
        ex_refs = refs[:n_ex]
        sc_in = refs[n_ex:n_ex + n_sc]
        o_ref = refs[n_ex + n_sc]
        sc_out = refs[n_ex + n_sc + 1:]
        for src, dst in zip(sc_in, sc_out):
            dst[...] = src[...].astype(dst.dtype)
        acc = lax.dot_general(w_ref[...], x_ref[...], NT_DIMS, preferred_element_type=F32)
        (out,) = epilogue(acc, *[r[...] for r in ex_refs])
        for t in range(bm // tb):
            o_ref[t] = out[:, t * tb:(t + 1) * tb].astype(o_ref.dtype)

    sc_specs, sc_shapes, sc_bytes = _side_cast_specs(side_casts, nm * nn, lambda i, j: i * nn + j)
    pipelined = [_nbytes((bn, K), wt.dtype), _nbytes((bm, K), x.dtype), _nbytes((bn, bm), out_dtype)]
    pipelined += [_nbytes(bs, a.dtype) for (a, bs, _) in extras] + sc_bytes
    outs = pl.pallas_call(
        body,
        grid=(nm, nn),
        in_specs=[pl.BlockSpec((bn, K), lambda i, j: (j0 + j, 0)),
                  pl.BlockSpec((bm, K), lambda i, j: (i, 0))]
        + [pl.BlockSpec(bs, im) for (_, bs, im) in extras] + sc_specs,
        out_specs=[pl.BlockSpec((bm // tb, bn, tb), lambda i, j: (i, j, 0))] + sc_specs,
        out_shape=[jax.ShapeDtypeStruct((M // tb, N, tb), out_dtype)] + sc_shapes,
        compiler_params=pltpu.CompilerParams(
            dimension_semantics=("arbitrary", "arbitrary") if n_sc else ("parallel", "parallel"),
            vmem_limit_bytes=_vmem_limit(pipelined, 3 * _nbytes((bn, bm), F32))),
        name=name,
    )(wt, x, *[a for (a, _, _) in extras], *side_casts)
    return outs


def _ep_scale(scale, acc):
    return (acc * scale,)


def _ep_colscale(acc, scale_row):
    return (acc * scale_row,)


def _ep_sigmoid(acc):
    return (0.5 * jnp.tanh(0.5 * acc) + 0.5,)


def _ep_identity(acc):
    return (acc,)


def _ep_relu_sq(acc):
    r = jnp.maximum(acc, 0.0)
    return (r * r,)


def _ep_rope(acc, cos, sin_signed):
    width = acc.shape[1]
    half = A_HD // 2
    lane = lax.broadcasted_iota(jnp.int32, acc.shape, 1)
    first_half = (lane % A_HD) < half
    partner = jnp.where(first_half, pltpu.roll(acc, width - half, 1), pltpu.roll(acc, half, 1))
    return (acc * cos + partner * sin_signed,)


def _ep_rope_lead(acc, cos, sin_signed):
    w = cos.shape[1]
    (lead,) = _ep_rope(acc[:, :w], cos, sin_signed)
    return (jnp.concatenate([lead, acc[:, w:]], axis=1),)


def _ep_rope_t(acc, cos_head, sin_head):
    rows = acc.shape[0]
    half = A_HD // 2
    row = lax.broadcasted_iota(jnp.int32, acc.shape, 0)
    first_half = (row % A_HD) < half
    partner = jnp.where(first_half, pltpu.roll(acc, rows - half, 0), pltpu.roll(acc, half, 0))
    cos = jnp.concatenate([cos_head] * (rows // A_HD), axis=0)
    sin_signed = jnp.concatenate([sin_head] * (rows // A_HD), axis=0)
    return (acc * cos + partner * sin_signed,)


def _ep_residual(alpha, acc, res):
    return (alpha * res + acc,)


def _merge(ya, yb, wa, wb, gates, g0_col, g1_col, *, bm, bn):
    M, Ka = ya.shape
    Kb = yb.shape[1]
    N = wa.shape[1]
    assert M % bm == 0 and N % bn == 0 and g0_col % bn == 0 and g1_col % bn == 0
    o0, o1 = g0_col // bn, g1_col // bn

    def body(ya_ref, yb_ref, wa_ref, wb_ref, g0_ref, g1_ref, o_ref):
        a = jnp.dot(ya_ref[...], wa_ref[...], preferred_element_type=F32)
        b = jnp.dot(yb_ref[...], wb_ref[...], preferred_element_type=F32)
        o_ref[...] = (g0_ref[...].astype(F32) * a + g1_ref[...].astype(F32) * b).astype(o_ref.dtype)

    pipelined = [_nbytes((bm, Ka), BF16), _nbytes((bm, Kb), BF16), _nbytes((Ka, bn), BF16),
                 _nbytes((Kb, bn), BF16), 3 * _nbytes((bm, bn), BF16)]
    return pl.pallas_call(
        body,
        grid=(M // bm, N // bn),
        in_specs=[pl.BlockSpec((bm, Ka), lambda i, j: (i, 0)),
                  pl.BlockSpec((bm, Kb), lambda i, j: (i, 0)),
                  pl.BlockSpec((Ka, bn), lambda i, j: (0, j)),
                  pl.BlockSpec((Kb, bn), lambda i, j: (0, j)),
                  pl.BlockSpec((bm, bn), lambda i, j: (i, o0 + j)),
                  pl.BlockSpec((bm, bn), lambda i, j: (i, o1 + j))],
        out_specs=pl.BlockSpec((bm, bn), lambda i, j: (i, j)),
        out_shape=jax.ShapeDtypeStruct((M, N), BF16),
        compiler_params=pltpu.CompilerParams(
            dimension_semantics=("parallel", "parallel"),
            vmem_limit_bytes=_vmem_limit(pipelined, 2 * _nbytes((bm, bn), F32))),
        name="branch_merge",
    )(ya, yb, wa, wb, gates, gates)


def _layer_norm(x, g, b, *, bm, out_dtypes):
    M, D = x.shape
    assert M % bm == 0

    def body(x_ref, g_ref, b_ref, *o_refs):
        xf = x_ref[...]
        mu = jnp.mean(xf, axis=-1, keepdims=True)
        xc = xf - mu
        var = jnp.mean(xc * xc, axis=-1, keepdims=True)
        y = xc * lax.rsqrt(var + LN_EPS) * g_ref[...] + b_ref[...]
        for o_ref in o_refs:
            o_ref[...] = y.astype(o_ref.dtype)

    pipelined = [_nbytes((bm, D), F32)] + [_nbytes((bm, D), dt) for dt in out_dtypes]
    return pl.pallas_call(
        body,
        grid=(M // bm,),
        in_specs=[pl.BlockSpec((bm, D), lambda i: (i, 0)),
                  pl.BlockSpec((1, D), lambda i: (0, 0)),
                  pl.BlockSpec((1, D), lambda i: (0, 0))],
        out_specs=[pl.BlockSpec((bm, D), lambda i: (i, 0)) for _ in out_dtypes],
        out_shape=[jax.ShapeDtypeStruct((M, D), dt) for dt in out_dtypes],
        compiler_params=pltpu.CompilerParams(
            dimension_semantics=("parallel",),
            vmem_limit_bytes=_vmem_limit(pipelined, 2 * _nbytes((bm, D), F32))),
        name="layer_norm",
    )(x, g.reshape(1, D), b.reshape(1, D))


MLSTM_HEADS_PER_STEP = 4


def _mlstm(qkv, og, gates_t, b_ig, b_fg, w_mnorm, C0, n0, m0, *, L):
    B, S, _ = qkv.shape
    nC = S // L
    nCp = gates_t.shape[2]
    H, dk, dv = M_HEADS, M_DQK, M_DV
    hb = MLSTM_HEADS_PER_STEP
    ng = H // hb
    k_blk0 = M_QK_W // (hb * dk)
    v_blk0 = (2 * M_QK_W) // (hb * dv)

    def body(big_ref, bfg_ref, q_ref, k_ref, v_ref, og_ref, ig_ref, fg_ref, wm_ref, C0_ref, n0_ref, m0_ref,
             ya_ref, C_ref, n_ref, m_ref, b_s, ig_s):
        g = pl.program_id(1)
        r_i = lax.broadcasted_iota(jnp.int32, (L, L), 0)
        c_i = lax.broadcasted_iota(jnp.int32, (L, L), 1)
        tri_incl = (r_i <= c_i).astype(F32)
        for j in range(hb):
            ig_s[j] = ig_ref[0, j] + big_ref[g * hb + j]
            lf_all = jax.nn.log_sigmoid(fg_ref[0, j] + bfg_ref[g * hb + j])
            b_s[j] = jnp.dot(lf_all, tri_incl, precision=lax.Precision.HIGHEST, preferred_element_type=F32)
        C_ref[...] = C0_ref[...]
        n_ref[...] = n0_ref[...]
        m_ref[...] = m0_ref[...]
        eye = r_i == c_i
        causal = c_i <= r_i

        def col_of(row):
            return jnp.sum(jnp.where(eye, jnp.broadcast_to(row, (L, L)), 0.0), axis=1, keepdims=True)

        def head_chunk(j, c, rows):
            q = q_ref[0, rows, j * dk:(j + 1) * dk]
            k = k_ref[0, rows, j * dk:(j + 1) * dk]
            v = v_ref[0, rows, j * dv:(j + 1) * dv]
            b_row = b_s[j, pl.ds(c, 1), :]
            ig_row = ig_s[j, pl.ds(c, 1), :]
            m_prev = m_ref[0, j]
            C_prev = C_ref[0, j]
            n_prev = n_ref[0, j]

            b_col = col_of(b_row)
            dlog = b_col - jnp.broadcast_to(b_row, (L, L)) + jnp.broadcast_to(ig_row, (L, L))
            dlog = jnp.where(causal, dlog, -jnp.inf)
            inter = b_col + m_prev
            m_t = jnp.maximum(inter, jnp.max(dlog, axis=1, keepdims=True))
            qk = lax.dot_general(q, k, NT_DIMS, preferred_element_type=F32)
            s = qk * jnp.exp(dlog - m_t)
            a = jnp.exp(inter - m_t)
            qC = lax.dot_general(q, C_prev.astype(BF16), NT_DIMS, preferred_element_type=F32)
            sv = jnp.dot(s.astype(BF16), v, preferred_element_type=F32)
            num = a * qC + sv
            qn = jnp.sum(q.astype(F32) * n_prev, axis=1, keepdims=True)
            den = a * qn + jnp.sum(s, axis=1, keepdims=True)
            hid = num / jnp.maximum(jnp.abs(den), jnp.exp(-m_t))
            hn = hid * lax.rsqrt(jnp.mean(hid * hid, axis=1, keepdims=True) + RMS_EPS)
            hn = hn * wm_ref[:, j * dv:(j + 1) * dv]
            gate = og_ref[0, rows, j * dv:(j + 1) * dv].astype(F32)
            ya_ref[0, rows, j * dv:(j + 1) * dv] = (gate * hn).astype(ya_ref.dtype)

            bL = b_row[:, L - 1:L]
            g_row = bL - b_row + ig_row
            m_new = jnp.maximum(bL + m_prev, jnp.max(g_row, axis=1, keepdims=True))
            wk_col = col_of(jnp.exp(g_row - m_new))
            decay = jnp.exp(bL + m_prev - m_new)
            kw = k.astype(F32) * wk_col
            vk = lax.dot_general(v, kw.astype(BF16), TN_DIMS, preferred_element_type=F32)
            C_ref[0, j] = decay * C_prev + vk
            n_ref[0, j] = decay * n_prev + jnp.sum(kw, axis=0, keepdims=True)
            m_ref[0, j] = m_new

        def chunk(c, carry):
            rows = pl.ds(pl.multiple_of(c * L, L), L)
            for j in range(hb):
                head_chunk(j, c, rows)
            return carry

        lax.fori_loop(0, nC, chunk, 0)

    smem = pl.BlockSpec(memory_space=pltpu.SMEM)
    pipelined = [2 * _nbytes((S, hb * dk), BF16), 3 * _nbytes((S, hb * dv), BF16),
                 2 * _nbytes((hb, nCp, L), F32), 2 * _nbytes((hb, dv, dk), F32)]
    ya, C, n, m = pl.pallas_call(
        body,
        grid=(B, ng),
        in_specs=[smem, smem,
                  pl.BlockSpec((1, S, hb * dk), lambda b, g: (b, 0, g)),
                  pl.BlockSpec((1, S, hb * dk), lambda b, g: (b, 0, k_blk0 + g)),
                  pl.BlockSpec((1, S, hb * dv), lambda b, g: (b, 0, v_blk0 + g)),
                  pl.BlockSpec((1, S, hb * dv), lambda b, g: (b, 0, g)),
                  pl.BlockSpec((1, hb, nCp, L), lambda b, g: (b, g, 0, 0)),
                  pl.BlockSpec((1, hb, nCp, L), lambda b, g: (b, ng + g, 0, 0)),
                  pl.BlockSpec((1, hb * dv), lambda b, g: (0, g)),
                  pl.BlockSpec((1, hb, dv, dk), lambda b, g: (b, g, 0, 0)),
                  pl.BlockSpec((1, hb, 1, dk), lambda b, g: (b, g, 0, 0)),
                  pl.BlockSpec((1, hb, 1, 1), lambda b, g: (b, g, 0, 0))],
        out_specs=[pl.BlockSpec((1, S, hb * dv), lambda b, g: (b, 0, g)),
                   pl.BlockSpec((1, hb, dv, dk), lambda b, g: (b, g, 0, 0)),
                   pl.BlockSpec((1, hb, 1, dk), lambda b, g: (b, g, 0, 0)),
                   pl.BlockSpec((1, hb, 1, 1), lambda b, g: (b, g, 0, 0))],
        out_shape=[jax.ShapeDtypeStruct((B, S, M_V_W), BF16),
                   jax.ShapeDtypeStruct((B, H, dv, dk), F32),
                   jax.ShapeDtypeStruct((B, H, 1, dk), F32),
                   jax.ShapeDtypeStruct((B, H, 1, 1), F32)],
        scratch_shapes=[pltpu.VMEM((hb, nCp, L), F32), pltpu.VMEM((hb, nCp, L), F32)],
        compiler_params=pltpu.CompilerParams(
            dimension_semantics=("parallel", "parallel"),
            vmem_limit_bytes=_vmem_limit(pipelined, 4 << 20)),
        name="mlstm_chunks",
    )(b_ig, b_fg, qkv, qkv, qkv, og, gates_t, gates_t, w_mnorm.reshape(1, M_V_W), C0, n0, m0)
    return ya, C, n, m


def _mlstm_prompt(qt, k, vt, ogt, gates_p, b_ig, b_fg, wm_b, *, B, S):
    TB = TOKEN_BLOCK
    assert TB == 2 * CHUNK and S % TB == 0
    nP = S // TB
    nPp = gates_p.shape[2]
    H, dk, dv = M_HEADS, M_DQK, M_DV
    hb = MLSTM_HEADS_PER_STEP
    ng = H // hb
    dva = dv + V7X_SUBLANES
    NEG = -jnp.inf

    def body(big_ref, bfg_ref, q_ref, k_ref, v_ref, og_ref, ig_ref, fg_ref, wm_ref,
             ya_ref, C_ref, n_ref, m_ref, caug_s, row_s, u_s):
        g = pl.program_id(1)
        s_i = lax.broadcasted_iota(jnp.int32, (TB, TB), 0)
        t_i = lax.broadcasted_iota(jnp.int32, (TB, TB), 1)
        causal = s_i <= t_i
        same_chunk = (s_i // CHUNK) == (t_i // CHUNK)
        intra = jnp.logical_and(causal, same_chunk)
        eye = s_i == t_i
        tri_chunk = intra.astype(F32)
        lane_p = lax.broadcasted_iota(jnp.int32, (nPp, TB), 1)
        first_p = lane_p < CHUNK
        first_1 = lax.broadcasted_iota(jnp.int32, (1, TB), 1) < CHUNK
        for j in range(hb):
            ig = ig_ref[0, j] + big_ref[g * hb + j]
            lf = jax.nn.log_sigmoid(fg_ref[0, j] + bfg_ref[g * hb + j])
            bc = jnp.dot(lf, tri_chunk, precision=lax.Precision.HIGHEST, preferred_element_type=F32)
            b_a = jnp.broadcast_to(bc[:, CHUNK - 1:CHUNK], (nPp, TB))
            b_b = jnp.broadcast_to(bc[:, TB - 1:TB], (nPp, TB))
            bp = bc + jnp.where(first_p, 0.0, b_a)
            gl = jnp.where(first_p, b_a, b_b) - bc + ig
            gmax_a = jnp.max(jnp.where(first_p, gl, NEG), axis=1, keepdims=True)
            gmax_b = jnp.max(jnp.where(first_p, NEG, gl), axis=1, keepdims=True)
            row_s[j, 0] = bc
            row_s[j, 1] = bp
            row_s[j, 2] = ig - bp
            row_s[j, 6] = b_a + b_b
            row_s[j, 3] = b_a
            row_s[j, 4] = b_b
            row_s[j, 5] = jnp.broadcast_to(gmax_a, (nPp, TB))
            row_s[j, 7] = jnp.broadcast_to(gmax_b, (nPp, TB))
            caug_s[j] = jnp.zeros((dva, dk), F32)

        def scan(p, m_rows):
            nxt = []
            for j in range(hb):
                m0 = m_rows[j]
                r = pl.ds(p, 1)
                m1 = jnp.maximum(row_s[j, 3, r, :] + m0, row_s[j, 5, r, :])
                m2 = jnp.maximum(row_s[j, 4, r, :] + m1, row_s[j, 7, r, :])
                row_s[j, 3, r, :] = jnp.where(first_1, m0, m1)
                row_s[j, 4, r, :] = m0
                row_s[j, 5, r, :] = m2
                u_row = row_s[j, 2, r, :]
                u_col = jnp.sum(jnp.where(eye, jnp.broadcast_to(u_row, (TB, TB)), 0.0), axis=1, keepdims=True)
                u_s[j, p] = jnp.broadcast_to(u_col, (TB, TB))
                nxt.append(m2)
            return tuple(nxt)

        m_fin = lax.fori_loop(0, nP, scan, tuple(jnp.zeros((1, TB), F32) for _ in range(hb)))

        def head_block(j, p, rows):
            kp = k_ref[0, rows, j * dk:(j + 1) * dk]
            q_t = q_ref[p, j * dk:(j + 1) * dk, :]
            v_t = v_ref[p, j * dv:(j + 1) * dv, :]
            r = pl.ds(p, 1)
            bc, bp, u_row = row_s[j, 0, r, :], row_s[j, 1, r, :], row_s[j, 2, r, :]
            m_chunk, m_start, m_end, b_tot = row_s[j, 3, r, :], row_s[j, 4, r, :], row_s[j, 5, r, :], row_s[j, 6, r, :]
            d_t = bp + u_s[j, p]
            m_t = jnp.maximum(bc + m_chunk, jnp.max(jnp.where(intra, d_t, NEG), axis=0, keepdims=True))
            e_t = jnp.exp(jnp.where(causal, d_t, NEG) - m_t)
            p_t = jnp.dot(kp, q_t, preferred_element_type=F32) * e_t
            a = jnp.exp(bp + m_start - m_t)
            caug = caug_s[j]
            cq = jnp.dot(caug.astype(BF16), q_t, preferred_element_type=F32)
            num = a * cq[:dv] + jnp.dot(v_t, p_t.astype(BF16), preferred_element_type=F32)
            den = a * cq[dv:dv + 1] + jnp.sum(p_t, axis=0, keepdims=True)
            hid = num / jnp.maximum(jnp.abs(den), jnp.exp(-m_t))
            hn = hid * lax.rsqrt(jnp.mean(hid * hid, axis=0, keepdims=True) + RMS_EPS)
            hn = hn * wm_ref[j * dv:(j + 1) * dv, :] * og_ref[p, j * dv:(j + 1) * dv, :].astype(F32)
            for c0 in range(0, dv, TB):
                ya_ref[0, rows, j * dv + c0:j * dv + c0 + TB] = hn[c0:c0 + TB].T.astype(ya_ref.dtype)

            wk = jnp.exp(b_tot + u_row - m_end)
            decay = jnp.exp(b_tot + m_start - m_end)
            vw = (v_t.astype(F32) * wk).astype(BF16)
            vw = jnp.concatenate([vw, jnp.broadcast_to(wk, (V7X_SUBLANES, TB)).astype(BF16)], axis=0)
            caug_s[j] = decay * caug + jnp.dot(vw, kp, preferred_element_type=F32)

        def block(p, carry):
            rows = pl.ds(pl.multiple_of(p * TB, TB), TB)
            for j in range(hb):
                head_block(j, p, rows)
            return carry

        lax.fori_loop(0, nP, block, 0)
        for j in range(hb):
            C_ref[0, j] = caug_s[j, :dv, :]
            n_ref[0, j] = caug_s[j, dv:dv + 1, :]
            m_ref[0, j] = m_fin[j][:, :1]

    smem = pl.BlockSpec(memory_space=pltpu.SMEM)
    pipelined = [_nbytes((nP, hb * dk, TB), BF16), _nbytes((S, hb * dk), BF16), 3 * _nbytes((S, hb * dv), BF16),
                 2 * _nbytes((hb, nPp, TB), F32), _nbytes((hb * dv, TB), F32), _nbytes((hb, dv, dk), F32)]
    scratch = [pltpu.VMEM((hb, dva, dk), F32), pltpu.VMEM((hb, 8, nPp, TB), F32), pltpu.VMEM((hb, nP, TB, TB), F32)]
    resident = _nbytes((hb, dva, dk), F32) + _nbytes((hb, 8, nPp, TB), F32) + _nbytes((hb, nP, TB, TB), F32)
    ya, C, n, m = pl.pallas_call(
        body,
        grid=(B, ng),
        in_specs=[smem, smem,
                  pl.BlockSpec((nP, hb * dk, TB), lambda b, g: (b, g, 0)),
                  pl.BlockSpec((1, S, hb * dk), lambda b, g: (b, 0, g)),
                  pl.BlockSpec((nP, hb * dv, TB), lambda b, g: (b, g, 0)),
                  pl.BlockSpec((nP, hb * dv, TB), lambda b, g: (b, g, 0)),
                  pl.BlockSpec((1, hb, nPp, TB), lambda b, g: (b, g, 0, 0)),
                  pl.BlockSpec((1, hb, nPp, TB), lambda b, g: (b, ng + g, 0, 0)),
                  pl.BlockSpec((hb * dv, TB), lambda b, g: (g, 0))],
        out_specs=[pl.BlockSpec((1, S, hb * dv), lambda b, g: (b, 0, g)),
                   pl.BlockSpec((1, hb, dv, dk), lambda b, g: (b, g, 0, 0)),
                   pl.BlockSpec((1, hb, 1, dk), lambda b, g: (b, g, 0, 0)),
                   pl.BlockSpec((1, hb, 1, 1), lambda b, g: (b, g, 0, 0))],
        out_shape=[jax.ShapeDtypeStruct((B, S, M_V_W), BF16),
                   jax.ShapeDtypeStruct((B, H, dv, dk), F32),
                   jax.ShapeDtypeStruct((B, H, 1, dk), F32),
                   jax.ShapeDtypeStruct((B, H, 1, 1), F32)],
        scratch_shapes=scratch,
        compiler_params=pltpu.CompilerParams(
            dimension_semantics=("parallel", "parallel"),
            vmem_limit_bytes=_vmem_limit(pipelined, resident + (4 << 20))),
        name="mlstm_prompt_blocks",
    )(b_ig, b_fg, qt, k, vt, ogt, gates_p, gates_p, wm_b)
    return ya, C, n, m


def _swa_prompt(qt, kv, vt, sink, *, B, S):
    TB = 2 * CHUNK
    assert TB == V7X_LANES and S % TB == 0 and WIN_CHUNKS == 2
    nblk = S // TB
    G, KVH = A_GROUP, A_KV_HEADS

    def body(sink_ref, q_ref, kp_ref, kc_ref, vp_ref, vc_ref, o_ref):
        p_id = pl.program_id(1)
        key_chunk = lax.broadcasted_iota(jnp.int32, (2 * TB, TB), 0) // CHUNK
        q_half = lax.broadcasted_iota(jnp.int32, (2 * TB, TB), 1) // CHUNK
        valid = jnp.logical_and(key_chunk >= q_half, key_chunk <= q_half + WIN_CHUNKS)
        valid = jnp.logical_and(valid, jnp.logical_or(key_chunk >= WIN_CHUNKS, p_id > 0))
        bias = jnp.where(valid, 0.0, -jnp.inf).astype(F32)
        bias = jnp.concatenate([bias] * G, axis=1)
        zeros = jnp.zeros((A_HD, TB), q_ref.dtype)
        for kvh in range(KVH):
            pr = kvh // 2
            lanes = slice(pr * V7X_LANES, (pr + 1) * V7X_LANES)
            kc = jnp.concatenate([kp_ref[:, lanes], kc_ref[:, lanes]], axis=0).astype(BF16)
            rows = slice(kvh * A_HD, (kvh + 1) * A_HD)
            vt_w = jnp.concatenate([vp_ref[0, rows, :], vc_ref[0, rows, :]], axis=1)
            pieces, sinks = [], []
            for g in range(G):
                head = kvh * G + g
                qg = q_ref[0, head * A_HD:(head + 1) * A_HD, :]
                pieces.append(jnp.concatenate([zeros, qg] if kvh % 2 else [qg, zeros], axis=0))
                sinks.append(jnp.full((1, TB), sink_ref[head] * LOG2_E, F32))
            qz = jnp.concatenate(pieces, axis=1)
            sk = jnp.concatenate(sinks, axis=1)
            s_t = jnp.dot(kc, qz, preferred_element_type=F32) + bias
            mx = jnp.maximum(jnp.max(s_t, axis=0, keepdims=True), sk)
            p_t = jnp.exp2(s_t - mx)
            den = jnp.sum(p_t, axis=0, keepdims=True) + jnp.exp2(sk - mx)
            o_t = jnp.dot(vt_w, p_t.astype(BF16), preferred_element_type=F32) / den
            for j in range(G // 2):
                two = jnp.concatenate([o_t[:, (2 * j) * TB:(2 * j + 1) * TB],
                                       o_t[:, (2 * j + 1) * TB:(2 * j + 2) * TB]], axis=0)
                col0 = (kvh * G + 2 * j) * A_HD
                o_ref[:, col0:col0 + V7X_LANES] = two.T.astype(o_ref.dtype)

    prev = lambda b, p: b * nblk + jnp.maximum(p - 1, 0)
    cur = lambda b, p: b * nblk + p
    pipelined = [2 * _nbytes((A_Q_W, TB), BF16), 2 * _nbytes((TB, A_KV_W), F32), 2 * _nbytes((A_KV_W, TB), BF16)]
    return pl.pallas_call(
        body,
        grid=(B, nblk),
        in_specs=[pl.BlockSpec(memory_space=pltpu.SMEM),
                  pl.BlockSpec((1, A_Q_W, TB), lambda b, p: (cur(b, p), 0, 0)),
                  pl.BlockSpec((TB, A_KV_W), lambda b, p: (prev(b, p), 0)),
                  pl.BlockSpec((TB, A_KV_W), lambda b, p: (cur(b, p), 0)),
                  pl.BlockSpec((1, A_KV_W, TB), lambda b, p: (prev(b, p), 0, 0)),
                  pl.BlockSpec((1, A_KV_W, TB), lambda b, p: (cur(b, p), 0, 0))],
        out_specs=pl.BlockSpec((TB, A_Q_W), lambda b, p: (cur(b, p), 0)),
        out_shape=jax.ShapeDtypeStruct((B * S, A_Q_W), BF16),
        compiler_params=pltpu.CompilerParams(
            dimension_semantics=("parallel", "arbitrary"),
            vmem_limit_bytes=_vmem_limit(pipelined, 16 << 20)),
        name="swa_prompt_attention",
    )(sink, qt, kv, kv, vt, vt)


def _swa_sample(q, kv, sink):
    B, T, _ = q.shape
    Skv = kv.shape[1]
    G, KVH, LN = A_GROUP, A_KV_HEADS, V7X_LANES
    assert 2 * A_HD == LN and T % 16 == 0

    def body(sink_ref, q_ref, kv_ref, o_ref):
        kvw = kv_ref[0]
        lo_q = lax.broadcasted_iota(jnp.int32, (T, LN), 1) < A_HD
        lo_k = lax.broadcasted_iota(jnp.int32, (Skv, LN), 1) < A_HD
        zero = jnp.zeros((T, LN), q_ref.dtype)
        for kvh in range(KVH):
            pair, odd = kvh // 2, kvh % 2
            kx = kvw[:, pair * LN:(pair + 1) * LN]
            vx = kvw[:, A_KV_W + pair * LN:A_KV_W + (pair + 1) * LN]
            kr = pltpu.roll(kx, A_HD, 1)
            vr = pltpu.roll(vx, A_HD, 1)
            k2 = (jnp.where(lo_k, kr, kx) if odd else jnp.where(lo_k, kx, kr)).astype(BF16)
            v2 = (jnp.where(lo_k, vr, vx) if odd else jnp.where(lo_k, vx, vr)).astype(BF16)
            pieces, sinks = [], []
            for g in range(G):
                head = kvh * G + g
                q2 = q_ref[0, :, (head // 2) * LN:(head // 2 + 1) * LN]
                pieces.append(jnp.where(lo_q, zero, q2) if head % 2 else jnp.where(lo_q, q2, zero))
                sinks.append(jnp.full((T, 1), sink_ref[head], F32))
            qs = jnp.concatenate(pieces, axis=0)
            sk = jnp.concatenate(sinks, axis=0)
            s = lax.dot_general(qs, k2, NT_DIMS, preferred_element_type=F32)
            mx = jnp.maximum(jnp.max(s, axis=1, keepdims=True), sk)
            p = jnp.exp(s - mx)
            den = jnp.sum(p, axis=1, keepdims=True) + jnp.exp(sk - mx)
            o = jnp.dot(p.astype(BF16), v2, preferred_element_type=F32) / den
            for j in range(G // 2):
                pair_o = jnp.where(lo_q, o[(2 * j) * T:(2 * j + 1) * T], o[(2 * j + 1) * T:(2 * j + 2) * T])
                col0 = (kvh * G + 2 * j) * A_HD
                o_ref[0, :, col0:col0 + LN] = pair_o.astype(o_ref.dtype)

    pipelined = [2 * _nbytes((T, A_Q_W), BF16), _nbytes((Skv, 2 * A_KV_W), F32)]
    return pl.pallas_call(
        body,
        grid=(B,),
        in_specs=[pl.BlockSpec(memory_space=pltpu.SMEM),
                  pl.BlockSpec((1, T, A_Q_W), lambda b: (b, 0, 0)),
                  pl.BlockSpec((1, Skv, 2 * A_KV_W), lambda b: (b, 0, 0))],
        out_specs=pl.BlockSpec((1, T, A_Q_W), lambda b: (b, 0, 0)),
        out_shape=jax.ShapeDtypeStruct((B, T, A_Q_W), BF16),
        compiler_params=pltpu.CompilerParams(
            dimension_semantics=("parallel",),
            vmem_limit_bytes=_vmem_limit(pipelined, 8 << 20)),
        name="swa_sample_attention",
    )(sink, q, kv)


def _rope_tables(pos, n_rot_heads, n_plain_cols, scale):
    half = A_HD // 2
    inv = ROPE_THETA ** (-jnp.arange(half, dtype=F32) / half)
    ang = pos.astype(F32)[:, None] * inv[None, :]
    cos, sin = jnp.cos(ang) * scale, jnp.sin(ang) * scale
    cos_h = jnp.concatenate([cos, cos], -1)
    sin_h = jnp.concatenate([-sin, sin], -1)
    n = pos.shape[0]
    cos_t = jnp.concatenate([jnp.tile(cos_h, (1, n_rot_heads)), jnp.ones((n, n_plain_cols), F32)], -1)
    sin_t = jnp.concatenate([jnp.tile(sin_h, (1, n_rot_heads)), jnp.zeros((n, n_plain_cols), F32)], -1)
    return cos_t, sin_t


W_IN_BLOCK = 512


def _in_proj_layout(d_model):
    widths = [("mqkv", 2 * M_QK_W + M_V_W), ("sig", M_V_W + N_BRANCH * d_model), ("aq", A_Q_W),
              ("kv", 2 * A_KV_W), ("gate", W_IN_BLOCK)]
    off, col = {}, 0
    for name, w in widths:
        off[name] = col
        col += w
    off["end"] = col
    return off


def _regroup_w_in_t(w_in_t):
    D = w_in_t.shape[1]
    off = _in_proj_layout(D)
    bw = W_IN_BLOCK
    n_gate = 2 * M_HEADS
    src_gate = 2 * M_QK_W + 2 * M_V_W
    src_aq = src_gate + n_gate
    src_gp = src_aq + A_Q_W + 2 * A_KV_W
    assert src_gate % bw == 0 and off["sig"] % bw == 0 and (N_BRANCH * D) % bw == 0 and n_gate <= bw
    assert src_aq % V7X_SUBLANES == 0
    j_gp = (off["sig"] + M_V_W) // bw
    j_aq = off["aq"] // bw
    j_gate = off["gate"] // bw

    sl = V7X_SUBLANES

    def src_row(j):
        tile = jnp.where(j < j_gp, j * (bw // sl),
                         jnp.where(j < j_aq, src_gp // sl + (j - j_gp) * (bw // sl),
                                   jnp.where(j < j_gate, src_aq // sl + (j - j_aq) * (bw // sl), src_gate // sl)))
        return tile * sl

    def body(a_ref, o_ref):
        j = pl.program_id(0)
        a = a_ref[...]
        row = lax.broadcasted_iota(jnp.int32, a.shape, 0)
        keep = jnp.logical_or(j < j_gate, row < n_gate)
        o_ref[...] = jnp.where(keep, a, 0.0).astype(o_ref.dtype)

    pipelined = [_nbytes((bw, D), F32), _nbytes((bw, D), BF16)]
    return pl.pallas_call(
        body,
        grid=(off["end"] // bw,),
        in_specs=[pl.BlockSpec((pl.Element(bw), pl.Element(D)), lambda j: (src_row(j), 0))],
        out_specs=pl.BlockSpec((bw, D), lambda j: (j, 0)),
        out_shape=jax.ShapeDtypeStruct((off["end"], D), BF16),
        compiler_params=pltpu.CompilerParams(
            dimension_semantics=("parallel",),
            vmem_limit_bytes=_vmem_limit(pipelined, 2 * _nbytes((bw, D), F32))),
        name="regroup_w_in",
    )(w_in_t)


DENSE_WEIGHTS = ("br_a", "br_b", "out", "up", "down")


def _layer(x, pos, W, b_ig, b_fg, w_mnorm, sink, ln1_g, ln1_b, ln2_g, ln2_b, alpha, state, cache, *, bm):
    B, S, D = x.shape
    M = B * S
    L = min(CHUNK, S)
    xf = x.reshape(M, D)
    bn_big = min(1024, D)
    off = _in_proj_layout(D)
    w_in_t = W["w_in_t"]
    Wb = {name: W[name] for name in DENSE_WEIGHTS if W[name].dtype == BF16}

    def pos_extra(table, rows, width):
        if rows <= S:
            per = S // rows
            return (table, (rows, width), lambda i, j, k: (i % per, j))
        return (jnp.tile(table, (rows // S, 1)), (rows, width), lambda i, j, k: (0, j))

    q_scale_m = M_DQK ** -0.5
    wkv = 2 * A_KV_W
    bm0 = min(256, M)
    n_first = wkv + W_IN_BLOCK
    cos_k, sin_k = _rope_tables(pos, A_KV_HEADS, 0, 1.0)
    kvg, xb = _matmul(xf, w_in_t, w_nk=True, bm=bm0, bn=n_first, bk=D, epilogue=_ep_rope_lead, out_dtypes=[F32],
                      extras=[pos_extra(cos_k, bm0, A_KV_W), pos_extra(sin_k, bm0, A_KV_W)],
                      w_col0=off["kv"], n_cols=n_first, emit_x_bf16=True, name="in_proj_attn_kv_if_gates")
    gates = kvg[:, wkv:wkv + 2 * M_HEADS]

    def gate_rows(lanes):
        n = S // lanes
        t = gates.reshape(B, n, lanes, 2 * M_HEADS).transpose(0, 3, 1, 2)
        return jnp.pad(t, ((0, 0), (0, 0), (0, max(n, V7X_SUBLANES) - n), (0, 0)))

    def side(name):
        return [] if name in Wb else [W[name]]

    def keep(name, casts):
        if casts:
            Wb[name] = casts[0]

    o_mo, o_gp = off["sig"], off["sig"] + M_V_W
    if state is None:
        mk, *c = _matmul(xb, w_in_t, w_nk=True, bm=bm, bn=bn_big, bk=D, epilogue=_ep_identity, out_dtypes=[BF16],
                         w_col0=M_QK_W, n_cols=M_QK_W, side_casts=side("br_a"), name="in_proj_mlstm_k")
        keep("br_a", c)
        bn_t = min(1024, M_QK_W)
        mq_t, *c = _matmul_nt(w_in_t, xb, bm=bm, bn=bn_t, epilogue=functools.partial(_ep_scale, q_scale_m),
                              out_dtype=BF16, w_row0=0, n_rows=M_QK_W, side_casts=side("br_b"),
                              name="in_proj_mlstm_q_t")
        keep("br_b", c)
        mv_t, *c = _matmul_nt(w_in_t, xb, bm=bm, bn=bn_t, epilogue=_ep_identity, out_dtype=BF16,
                              w_row0=2 * M_QK_W, n_rows=M_V_W, side_casts=side("out"),
                              name="in_proj_mlstm_v_t")
        keep("out", c)
        (og_t,) = _matmul_nt(w_in_t, xb, bm=bm, bn=bn_t, epilogue=_ep_sigmoid, out_dtype=BF16,
                             w_row0=o_mo, n_rows=M_V_W, name="in_proj_mlstm_o_t")
        br_gates, *c = _matmul(xb, w_in_t, w_nk=True, bm=bm, bn=bn_big, bk=D, epilogue=_ep_sigmoid,
                               out_dtypes=[BF16], w_col0=o_gp, n_cols=N_BRANCH * D, side_casts=side("up"),
                               name="in_proj_branch_gates")
        keep("up", c)
        g_cols = (0, D)
        wm_b = jnp.broadcast_to(w_mnorm[:, None], (M_V_W, TOKEN_BLOCK))
        ya, C, n_st, m = _mlstm_prompt(mq_t, mk.reshape(B, S, M_QK_W), mv_t, og_t, gate_rows(TOKEN_BLOCK),
                                       b_ig, b_fg, wm_b, B=B, S=S)
    else:
        n_qkv = 2 * M_QK_W + M_V_W
        colscale = jnp.concatenate([jnp.full((1, M_QK_W), q_scale_m, F32), jnp.ones((1, M_QK_W + M_V_W), F32)], 1)
        (qkv,) = _matmul(xb, w_in_t, w_nk=True, bm=bm, bn=bn_big, bk=D, epilogue=_ep_colscale, out_dtypes=[BF16],
                         extras=[(colscale, (1, bn_big), lambda i, j, k: (0, j))],
                         w_col0=off["mqkv"], n_cols=n_qkv, name="in_proj_mlstm_qkv")
        (br_gates,) = _matmul(xb, w_in_t, w_nk=True, bm=bm, bn=bn_big, bk=D, epilogue=_ep_sigmoid,
                              out_dtypes=[BF16], w_col0=o_mo, n_cols=M_V_W + N_BRANCH * D,
                              name="in_proj_sigmoid_gates")
        g_cols = (M_V_W, M_V_W + D)
        C0 = state[0].astype(F32)
        n0 = state[1].astype(F32).reshape(B, M_HEADS, 1, M_DQK)
        m0 = state[2].astype(F32).reshape(B, M_HEADS, 1, 1)
        ya, C, n_st, m = _mlstm(qkv.reshape(B, S, -1), br_gates.reshape(B, S, -1), gate_rows(L), b_ig, b_fg,
                                w_mnorm, C0, n0, m0, L=L)

    kv3 = kvg.reshape(B, S, n_first)
    k_new = kv3[:, :, :A_KV_W].reshape(B, S, A_KV_HEADS, A_HD)
    v_new = kv3[:, :, A_KV_W:wkv].reshape(B, S, A_KV_HEADS, A_HD)
    q_scale = A_HD ** -0.5
    wq = 512
    if cache is None:
        cos_q, sin_q = _rope_tables(pos, 1, 0, q_scale * LOG2_E)
        if bm <= S:
            tab = lambda t: (t.T, (A_HD, bm), lambda i, j: (0, i % (S // bm)))
        else:
            tab = lambda t: (jnp.tile(t.T, (1, bm // S)), (A_HD, bm), lambda i, j: (0, 0))
        (qt,) = _matmul_nt(w_in_t, xb, bm=bm, bn=min(1024, A_Q_W), epilogue=_ep_rope_t, out_dtype=BF16,
                           extras=[tab(cos_q), tab(sin_q)], w_row0=off["aq"], n_rows=A_Q_W,
                           name="in_proj_attn_q_t")
        (vt,) = _matmul_nt(w_in_t, xb, bm=bm, bn=A_KV_W, epilogue=_ep_identity, out_dtype=BF16,
                           w_row0=off["kv"] + A_KV_W, n_rows=A_KV_W, name="in_proj_attn_v_t")
        yb = _swa_prompt(qt, kvg, vt, sink, B=B, S=S)
    else:
        cos_q, sin_q = _rope_tables(pos, A_HEADS, 0, q_scale)
        (aq,) = _matmul(xb, w_in_t, w_nk=True, bm=bm, bn=wq, bk=D, epilogue=_ep_rope, out_dtypes=[BF16],
                        extras=[pos_extra(cos_q, bm, wq), pos_extra(sin_q, bm, wq)],
                        w_col0=off["aq"], n_cols=A_Q_W, name="in_proj_attn_q")
        ck = cache[0].astype(F32).reshape(B, WINDOW, A_KV_W)
        cv = cache[1].astype(F32).reshape(B, WINDOW, A_KV_W)
        kv_all = jnp.concatenate([jnp.concatenate([ck, cv], -1), kv3[:, :, :wkv]], axis=1)
        yb = _swa_sample(aq.reshape(B, S, A_Q_W), kv_all, sink).reshape(M, A_Q_W)

    for name in DENSE_WEIGHTS[:3]:
        Wb.setdefault(name, W[name].astype(BF16))
    merged = _merge(ya.reshape(M, M_V_W), yb, Wb["br_a"], Wb["br_b"], br_gates,
                    g_cols[0], g_cols[1], bm=bm, bn=bn_big)
    res1 = (xf, (bm, bn_big), lambda i, j, k: (i, j))
    (h_pre,) = _matmul(merged, Wb["out"], bm=bm, bn=bn_big, bk=D, epilogue=functools.partial(_ep_residual, alpha),
                       out_dtypes=[F32], extras=[res1], name="out_proj_residual")
    bm_ln = min(256, M)
    h32, h16 = _layer_norm(h_pre, ln1_g, ln1_b, bm=bm_ln, out_dtypes=[F32, BF16])
    Wb.setdefault("up", W["up"].astype(BF16))
    act, *c = _matmul(h16, Wb["up"], bm=bm, bn=bn_big, bk=D, epilogue=_ep_relu_sq, out_dtypes=[BF16],
                      side_casts=side("down"), name="mlp_up_relu_sq")
    keep("down", c)
    res2 = (h32, (bm, bn_big), lambda i, j, k: (i, j))
    (y_pre,) = _matmul(act, Wb["down"], bm=bm, bn=bn_big, bk=min(4096, act.shape[1]),
                       epilogue=functools.partial(_ep_residual, alpha), out_dtypes=[F32], extras=[res2],
                       name="mlp_down_residual")
    (y,) = _layer_norm(y_pre, ln2_g, ln2_b, bm=bm_ln, out_dtypes=[F32])
    return (y.reshape(B, S, D), k_new, v_new, C, n_st.reshape(B, M_HEADS, M_DQK), m.reshape(B, M_HEADS), Wb)


def kernel(x_prompt, x_sample, cache_swa_k, cache_swa_v, state_mlstm_C, state_mlstm_n, state_mlstm_m,
           w_in, b_igate, b_fgate, w_mnorm, attn_sink, w_branch_a, w_branch_b, w_out,
           ln1_g, ln1_b, w_up, w_down, ln2_g, ln2_b):
    depth = w_in.shape[0]
    alpha = (2.0 * depth) ** 0.25
    S, T = x_prompt.shape[1], x_sample.shape[1]
    pos_p = jnp.arange(S, dtype=jnp.int32)
    pos_s = PAST_LEN + jnp.arange(T, dtype=jnp.int32)
    xp, xs = x_prompt, x_sample
    outs_p, outs_s = [], []
    for l in range(depth):
        W = dict(w_in_t=_regroup_w_in_t(w_in[l].T), br_a=w_branch_a[l], br_b=w_branch_b[l], out=w_out[l],
                 up=w_up[l], down=w_down[l])
        shared = (b_igate[l], b_fgate[l], w_mnorm[l], attn_sink[l], ln1_g[l], ln1_b[l], ln2_g[l], ln2_b[l], alpha)
        xp, *rest_p, Wb = _layer(xp, pos_p, W, *shared, None, None, bm=min(1024, xp.shape[0] * S))
        outs_p.append(rest_p)
        xs, *rest_s, _ = _layer(xs, pos_s, dict(W, **Wb), *shared,
                                (state_mlstm_C[l], state_mlstm_n[l], state_mlstm_m[l]),
                                (cache_swa_k[l], cache_swa_v[l]), bm=xs.shape[0] * T)
        outs_s.append(rest_s)
    stack = lambda outs, i: jnp.stack([o[i] for o in outs])
    pk = jnp.stack([o[0][:, -WINDOW:] for o in outs_p])
    pv = jnp.stack([o[1][:, -WINDOW:] for o in outs_p])
    return (xp, xs, pk, pv, stack(outs_p, 2), stack(outs_p, 3), stack(outs_p, 4),
            stack(outs_s, 0), stack(outs_s, 1), stack(outs_s, 2), stack(outs_s, 3), stack(outs_s, 4))
```

```python
import functools

import jax
import jax.numpy as jnp
from jax import lax
from jax.experimental import pallas as pl
from jax.experimental.pallas import tpu as pltpu

CHUNK = 64
M_HEADS = 8
M_DQK = 128
M_DV = 256
A_HEADS = 32
A_KV_HEADS = 4
A_GROUP = A_HEADS // A_KV_HEADS
A_HD = 64
WINDOW = 128
WIN_CHUNKS = WINDOW // CHUNK
ROPE_THETA = 10000.0
PAST_LEN = 1024
N_BRANCH = 2
LN_EPS = 1e-5
RMS_EPS = 1e-6
LOG2_E = 1.4426950408889634

M_QK_W = M_HEADS * M_DQK
M_V_W = M_HEADS * M_DV
A_Q_W = A_HEADS * A_HD
A_KV_W = A_KV_HEADS * A_HD

V7X_VMEM_BYTES = 64 * 1024 * 1024
V7X_LANES = 128
V7X_SUBLANES = 8
VMEM_REQUEST_CAP = (V7X_VMEM_BYTES * 15) // 16

F32 = jnp.float32
BF16 = jnp.bfloat16
NT_DIMS = (((1,), (1,)), ((), ()))
TN_DIMS = (((0,), (0,)), ((), ()))


def _nbytes(shape, dtype):
    n = 1
    for s in shape:
        n *= s
    return n * jnp.dtype(dtype).itemsize


SPILL_ALLOWANCE = 6 << 20


def _vmem_limit(pipelined, resident=0):
    return int(min(VMEM_REQUEST_CAP, 2 * sum(pipelined) + resident + SPILL_ALLOWANCE))


def _side_cast_specs(side_casts, n_steps, step_of):
    specs, shapes, nbytes = [], [], []
    for a in side_casts:
        rows, cols = a.shape
        assert rows % n_steps == 0 and (rows // n_steps) % (2 * V7X_SUBLANES) == 0, (a.shape, n_steps)
        slab = rows // n_steps
        specs.append(pl.BlockSpec((slab, cols), lambda *ids: (step_of(*ids), 0)))
        shapes.append(jax.ShapeDtypeStruct(a.shape, BF16))
        nbytes.append(_nbytes((slab, cols), F32) + _nbytes((slab, cols), BF16))
    return specs, shapes, nbytes


def _matmul(x, w, *, bm, bn, bk, epilogue, out_dtypes, extras=(), w_nk=False, w_col0=0, n_cols=None,
            emit_x_bf16=False, side_casts=(), name):
    M, K = x.shape
    N = (w.shape[0] if w_nk else w.shape[1]) if n_cols is None else n_cols
    assert M % bm == 0 and N % bn == 0 and K % bk == 0 and w_col0 % bn == 0, (M, N, K, bm, bn, bk, w_col0)
    nm, nn, nk = M // bm, N // bn, K // bk
    assert not emit_x_bf16 or nk == 1
    j0 = w_col0 // bn
    n_ex, n_out, n_sc = len(extras), len(out_dtypes), len(side_casts)
    n_xo = 1 if emit_x_bf16 else 0

    def body(*refs):
        x_ref, w_ref = refs[0], refs[1]
        ex_refs = refs[2:2 + n_ex]
        sc_in = refs[2 + n_ex:2 + n_ex + n_sc]
        o0 = 2 + n_ex + n_sc
        out_refs = refs[o0:o0 + n_out]
        sc_out = refs[o0 + n_out + n_xo:o0 + n_out + n_xo + n_sc]
        for src, dst in zip(sc_in, sc_out):
            dst[...] = src[...].astype(dst.dtype)

        def product():
            xv = x_ref[...]
            if emit_x_bf16:
                xv = xv.astype(BF16)
                refs[o0 + n_out][...] = xv
            if w_nk:
                return lax.dot_general(xv, w_ref[...], NT_DIMS, preferred_element_type=F32)
            return jnp.dot(xv, w_ref[...], preferred_element_type=F32)

        def finish(acc):
            outs = epilogue(acc, *[r[...] for r in ex_refs])
            for o_ref, o in zip(out_refs, outs):
                o_ref[...] = o.astype(o_ref.dtype)

        if nk == 1:
            finish(product())
        else:
            acc_ref = out_refs[0]
            k = pl.program_id(2)

            @pl.when(k == 0)
            def _():
                acc_ref[...] = product()

            @pl.when(jnp.logical_and(k > 0, k < nk - 1))
            def _():
                acc_ref[...] = acc_ref[...] + product()

            @pl.when(k == nk - 1)
            def _():
                finish(acc_ref[...] + product())

    w_spec = (pl.BlockSpec((bn, bk), lambda i, j, k: (j0 + j, k)) if w_nk
              else pl.BlockSpec((bk, bn), lambda i, j, k: (k, j0 + j)))
    sc_specs, sc_shapes, sc_bytes = _side_cast_specs(side_casts, nm * nn * nk, lambda i, j, k: (i * nn + j) * nk + k)
    in_specs = [pl.BlockSpec((bm, bk), lambda i, j, k: (i, k)), w_spec]
    in_specs += [pl.BlockSpec(bs, im) for (_, bs, im) in extras] + sc_specs
    out_specs = [pl.BlockSpec((bm, bn), lambda i, j, k: (i, j)) for _ in out_dtypes]
    out_shape = [jax.ShapeDtypeStruct((M, N), dt) for dt in out_dtypes]
    if emit_x_bf16:
        out_specs.append(pl.BlockSpec((bm, bk), lambda i, j, k: (i, k)))
        out_shape.append(jax.ShapeDtypeStruct((M, K), BF16))
    out_specs += sc_specs
    out_shape += sc_shapes
    assert nk == 1 or list(out_dtypes) == [F32]
    pipelined = [_nbytes((bm, bk), x.dtype), _nbytes((bk, bn), w.dtype)]
    pipelined += [_nbytes(bs, a.dtype) for (a, bs, _) in extras]
    pipelined += [_nbytes((bm, bn), dt) for dt in out_dtypes] + sc_bytes
    resident = _nbytes((bm, bn), F32) * (1 if nk > 1 else 3)
    if emit_x_bf16:
        pipelined.append(_nbytes((bm, bk), BF16))
        resident += _nbytes((bm, bk), BF16)
    semantics = ("arbitrary",) * 3 if (n_sc or emit_x_bf16) else ("parallel", "parallel", "arbitrary")
    outs = pl.pallas_call(
        body,
        grid=(nm, nn, nk),
        in_specs=in_specs,
        out_specs=out_specs,
        out_shape=out_shape,
        compiler_params=pltpu.CompilerParams(
            dimension_semantics=semantics,
            vmem_limit_bytes=_vmem_limit(pipelined, resident)),
        name=name,
    )(x, w, *[a for (a, _, _) in extras], *side_casts)
    return outs


TOKEN_BLOCK = V7X_LANES


def _matmul_nt(wt, x, *, bm, bn, epilogue, out_dtype, extras=(), w_row0=0, n_rows=None, side_casts=(), name):
    K = wt.shape[1]
    N = wt.shape[0] if n_rows is None else n_rows
    M = x.shape[0]
    tb = TOKEN_BLOCK
    assert M % bm == 0 and N % bn == 0 and w_row0 % bn == 0 and bm % tb == 0
    j0 = w_row0 // bn
    nm, nn = M // bm, N // bn
    n_ex, n_sc = len(extras), len(side_casts)

    def body(w_ref, x_ref, *refs):
        ex_refs = refs[:n_ex]
        sc_in = refs[n_ex:n_ex + n_sc]
        o_ref = refs[n_ex + n_sc]
        sc_out = refs[n_ex + n_sc + 1:]
        for src, dst in zip(sc_in, sc_out):
            dst[...] = src[...].astype(dst.dtype)
        acc = lax.dot_general(w_ref[...], x_ref[...], NT_DIMS, preferred_element_type=F32)
        (out,) = epilogue(acc, *[r[...] for r in ex_refs])
        for t in range(bm // tb):
            o_ref[t] = out[:, t * tb:(t + 1) * tb].astype(o_ref.dtype)

    sc_specs, sc_shapes, sc_bytes = _side_cast_specs(side_casts, nm * nn, lambda i, j: i * nn + j)
    pipelined = [_nbytes((bn, K), wt.dtype), _nbytes((bm, K), x.dtype), _nbytes((bn, bm), out_dtype)]
    pipelined += [_nbytes(bs, a.dtype) for (a, bs, _) in extras] + sc_bytes
    outs = pl.pallas_call(
        body,
        grid=(nm, nn),
        in_specs=[pl.BlockSpec((bn, K), lambda i, j: (j0 + j, 0)),
                  pl.BlockSpec((bm, K), lambda i, j: (i, 0))]
        + [pl.BlockSpec(bs, im) for (_, bs, im) in extras] + sc_specs,
        out_specs=[pl.BlockSpec((bm // tb, bn, tb), lambda i, j: (i, j, 0))] + sc_specs,
        out_shape=[jax.ShapeDtypeStruct((M // tb, N, tb), out_dtype)] + sc_shapes,
        compiler_params=pltpu.CompilerParams(
            dimension_semantics=("arbitrary", "arbitrary") if n_sc else ("parallel", "parallel"),
            vmem_limit_bytes=_vmem_limit(pipelined, 3 * _nbytes((bn, bm), F32))),
        name=name,
    )(wt, x, *[a for (a, _, _) in extras], *side_casts)
    return outs


def _ep_scale(scale, acc):
    return (acc * scale,)


def _ep_colscale(acc, scale_row):
    return (acc * scale_row,)


def _ep_sigmoid(acc):
    return (0.5 * jnp.tanh(0.5 * acc) + 0.5,)


def _ep_identity(acc):
    return (acc,)


def _ep_relu_sq(acc):
    r = jnp.maximum(acc, 0.0)
    return (r * r,)


def _ep_rope(acc, cos, sin_signed):
    width = acc.shape[1]
    half = A_HD // 2
    lane = lax.broadcasted_iota(jnp.int32, acc.shape, 1)
    first_half = (lane % A_HD) < half
    partner = jnp.where(first_half, pltpu.roll(acc, width - half, 1), pltpu.roll(acc, half, 1))
    return (acc * cos + partner * sin_signed,)


def _ep_rope_lead(acc, cos, sin_signed):
    w = cos.shape[1]
    (lead,) = _ep_rope(acc[:, :w], cos, sin_signed)
    return (jnp.concatenate([lead, acc[:, w:]], axis=1),)


def _ep_rope_t(acc, cos_head, sin_head):
    rows = acc.shape[0]
    half = A_HD // 2
    row = lax.broadcasted_iota(jnp.int32, acc.shape, 0)
    first_half = (row % A_HD) < half
    partner = jnp.where(first_half, pltpu.roll(acc, rows - half, 0), pltpu.roll(acc, half, 0))
    cos = jnp.concatenate([cos_head] * (rows // A_HD), axis=0)
    sin_signed = jnp.concatenate([sin_head] * (rows // A_HD), axis=0)
    return (acc * cos + partner * sin_signed,)


def _ep_residual(alpha, acc, res):
    return (alpha * res + acc,)


def _ep_residual_ln(alpha, acc, pre, mu_b, rstd_b, g, b):
    return (alpha * _ln_apply(pre, mu_b[:, :1], rstd_b[:, :1], g, b) + acc,)


def _merge(ya, yb, wa, wb, gates, g0_col, g1_col, *, bm, bn):
    M, Ka = ya.shape
    Kb = yb.shape[1]
    N = wa.shape[1]
    assert M % bm == 0 and N % bn == 0 and g0_col % bn == 0 and g1_col % bn == 0
    o0, o1 = g0_col // bn, g1_col // bn

    def body(ya_ref, yb_ref, wa_ref, wb_ref, g0_ref, g1_ref, o_ref):
        a = jnp.dot(ya_ref[...], wa_ref[...], preferred_element_type=F32)
        b = jnp.dot(yb_ref[...], wb_ref[...], preferred_element_type=F32)
        o_ref[...] = (g0_ref[...].astype(F32) * a + g1_ref[...].astype(F32) * b).astype(o_ref.dtype)

    pipelined = [_nbytes((bm, Ka), BF16), _nbytes((bm, Kb), BF16), _nbytes((Ka, bn), BF16),
                 _nbytes((Kb, bn), BF16), 3 * _nbytes((bm, bn), BF16)]
    return pl.pallas_call(
        body,
        grid=(M // bm, N // bn),
        in_specs=[pl.BlockSpec((bm, Ka), lambda i, j: (i, 0)),
                  pl.BlockSpec((bm, Kb), lambda i, j: (i, 0)),
                  pl.BlockSpec((Ka, bn), lambda i, j: (0, j)),
                  pl.BlockSpec((Kb, bn), lambda i, j: (0, j)),
                  pl.BlockSpec((bm, bn), lambda i, j: (i, o0 + j)),
                  pl.BlockSpec((bm, bn), lambda i, j: (i, o1 + j))],
        out_specs=pl.BlockSpec((bm, bn), lambda i, j: (i, j)),
        out_shape=jax.ShapeDtypeStruct((M, N), BF16),
        compiler_params=pltpu.CompilerParams(
            dimension_semantics=("parallel", "parallel"),
            vmem_limit_bytes=_vmem_limit(pipelined, 2 * _nbytes((bm, bn), F32))),
        name="branch_merge",
    )(ya, yb, wa, wb, gates, gates)


def _ln_apply(x, mu, rstd, g, b):
    return (x - mu) * rstd * g + b


def _layer_norm(x, g, b, *, bm, out_dtypes, emit_stats=False):
    M, D = x.shape
    assert M % bm == 0
    n_out = len(out_dtypes)

    def body(x_ref, g_ref, b_ref, *o_refs):
        xf = x_ref[...]
        mu = jnp.mean(xf, axis=-1, keepdims=True)
        xc = xf - mu
        rstd = lax.rsqrt(jnp.mean(xc * xc, axis=-1, keepdims=True) + LN_EPS)
        y = _ln_apply(xf, mu, rstd, g_ref[...], b_ref[...])
        for o_ref in o_refs[:n_out]:
            o_ref[...] = y.astype(o_ref.dtype)
        if emit_stats:
            o_refs[n_out][...] = jnp.broadcast_to(mu, (bm, V7X_LANES))
            o_refs[n_out + 1][...] = jnp.broadcast_to(rstd, (bm, V7X_LANES))

    stat_specs = [pl.BlockSpec((bm, V7X_LANES), lambda i: (i, 0))] * (2 if emit_stats else 0)
    stat_shapes = [jax.ShapeDtypeStruct((M, V7X_LANES), F32)] * (2 if emit_stats else 0)
    pipelined = [_nbytes((bm, D), F32)] + [_nbytes((bm, D), dt) for dt in out_dtypes]
    return pl.pallas_call(
        body,
        grid=(M // bm,),
        in_specs=[pl.BlockSpec((bm, D), lambda i: (i, 0)),
                  pl.BlockSpec((1, D), lambda i: (0, 0)),
                  pl.BlockSpec((1, D), lambda i: (0, 0))],
        out_specs=[pl.BlockSpec((bm, D), lambda i: (i, 0)) for _ in out_dtypes] + stat_specs,
        out_shape=[jax.ShapeDtypeStruct((M, D), dt) for dt in out_dtypes] + stat_shapes,
        compiler_params=pltpu.CompilerParams(
            dimension_semantics=("parallel",),
            vmem_limit_bytes=_vmem_limit(pipelined, 2 * _nbytes((bm, D), F32))),
        name="layer_norm",
    )(x, g.reshape(1, D), b.reshape(1, D))


MLSTM_HEADS_PER_STEP = 4


def _mlstm(qkv, og, gates_t, b_ig, b_fg, w_mnorm, C0, n0, m0, *, L):
    B, S, _ = qkv.shape
    nC = S // L
    nCp = gates_t.shape[2]
    H, dk, dv = M_HEADS, M_DQK, M_DV
    hb = MLSTM_HEADS_PER_STEP
    ng = H // hb
    k_blk0 = M_QK_W // (hb * dk)
    v_blk0 = (2 * M_QK_W) // (hb * dv)

    def body(big_ref, bfg_ref, q_ref, k_ref, v_ref, og_ref, ig_ref, fg_ref, wm_ref, C0_ref, n0_ref, m0_ref,
             ya_ref, C_ref, n_ref, m_ref, b_s, ig_s):
        g = pl.program_id(1)
        r_i = lax.broadcasted_iota(jnp.int32, (L, L), 0)
        c_i = lax.broadcasted_iota(jnp.int32, (L, L), 1)
        tri_incl = (r_i <= c_i).astype(F32)
        for j in range(hb):
            ig_s[j] = ig_ref[0, j] + big_ref[g * hb + j]
            lf_all = jax.nn.log_sigmoid(fg_ref[0, j] + bfg_ref[g * hb + j])
            b_s[j] = jnp.dot(lf_all, tri_incl, precision=lax.Precision.HIGHEST, preferred_element_type=F32)
        C_ref[...] = C0_ref[...]
        n_ref[...] = n0_ref[...]
        m_ref[...] = m0_ref[...]
        eye = r_i == c_i
        causal = c_i <= r_i

        def col_of(row):
            return jnp.sum(jnp.where(eye, jnp.broadcast_to(row, (L, L)), 0.0), axis=1, keepdims=True)

        def head_chunk(j, c, rows):
            q = q_ref[0, rows, j * dk:(j + 1) * dk]
            k = k_ref[0, rows, j * dk:(j + 1) * dk]
            v = v_ref[0, rows, j * dv:(j + 1) * dv]
            b_row = b_s[j, pl.ds(c, 1), :]
            ig_row = ig_s[j, pl.ds(c, 1), :]
            m_prev = m_ref[0, j]
            C_prev = C_ref[0, j]
            n_prev = n_ref[0, j]

            b_col = col_of(b_row)
            dlog = b_col - jnp.broadcast_to(b_row, (L, L)) + jnp.broadcast_to(ig_row, (L, L))
            dlog = jnp.where(causal, dlog, -jnp.inf)
            inter = b_col + m_prev
            m_t = jnp.maximum(inter, jnp.max(dlog, axis=1, keepdims=True))
            qk = lax.dot_general(q, k, NT_DIMS, preferred_element_type=F32)
            s = qk * jnp.exp(dlog - m_t)
            a = jnp.exp(inter - m_t)
            qC = lax.dot_general(q, C_prev.astype(BF16), NT_DIMS, preferred_element_type=F32)
            sv = jnp.dot(s.astype(BF16), v, preferred_element_type=F32)
            num = a * qC + sv
            qn = jnp.sum(q.astype(F32) * n_prev, axis=1, keepdims=True)
            den = a * qn + jnp.sum(s, axis=1, keepdims=True)
            hid = num / jnp.maximum(jnp.abs(den), jnp.exp(-m_t))
            hn = hid * lax.rsqrt(jnp.mean(hid * hid, axis=1, keepdims=True) + RMS_EPS)
            hn = hn * wm_ref[:, j * dv:(j + 1) * dv]
            gate = og_ref[0, rows, j * dv:(j + 1) * dv].astype(F32)
            ya_ref[0, rows, j * dv:(j + 1) * dv] = (gate * hn).astype(ya_ref.dtype)

            bL = b_row[:, L - 1:L]
            g_row = bL - b_row + ig_row
            m_new = jnp.maximum(bL + m_prev, jnp.max(g_row, axis=1, keepdims=True))
            wk_col = col_of(jnp.exp(g_row - m_new))
            decay = jnp.exp(bL + m_prev - m_new)
            kw = k.astype(F32) * wk_col
            vk = lax.dot_general(v, kw.astype(BF16), TN_DIMS, preferred_element_type=F32)
            C_ref[0, j] = decay * C_prev + vk
            n_ref[0, j] = decay * n_prev + jnp.sum(kw, axis=0, keepdims=True)
            m_ref[0, j] = m_new

        def chunk(c, carry):
            rows = pl.ds(pl.multiple_of(c * L, L), L)
            for j in range(hb):
                head_chunk(j, c, rows)
            return carry

        lax.fori_loop(0, nC, chunk, 0)

    smem = pl.BlockSpec(memory_space=pltpu.SMEM)
    pipelined = [2 * _nbytes((S, hb * dk), BF16), 3 * _nbytes((S, hb * dv), BF16),
                 2 * _nbytes((hb, nCp, L), F32), 2 * _nbytes((hb, dv, dk), F32)]
    ya, C, n, m = pl.pallas_call(
        body,
        grid=(B, ng),
        in_specs=[smem, smem,
                  pl.BlockSpec((1, S, hb * dk), lambda b, g: (b, 0, g)),
                  pl.BlockSpec((1, S, hb * dk), lambda b, g: (b, 0, k_blk0 + g)),
                  pl.BlockSpec((1, S, hb * dv), lambda b, g: (b, 0, v_blk0 + g)),
                  pl.BlockSpec((1, S, hb * dv), lambda b, g: (b, 0, g)),
                  pl.BlockSpec((1, hb, nCp, L), lambda b, g: (b, g, 0, 0)),
                  pl.BlockSpec((1, hb, nCp, L), lambda b, g: (b, ng + g, 0, 0)),
                  pl.BlockSpec((1, hb * dv), lambda b, g: (0, g)),
                  pl.BlockSpec((1, hb, dv, dk), lambda b, g: (b, g, 0, 0)),
                  pl.BlockSpec((1, hb, 1, dk), lambda b, g: (b, g, 0, 0)),
                  pl.BlockSpec((1, hb, 1, 1), lambda b, g: (b, g, 0, 0))],
        out_specs=[pl.BlockSpec((1, S, hb * dv), lambda b, g: (b, 0, g)),
                   pl.BlockSpec((1, hb, dv, dk), lambda b, g: (b, g, 0, 0)),
                   pl.BlockSpec((1, hb, 1, dk), lambda b, g: (b, g, 0, 0)),
                   pl.BlockSpec((1, hb, 1, 1), lambda b, g: (b, g, 0, 0))],
        out_shape=[jax.ShapeDtypeStruct((B, S, M_V_W), BF16),
                   jax.ShapeDtypeStruct((B, H, dv, dk), F32),
                   jax.ShapeDtypeStruct((B, H, 1, dk), F32),
                   jax.ShapeDtypeStruct((B, H, 1, 1), F32)],
        scratch_shapes=[pltpu.VMEM((hb, nCp, L), F32), pltpu.VMEM((hb, nCp, L), F32)],
        compiler_params=pltpu.CompilerParams(
            dimension_semantics=("parallel", "parallel"),
            vmem_limit_bytes=_vmem_limit(pipelined, 4 << 20)),
        name="mlstm_chunks",
    )(b_ig, b_fg, qkv, qkv, qkv, og, gates_t, gates_t, w_mnorm.reshape(1, M_V_W), C0, n0, m0)
    return ya, C, n, m


def _mlstm_prompt(qt, k, vt, ogt, gates_p, b_ig, b_fg, wm_b, *, B, S):
    TB = TOKEN_BLOCK
    assert TB == 2 * CHUNK and S % TB == 0
    nP = S // TB
    nPp = gates_p.shape[2]
    H, dk, dv = M_HEADS, M_DQK, M_DV
    hb = MLSTM_HEADS_PER_STEP
    ng = H // hb
    dva = dv + V7X_SUBLANES
    NEG = -jnp.inf

    def body(big_ref, bfg_ref, q_ref, k_ref, v_ref, og_ref, ig_ref, fg_ref, wm_ref,
             ya_ref, C_ref, n_ref, m_ref, caug_s, row_s, u_s):
        g = pl.program_id(1)
        s_i = lax.broadcasted_iota(jnp.int32, (TB, TB), 0)
        t_i = lax.broadcasted_iota(jnp.int32, (TB, TB), 1)
        causal = s_i <= t_i
        same_chunk = (s_i // CHUNK) == (t_i // CHUNK)
        intra = jnp.logical_and(causal, same_chunk)
        eye = s_i == t_i
        tri_chunk = intra.astype(F32)
        lane_p = lax.broadcasted_iota(jnp.int32, (nPp, TB), 1)
        first_p = lane_p < CHUNK
        first_1 = lax.broadcasted_iota(jnp.int32, (1, TB), 1) < CHUNK
        for j in range(hb):
            ig = ig_ref[0, j] + big_ref[g * hb + j]
            lf = jax.nn.log_sigmoid(fg_ref[0, j] + bfg_ref[g * hb + j])
            bc = jnp.dot(lf, tri_chunk, precision=lax.Precision.HIGHEST, preferred_element_type=F32)
            b_a = jnp.broadcast_to(bc[:, CHUNK - 1:CHUNK], (nPp, TB))
            b_b = jnp.broadcast_to(bc[:, TB - 1:TB], (nPp, TB))
            bp = bc + jnp.where(first_p, 0.0, b_a)
            gl = jnp.where(first_p, b_a, b_b) - bc + ig
            gmax_a = jnp.max(jnp.where(first_p, gl, NEG), axis=1, keepdims=True)
            gmax_b = jnp.max(jnp.where(first_p, NEG, gl), axis=1, keepdims=True)
            row_s[j, 0] = bc
            row_s[j, 1] = bp
            row_s[j, 2] = ig - bp
            row_s[j, 6] = b_a + b_b
            row_s[j, 3] = b_a
            row_s[j, 4] = b_b
            row_s[j, 5] = jnp.broadcast_to(gmax_a, (nPp, TB))
            row_s[j, 7] = jnp.broadcast_to(gmax_b, (nPp, TB))
            caug_s[j] = jnp.zeros((dva, dk), F32)

        def scan(p, m_rows):
            nxt = []
            for j in range(hb):
                m0 = m_rows[j]
                r = pl.ds(p, 1)
                m1 = jnp.maximum(row_s[j, 3, r, :] + m0, row_s[j, 5, r, :])
                m2 = jnp.maximum(row_s[j, 4, r, :] + m1, row_s[j, 7, r, :])
                row_s[j, 3, r, :] = jnp.where(first_1, m0, m1)
                row_s[j, 4, r, :] = m0
                row_s[j, 5, r, :] = m2
                u_row = row_s[j, 2, r, :]
                u_col = jnp.sum(jnp.where(eye, jnp.broadcast_to(u_row, (TB, TB)), 0.0), axis=1, keepdims=True)
                u_s[j, p] = jnp.broadcast_to(u_col, (TB, TB))
                nxt.append(m2)
            return tuple(nxt)

        m_fin = lax.fori_loop(0, nP, scan, tuple(jnp.zeros((1, TB), F32) for _ in range(hb)))

        def head_block(j, p, rows):
            kp = k_ref[0, rows, j * dk:(j + 1) * dk]
            q_t = q_ref[p, j * dk:(j + 1) * dk, :]
            v_t = v_ref[p, j * dv:(j + 1) * dv, :]
            r = pl.ds(p, 1)
            bc, bp, u_row = row_s[j, 0, r, :], row_s[j, 1, r, :], row_s[j, 2, r, :]
            m_chunk, m_start, m_end, b_tot = row_s[j, 3, r, :], row_s[j, 4, r, :], row_s[j, 5, r, :], row_s[j, 6, r, :]
            d_t = bp + u_s[j, p]
            m_t = jnp.maximum(bc + m_chunk, jnp.max(jnp.where(intra, d_t, NEG), axis=0, keepdims=True))
            e_t = jnp.exp(jnp.where(causal, d_t, NEG) - m_t)
            p_t = jnp.dot(kp, q_t, preferred_element_type=F32) * e_t
            a = jnp.exp(bp + m_start - m_t)
            caug = caug_s[j]
            cq = jnp.dot(caug.astype(BF16), q_t, preferred_element_type=F32)
            num = a * cq[:dv] + jnp.dot(v_t, p_t.astype(BF16), preferred_element_type=F32)
            den = a * cq[dv:dv + 1] + jnp.sum(p_t, axis=0, keepdims=True)
            hid = num / jnp.maximum(jnp.abs(den), jnp.exp(-m_t))
            hn = hid * lax.rsqrt(jnp.mean(hid * hid, axis=0, keepdims=True) + RMS_EPS)
            hn = hn * wm_ref[j * dv:(j + 1) * dv, :] * og_ref[p, j * dv:(j + 1) * dv, :].astype(F32)
            for c0 in range(0, dv, TB):
                ya_ref[0, rows, j * dv + c0:j * dv + c0 + TB] = hn[c0:c0 + TB].T.astype(ya_ref.dtype)

            wk = jnp.exp(b_tot + u_row - m_end)
            decay = jnp.exp(b_tot + m_start - m_end)
            vw = (v_t.astype(F32) * wk).astype(BF16)
            vw = jnp.concatenate([vw, jnp.broadcast_to(wk, (V7X_SUBLANES, TB)).astype(BF16)], axis=0)
            caug_s[j] = decay * caug + jnp.dot(vw, kp, preferred_element_type=F32)

        def block(p, carry):
            rows = pl.ds(pl.multiple_of(p * TB, TB), TB)
            for j in range(hb):
                head_block(j, p, rows)
            return carry

        lax.fori_loop(0, nP, block, 0)
        for j in range(hb):
            C_ref[0, j] = caug_s[j, :dv, :]
            n_ref[0, j] = caug_s[j, dv:dv + 1, :]
            m_ref[0, j] = m_fin[j][:, :1]

    smem = pl.BlockSpec(memory_space=pltpu.SMEM)
    pipelined = [_nbytes((nP, hb * dk, TB), BF16), _nbytes((S, hb * dk), BF16), 3 * _nbytes((S, hb * dv), BF16),
                 2 * _nbytes((hb, nPp, TB), F32), _nbytes((hb * dv, TB), F32), _nbytes((hb, dv, dk), F32)]
    scratch = [pltpu.VMEM((hb, dva, dk), F32), pltpu.VMEM((hb, 8, nPp, TB), F32), pltpu.VMEM((hb, nP, TB, TB), F32)]
    resident = _nbytes((hb, dva, dk), F32) + _nbytes((hb, 8, nPp, TB), F32) + _nbytes((hb, nP, TB, TB), F32)
    ya, C, n, m = pl.pallas_call(
        body,
        grid=(B, ng),
        in_specs=[smem, smem,
                  pl.BlockSpec((nP, hb * dk, TB), lambda b, g: (b, g, 0)),
                  pl.BlockSpec((1, S, hb * dk), lambda b, g: (b, 0, g)),
                  pl.BlockSpec((nP, hb * dv, TB), lambda b, g: (b, g, 0)),
                  pl.BlockSpec((nP, hb * dv, TB), lambda b, g: (b, g, 0)),
                  pl.BlockSpec((1, hb, nPp, TB), lambda b, g: (b, g, 0, 0)),
                  pl.BlockSpec((1, hb, nPp, TB), lambda b, g: (b, ng + g, 0, 0)),
                  pl.BlockSpec((hb * dv, TB), lambda b, g: (g, 0))],
        out_specs=[pl.BlockSpec((1, S, hb * dv), lambda b, g: (b, 0, g)),
                   pl.BlockSpec((1, hb, dv, dk), lambda b, g: (b, g, 0, 0)),
                   pl.BlockSpec((1, hb, 1, dk), lambda b, g: (b, g, 0, 0)),
                   pl.BlockSpec((1, hb, 1, 1), lambda b, g: (b, g, 0, 0))],
        out_shape=[jax.ShapeDtypeStruct((B, S, M_V_W), BF16),
                   jax.ShapeDtypeStruct((B, H, dv, dk), F32),
                   jax.ShapeDtypeStruct((B, H, 1, dk), F32),
                   jax.ShapeDtypeStruct((B, H, 1, 1), F32)],
        scratch_shapes=scratch,
        compiler_params=pltpu.CompilerParams(
            dimension_semantics=("parallel", "parallel"),
            vmem_limit_bytes=_vmem_limit(pipelined, resident + (4 << 20))),
        name="mlstm_prompt_blocks",
    )(b_ig, b_fg, qt, k, vt, ogt, gates_p, gates_p, wm_b)
    return ya, C, n, m


def _swa_prompt(qt, kv, vt, sink, *, B, S):
    TB = 2 * CHUNK
    assert TB == V7X_LANES and S % TB == 0 and WIN_CHUNKS == 2
    nblk = S // TB
    G, KVH = A_GROUP, A_KV_HEADS

    def body(sink_ref, q_ref, kp_ref, kc_ref, vp_ref, vc_ref, o_ref):
        p_id = pl.program_id(1)
        key_chunk = lax.broadcasted_iota(jnp.int32, (2 * TB, TB), 0) // CHUNK
        q_half = lax.broadcasted_iota(jnp.int32, (2 * TB, TB), 1) // CHUNK
        valid = jnp.logical_and(key_chunk >= q_half, key_chunk <= q_half + WIN_CHUNKS)
        valid = jnp.logical_and(valid, jnp.logical_or(key_chunk >= WIN_CHUNKS, p_id > 0))
        bias = jnp.where(valid, 0.0, -jnp.inf).astype(F32)
        bias = jnp.concatenate([bias] * G, axis=1)
        zeros = jnp.zeros((A_HD, TB), q_ref.dtype)
        for kvh in range(KVH):
            pr = kvh // 2
            lanes = slice(pr * V7X_LANES, (pr + 1) * V7X_LANES)
            kc = jnp.concatenate([kp_ref[:, lanes], kc_ref[:, lanes]], axis=0).astype(BF16)
            rows = slice(kvh * A_HD, (kvh + 1) * A_HD)
            vt_w = jnp.concatenate([vp_ref[0, rows, :], vc_ref[0, rows, :]], axis=1)
            pieces, sinks = [], []
            for g in range(G):
                head = kvh * G + g
                qg = q_ref[0, head * A_HD:(head + 1) * A_HD, :]
                pieces.append(jnp.concatenate([zeros, qg] if kvh % 2 else [qg, zeros], axis=0))
                sinks.append(jnp.full((1, TB), sink_ref[head] * LOG2_E, F32))
            qz = jnp.concatenate(pieces, axis=1)
            sk = jnp.concatenate(sinks, axis=1)
            s_t = jnp.dot(kc, qz, preferred_element_type=F32) + bias
            mx = jnp.maximum(jnp.max(s_t, axis=0, keepdims=True), sk)
            p_t = jnp.exp2(s_t - mx)
            den = jnp.sum(p_t, axis=0, keepdims=True) + jnp.exp2(sk - mx)
            o_t = jnp.dot(vt_w, p_t.astype(BF16), preferred_element_type=F32) / den
            for j in range(G // 2):
                two = jnp.concatenate([o_t[:, (2 * j) * TB:(2 * j + 1) * TB],
                                       o_t[:, (2 * j + 1) * TB:(2 * j + 2) * TB]], axis=0)
                col0 = (kvh * G + 2 * j) * A_HD
                o_ref[:, col0:col0 + V7X_LANES] = two.T.astype(o_ref.dtype)

    prev = lambda b, p: b * nblk + jnp.maximum(p - 1, 0)
    cur = lambda b, p: b * nblk + p
    pipelined = [2 * _nbytes((A_Q_W, TB), BF16), 2 * _nbytes((TB, A_KV_W), F32), 2 * _nbytes((A_KV_W, TB), BF16)]
    return pl.pallas_call(
        body,
        grid=(B, nblk),
        in_specs=[pl.BlockSpec(memory_space=pltpu.SMEM),
                  pl.BlockSpec((1, A_Q_W, TB), lambda b, p: (cur(b, p), 0, 0)),
                  pl.BlockSpec((TB, A_KV_W), lambda b, p: (prev(b, p), 0)),
                  pl.BlockSpec((TB, A_KV_W), lambda b, p: (cur(b, p), 0)),
                  pl.BlockSpec((1, A_KV_W, TB), lambda b, p: (prev(b, p), 0, 0)),
                  pl.BlockSpec((1, A_KV_W, TB), lambda b, p: (cur(b, p), 0, 0))],
        out_specs=pl.BlockSpec((TB, A_Q_W), lambda b, p: (cur(b, p), 0)),
        out_shape=jax.ShapeDtypeStruct((B * S, A_Q_W), BF16),
        compiler_params=pltpu.CompilerParams(
            dimension_semantics=("parallel", "arbitrary"),
            vmem_limit_bytes=_vmem_limit(pipelined, 16 << 20)),
        name="swa_prompt_attention",
    )(sink, qt, kv, kv, vt, vt)


def _swa_sample(q, kv, sink):
    B, T, _ = q.shape
    Skv = kv.shape[1]
    G, KVH, LN = A_GROUP, A_KV_HEADS, V7X_LANES
    assert 2 * A_HD == LN and T % 16 == 0

    def body(sink_ref, q_ref, kv_ref, o_ref):
        kvw = kv_ref[0]
        lo_q = lax.broadcasted_iota(jnp.int32, (T, LN), 1) < A_HD
        lo_k = lax.broadcasted_iota(jnp.int32, (Skv, LN), 1) < A_HD
        zero = jnp.zeros((T, LN), q_ref.dtype)
        for kvh in range(KVH):
            pair, odd = kvh // 2, kvh % 2
            kx = kvw[:, pair * LN:(pair + 1) * LN]
            vx = kvw[:, A_KV_W + pair * LN:A_KV_W + (pair + 1) * LN]
            kr = pltpu.roll(kx, A_HD, 1)
            vr = pltpu.roll(vx, A_HD, 1)
            k2 = (jnp.where(lo_k, kr, kx) if odd else jnp.where(lo_k, kx, kr)).astype(BF16)
            v2 = (jnp.where(lo_k, vr, vx) if odd else jnp.where(lo_k, vx, vr)).astype(BF16)
            pieces, sinks = [], []
            for g in range(G):
                head = kvh * G + g
                q2 = q_ref[0, :, (head // 2) * LN:(head // 2 + 1) * LN]
                pieces.append(jnp.where(lo_q, zero, q2) if head % 2 else jnp.where(lo_q, q2, zero))
                sinks.append(jnp.full((T, 1), sink_ref[head], F32))
            qs = jnp.concatenate(pieces, axis=0)
            sk = jnp.concatenate(sinks, axis=0)
            s = lax.dot_general(qs, k2, NT_DIMS, preferred_element_type=F32)
            mx = jnp.maximum(jnp.max(s, axis=1, keepdims=True), sk)
            p = jnp.exp(s - mx)
            den = jnp.sum(p, axis=1, keepdims=True) + jnp.exp(sk - mx)
            o = jnp.dot(p.astype(BF16), v2, preferred_element_type=F32) / den
            for j in range(G // 2):
                pair_o = jnp.where(lo_q, o[(2 * j) * T:(2 * j + 1) * T], o[(2 * j + 1) * T:(2 * j + 2) * T])
                col0 = (kvh * G + 2 * j) * A_HD
                o_ref[0, :, col0:col0 + LN] = pair_o.astype(o_ref.dtype)

    pipelined = [2 * _nbytes((T, A_Q_W), BF16), _nbytes((Skv, 2 * A_KV_W), F32)]
    return pl.pallas_call(
        body,
        grid=(B,),
        in_specs=[pl.BlockSpec(memory_space=pltpu.SMEM),
                  pl.BlockSpec((1, T, A_Q_W), lambda b: (b, 0, 0)),
                  pl.BlockSpec((1, Skv, 2 * A_KV_W), lambda b: (b, 0, 0))],
        out_specs=pl.BlockSpec((1, T, A_Q_W), lambda b: (b, 0, 0)),
        out_shape=jax.ShapeDtypeStruct((B, T, A_Q_W), BF16),
        compiler_params=pltpu.CompilerParams(
            dimension_semantics=("parallel",),
            vmem_limit_bytes=_vmem_limit(pipelined, 8 << 20)),
        name="swa_sample_attention",
    )(sink, q, kv)


def _rope_tables(pos, n_rot_heads, n_plain_cols, scale):
    half = A_HD // 2
    inv = ROPE_THETA ** (-jnp.arange(half, dtype=F32) / half)
    ang = pos.astype(F32)[:, None] * inv[None, :]
    cos, sin = jnp.cos(ang) * scale, jnp.sin(ang) * scale
    cos_h = jnp.concatenate([cos, cos], -1)
    sin_h = jnp.concatenate([-sin, sin], -1)
    n = pos.shape[0]
    cos_t = jnp.concatenate([jnp.tile(cos_h, (1, n_rot_heads)), jnp.ones((n, n_plain_cols), F32)], -1)
    sin_t = jnp.concatenate([jnp.tile(sin_h, (1, n_rot_heads)), jnp.zeros((n, n_plain_cols), F32)], -1)
    return cos_t, sin_t


W_IN_BLOCK = 512


def _in_proj_layout(d_model):
    widths = [("mqkv", 2 * M_QK_W + M_V_W), ("sig", M_V_W + N_BRANCH * d_model), ("aq", A_Q_W),
              ("kv", 2 * A_KV_W), ("gate", W_IN_BLOCK)]
    off, col = {}, 0
    for name, w in widths:
        off[name] = col
        col += w
    off["end"] = col
    return off


def _regroup_w_in_t(w_in_t):
    D = w_in_t.shape[1]
    off = _in_proj_layout(D)
    bw = W_IN_BLOCK
    n_gate = 2 * M_HEADS
    src_gate = 2 * M_QK_W + 2 * M_V_W
    src_aq = src_gate + n_gate
    src_gp = src_aq + A_Q_W + 2 * A_KV_W
    assert src_gate % bw == 0 and off["sig"] % bw == 0 and (N_BRANCH * D) % bw == 0 and n_gate <= bw
    assert src_aq % V7X_SUBLANES == 0
    j_gp = (off["sig"] + M_V_W) // bw
    j_aq = off["aq"] // bw
    j_gate = off["gate"] // bw

    sl = V7X_SUBLANES

    def src_row(j):
        tile = jnp.where(j < j_gp, j * (bw // sl),
                         jnp.where(j < j_aq, src_gp // sl + (j - j_gp) * (bw // sl),
                                   jnp.where(j < j_gate, src_aq // sl + (j - j_aq) * (bw // sl), src_gate // sl)))
        return tile * sl

    def body(a_ref, o_ref):
        j = pl.program_id(0)
        a = a_ref[...]
        row = lax.broadcasted_iota(jnp.int32, a.shape, 0)
        keep = jnp.logical_or(j < j_gate, row < n_gate)
        o_ref[...] = jnp.where(keep, a, 0.0).astype(o_ref.dtype)

    pipelined = [_nbytes((bw, D), F32), _nbytes((bw, D), BF16)]
    return pl.pallas_call(
        body,
        grid=(off["end"] // bw,),
        in_specs=[pl.BlockSpec((pl.Element(bw), pl.Element(D)), lambda j: (src_row(j), 0))],
        out_specs=pl.BlockSpec((bw, D), lambda j: (j, 0)),
        out_shape=jax.ShapeDtypeStruct((off["end"], D), BF16),
        compiler_params=pltpu.CompilerParams(
            dimension_semantics=("parallel",),
            vmem_limit_bytes=_vmem_limit(pipelined, 2 * _nbytes((bw, D), F32))),
        name="regroup_w_in",
    )(w_in_t)


DENSE_WEIGHTS = ("br_a", "br_b", "out", "up", "down")


def _layer(x, pos, W, b_ig, b_fg, w_mnorm, sink, ln1_g, ln1_b, ln2_g, ln2_b, alpha, state, cache, *, bm):
    B, S, D = x.shape
    M = B * S
    L = min(CHUNK, S)
    xf = x.reshape(M, D)
    bn_big = min(1024, D)
    off = _in_proj_layout(D)
    w_in_t = W["w_in_t"]
    Wb = {name: W[name] for name in DENSE_WEIGHTS if W[name].dtype == BF16}

    def pos_extra(table, rows, width):
        if rows <= S:
            per = S // rows
            return (table, (rows, width), lambda i, j, k: (i % per, j))
        return (jnp.tile(table, (rows // S, 1)), (rows, width), lambda i, j, k: (0, j))

    q_scale_m = M_DQK ** -0.5
    wkv = 2 * A_KV_W
    bm0 = min(256, M)
    n_first = wkv + W_IN_BLOCK
    cos_k, sin_k = _rope_tables(pos, A_KV_HEADS, 0, 1.0)
    kvg, xb = _matmul(xf, w_in_t, w_nk=True, bm=bm0, bn=n_first, bk=D, epilogue=_ep_rope_lead, out_dtypes=[F32],
                      extras=[pos_extra(cos_k, bm0, A_KV_W), pos_extra(sin_k, bm0, A_KV_W)],
                      w_col0=off["kv"], n_cols=n_first, emit_x_bf16=True, name="in_proj_attn_kv_if_gates")
    gates = kvg[:, wkv:wkv + 2 * M_HEADS]

    def gate_rows(lanes):
        n = S // lanes
        t = gates.reshape(B, n, lanes, 2 * M_HEADS).transpose(0, 3, 1, 2)
        return jnp.pad(t, ((0, 0), (0, 0), (0, max(n, V7X_SUBLANES) - n), (0, 0)))

    def side(name):
        return [] if name in Wb else [W[name]]

    def keep(name, casts):
        if casts:
            Wb[name] = casts[0]

    o_mo, o_gp = off["sig"], off["sig"] + M_V_W
    if state is None:
        mk, *c = _matmul(xb, w_in_t, w_nk=True, bm=bm, bn=bn_big, bk=D, epilogue=_ep_identity, out_dtypes=[BF16],
                         w_col0=M_QK_W, n_cols=M_QK_W, side_casts=side("br_a"), name="in_proj_mlstm_k")
        keep("br_a", c)
        bn_t = min(1024, M_QK_W)
        mq_t, *c = _matmul_nt(w_in_t, xb, bm=bm, bn=bn_t, epilogue=functools.partial(_ep_scale, q_scale_m),
                              out_dtype=BF16, w_row0=0, n_rows=M_QK_W, side_casts=side("br_b"),
                              name="in_proj_mlstm_q_t")
        keep("br_b", c)
        mv_t, *c = _matmul_nt(w_in_t, xb, bm=bm, bn=bn_t, epilogue=_ep_identity, out_dtype=BF16,
                              w_row0=2 * M_QK_W, n_rows=M_V_W, side_casts=side("out"),
                              name="in_proj_mlstm_v_t")
        keep("out", c)
        (og_t,) = _matmul_nt(w_in_t, xb, bm=bm, bn=bn_t, epilogue=_ep_sigmoid, out_dtype=BF16,
                             w_row0=o_mo, n_rows=M_V_W, name="in_proj_mlstm_o_t")
        br_gates, *c = _matmul(xb, w_in_t, w_nk=True, bm=bm, bn=bn_big, bk=D, epilogue=_ep_sigmoid,
                               out_dtypes=[BF16], w_col0=o_gp, n_cols=N_BRANCH * D, side_casts=side("up"),
                               name="in_proj_branch_gates")
        keep("up", c)
        g_cols = (0, D)
        wm_b = jnp.broadcast_to(w_mnorm[:, None], (M_V_W, TOKEN_BLOCK))
        ya, C, n_st, m = _mlstm_prompt(mq_t, mk.reshape(B, S, M_QK_W), mv_t, og_t, gate_rows(TOKEN_BLOCK),
                                       b_ig, b_fg, wm_b, B=B, S=S)
    else:
        n_qkv = 2 * M_QK_W + M_V_W
        colscale = jnp.concatenate([jnp.full((1, M_QK_W), q_scale_m, F32), jnp.ones((1, M_QK_W + M_V_W), F32)], 1)
        (qkv,) = _matmul(xb, w_in_t, w_nk=True, bm=bm, bn=bn_big, bk=D, epilogue=_ep_colscale, out_dtypes=[BF16],
                         extras=[(colscale, (1, bn_big), lambda i, j, k: (0, j))],
                         w_col0=off["mqkv"], n_cols=n_qkv, name="in_proj_mlstm_qkv")
        (br_gates,) = _matmul(xb, w_in_t, w_nk=True, bm=bm, bn=bn_big, bk=D, epilogue=_ep_sigmoid,
                              out_dtypes=[BF16], w_col0=o_mo, n_cols=M_V_W + N_BRANCH * D,
                              name="in_proj_sigmoid_gates")
        g_cols = (M_V_W, M_V_W + D)
        C0 = state[0].astype(F32)
        n0 = state[1].astype(F32).reshape(B, M_HEADS, 1, M_DQK)
        m0 = state[2].astype(F32).reshape(B, M_HEADS, 1, 1)
        ya, C, n_st, m = _mlstm(qkv.reshape(B, S, -1), br_gates.reshape(B, S, -1), gate_rows(L), b_ig, b_fg,
                                w_mnorm, C0, n0, m0, L=L)

    kv3 = kvg.reshape(B, S, n_first)
    k_new = kv3[:, :, :A_KV_W].reshape(B, S, A_KV_HEADS, A_HD)
    v_new = kv3[:, :, A_KV_W:wkv].reshape(B, S, A_KV_HEADS, A_HD)
    q_scale = A_HD ** -0.5
    wq = 512
    if cache is None:
        cos_q, sin_q = _rope_tables(pos, 1, 0, q_scale * LOG2_E)
        if bm <= S:
            tab = lambda t: (t.T, (A_HD, bm), lambda i, j: (0, i % (S // bm)))
        else:
            tab = lambda t: (jnp.tile(t.T, (1, bm // S)), (A_HD, bm), lambda i, j: (0, 0))
        (qt,) = _matmul_nt(w_in_t, xb, bm=bm, bn=min(1024, A_Q_W), epilogue=_ep_rope_t, out_dtype=BF16,
                           extras=[tab(cos_q), tab(sin_q)], w_row0=off["aq"], n_rows=A_Q_W,
                           name="in_proj_attn_q_t")
        (vt,) = _matmul_nt(w_in_t, xb, bm=bm, bn=A_KV_W, epilogue=_ep_identity, out_dtype=BF16,
                           w_row0=off["kv"] + A_KV_W, n_rows=A_KV_W, name="in_proj_attn_v_t")
        yb = _swa_prompt(qt, kvg, vt, sink, B=B, S=S)
    else:
        cos_q, sin_q = _rope_tables(pos, A_HEADS, 0, q_scale)
        (aq,) = _matmul(xb, w_in_t, w_nk=True, bm=bm, bn=wq, bk=D, epilogue=_ep_rope, out_dtypes=[BF16],
                        extras=[pos_extra(cos_q, bm, wq), pos_extra(sin_q, bm, wq)],
                        w_col0=off["aq"], n_cols=A_Q_W, name="in_proj_attn_q")
        ck = cache[0].astype(F32).reshape(B, WINDOW, A_KV_W)
        cv = cache[1].astype(F32).reshape(B, WINDOW, A_KV_W)
        kv_all = jnp.concatenate([jnp.concatenate([ck, cv], -1), kv3[:, :, :wkv]], axis=1)
        yb = _swa_sample(aq.reshape(B, S, A_Q_W), kv_all, sink).reshape(M, A_Q_W)

    for name in DENSE_WEIGHTS[:3]:
        Wb.setdefault(name, W[name].astype(BF16))
    merged = _merge(ya.reshape(M, M_V_W), yb, Wb["br_a"], Wb["br_b"], br_gates,
                    g_cols[0], g_cols[1], bm=bm, bn=bn_big)
    res1 = (xf, (bm, bn_big), lambda i, j, k: (i, j))
    (h_pre,) = _matmul(merged, Wb["out"], bm=bm, bn=bn_big, bk=D, epilogue=functools.partial(_ep_residual, alpha),
                       out_dtypes=[F32], extras=[res1], name="out_proj_residual")
    bm_ln = min(256, M)
    h16, mu_b, rstd_b = _layer_norm(h_pre, ln1_g, ln1_b, bm=bm_ln, out_dtypes=[BF16], emit_stats=True)
    Wb.setdefault("up", W["up"].astype(BF16))
    act, *c = _matmul(h16, Wb["up"], bm=bm, bn=bn_big, bk=D, epilogue=_ep_relu_sq, out_dtypes=[BF16],
                      side_casts=side("down"), name="mlp_up_relu_sq")
    keep("down", c)
    stat = lambda a: (a, (bm, V7X_LANES), lambda i, j, k: (i, 0))
    row = lambda v: (v.reshape(1, D), (1, bn_big), lambda i, j, k: (0, j))
    res2 = [(h_pre, (bm, bn_big), lambda i, j, k: (i, j)), stat(mu_b), stat(rstd_b), row(ln1_g), row(ln1_b)]
    (y_pre,) = _matmul(act, Wb["down"], bm=bm, bn=bn_big, bk=min(4096, act.shape[1]),
                       epilogue=functools.partial(_ep_residual_ln, alpha), out_dtypes=[F32], extras=res2,
                       name="mlp_down_residual")
    (y,) = _layer_norm(y_pre, ln2_g, ln2_b, bm=bm_ln, out_dtypes=[F32])
    return (y.reshape(B, S, D), k_new, v_new, C, n_st.reshape(B, M_HEADS, M_DQK), m.reshape(B, M_HEADS), Wb)


def kernel(x_prompt, x_sample, cache_swa_k, cache_swa_v, state_mlstm_C, state_mlstm_n, state_mlstm_m,
           w_in, b_igate, b_fgate, w_mnorm, attn_sink, w_branch_a, w_branch_b, w_out,
           ln1_g, ln1_b, w_up, w_down, ln2_g, ln2_b):
    depth = w_in.shape[0]
    alpha = (2.0 * depth) ** 0.25
    S, T = x_prompt.shape[1], x_sample.shape[1]
    pos_p = jnp.arange(S, dtype=jnp.int32)
    pos_s = PAST_LEN + jnp.arange(T, dtype=jnp.int32)
    xp, xs = x_prompt, x_sample
    outs_p, outs_s = [], []
    for l in range(depth):
        W = dict(w_in_t=_regroup_w_in_t(w_in[l].T), br_a=w_branch_a[l], br_b=w_branch_b[l], out=w_out[l],
                 up=w_up[l], down=w_down[l])
        shared = (b_igate[l], b_fgate[l], w_mnorm[l], attn_sink[l], ln1_g[l], ln1_b[l], ln2_g[l], ln2_b[l], alpha)
        xp, *rest_p, Wb = _layer(xp, pos_p, W, *shared, None, None, bm=min(1024, xp.shape[0] * S))
        outs_p.append(rest_p)
        xs, *rest_s, _ = _layer(xs, pos_s, dict(W, **Wb), *shared,
                                (state_mlstm_C[l], state_mlstm_n[l], state_mlstm_m[l]),
                                (cache_swa_k[l], cache_swa_v[l]), bm=xs.shape[0] * T)
        outs_s.append(rest_s)
    stack = lambda outs, i: jnp.stack([o[i] for o in outs])
    pk = jnp.stack([o[0][:, -WINDOW:] for o in outs_p])
    pv = jnp.stack([o[1][:, -WINDOW:] for o in outs_p])
    return (xp, xs, pk, pv, stack(outs_p, 2), stack(outs_p, 3), stack(outs_p, 4),
            stack(outs_s, 0), stack(outs_s, 1), stack(outs_s, 2), stack(outs_s, 3), stack(outs_s, 4))
```

```python
import functools

import jax
import jax.numpy as jnp
from jax import lax
from jax.experimental import pallas as pl
from jax.experimental.pallas import tpu as pltpu

CHUNK = 64
M_HEADS = 8
M_DQK = 128
M_DV = 256
A_HEADS = 32
A_KV_HEADS = 4
A_GROUP = A_HEADS // A_KV_HEADS
A_HD = 64
WINDOW = 128
WIN_CHUNKS = WINDOW // CHUNK
ROPE_THETA = 10000.0
PAST_LEN = 1024
N_BRANCH = 2
LN_EPS = 1e-5
RMS_EPS = 1e-6
LOG2_E = 1.4426950408889634

M_QK_W = M_HEADS * M_DQK
M_V_W = M_HEADS * M_DV
A_Q_W = A_HEADS * A_HD
A_KV_W = A_KV_HEADS * A_HD

V7X_VMEM_BYTES = 64 * 1024 * 1024
V7X_LANES = 128
V7X_SUBLANES = 8
VMEM_REQUEST_CAP = (V7X_VMEM_BYTES * 15) // 16

F32 = jnp.float32
BF16 = jnp.bfloat16
NT_DIMS = (((1,), (1,)), ((), ()))
TN_DIMS = (((0,), (0,)), ((), ()))


def _nbytes(shape, dtype):
    n = 1
    for s in shape:
        n *= s
    return n * jnp.dtype(dtype).itemsize


SPILL_ALLOWANCE = 6 << 20


def _vmem_limit(pipelined, resident=0):
    return int(min(VMEM_REQUEST_CAP, 2 * sum(pipelined) + resident + SPILL_ALLOWANCE))


def _side_cast_specs(side_casts, n_steps, step_of):
    specs, shapes, nbytes = [], [], []
    for a in side_casts:
        rows, cols = a.shape
        assert rows % n_steps == 0 and (rows // n_steps) % (2 * V7X_SUBLANES) == 0, (a.shape, n_steps)
        slab = rows // n_steps
        specs.append(pl.BlockSpec((slab, cols), lambda *ids: (step_of(*ids), 0)))
        shapes.append(jax.ShapeDtypeStruct(a.shape, BF16))
        nbytes.append(_nbytes((slab, cols), F32) + _nbytes((slab, cols), BF16))
    return specs, shapes, nbytes


def _matmul(x, w, *, bm, bn, bk, epilogue, out_dtypes, extras=(), w_nk=False, w_col0=0, n_cols=None,
            emit_x_bf16=False, side_casts=(), name):
    M, K = x.shape
    N = (w.shape[0] if w_nk else w.shape[1]) if n_cols is None else n_cols
    assert M % bm == 0 and N % bn == 0 and K % bk == 0, (M, N, K, bm, bn, bk)
    w_by_element = w_col0 % bn != 0
    assert not w_by_element or (w_nk and w_col0 % (2 * V7X_SUBLANES) == 0), (w_col0, bn)
    nm, nn, nk = M // bm, N // bn, K // bk
    assert not emit_x_bf16 or nk == 1
    j0 = w_col0 // bn
    n_ex, n_out, n_sc = len(extras), len(out_dtypes), len(side_casts)
    n_xo = 1 if emit_x_bf16 else 0

    def body(*refs):
        x_ref, w_ref = refs[0], refs[1]
        ex_refs = refs[2:2 + n_ex]
        sc_in = refs[2 + n_ex:2 + n_ex + n_sc]
        o0 = 2 + n_ex + n_sc
        out_refs = refs[o0:o0 + n_out]
        sc_out = refs[o0 + n_out + n_xo:o0 + n_out + n_xo + n_sc]
        for src, dst in zip(sc_in, sc_out):
            dst[...] = src[...].astype(dst.dtype)

        def product():
            xv = x_ref[...]
            if emit_x_bf16:
                xv = xv.astype(BF16)
                refs[o0 + n_out][...] = xv
            if w_nk:
                return lax.dot_general(xv, w_ref[...], NT_DIMS, preferred_element_type=F32)
            return jnp.dot(xv, w_ref[...], preferred_element_type=F32)

        def finish(acc):
            outs = epilogue(acc, *[r[...] for r in ex_refs])
            for o_ref, o in zip(out_refs, outs):
                o_ref[...] = o.astype(o_ref.dtype)

        if nk == 1:
            finish(product())
        else:
            acc_ref = out_refs[0]
            k = pl.program_id(2)

            @pl.when(k == 0)
            def _():
                acc_ref[...] = product()

            @pl.when(jnp.logical_and(k > 0, k < nk - 1))
            def _():
                acc_ref[...] = acc_ref[...] + product()

            @pl.when(k == nk - 1)
            def _():
                finish(acc_ref[...] + product())

    if w_by_element:
        sl = 2 * V7X_SUBLANES
        w_spec = pl.BlockSpec((pl.Element(bn), pl.Element(bk)),
                              lambda i, j, k: ((w_col0 // sl + j * (bn // sl)) * sl, k * bk))
    elif w_nk:
        w_spec = pl.BlockSpec((bn, bk), lambda i, j, k: (j0 + j, k))
    else:
        w_spec = pl.BlockSpec((bk, bn), lambda i, j, k: (k, j0 + j))
    sc_specs, sc_shapes, sc_bytes = _side_cast_specs(side_casts, nm * nn * nk, lambda i, j, k: (i * nn + j) * nk + k)
    in_specs = [pl.BlockSpec((bm, bk), lambda i, j, k: (i, k)), w_spec]
    in_specs += [pl.BlockSpec(bs, im) for (_, bs, im) in extras] + sc_specs
    out_specs = [pl.BlockSpec((bm, bn), lambda i, j, k: (i, j)) for _ in out_dtypes]
    out_shape = [jax.ShapeDtypeStruct((M, N), dt) for dt in out_dtypes]
    if emit_x_bf16:
        out_specs.append(pl.BlockSpec((bm, bk), lambda i, j, k: (i, k)))
        out_shape.append(jax.ShapeDtypeStruct((M, K), BF16))
    out_specs += sc_specs
    out_shape += sc_shapes
    assert nk == 1 or list(out_dtypes) == [F32]
    pipelined = [_nbytes((bm, bk), x.dtype), _nbytes((bk, bn), w.dtype)]
    pipelined += [_nbytes(bs, a.dtype) for (a, bs, _) in extras]
    pipelined += [_nbytes((bm, bn), dt) for dt in out_dtypes] + sc_bytes
    resident = _nbytes((bm, bn), F32) * (1 if nk > 1 else 3)
    if emit_x_bf16:
        pipelined.append(_nbytes((bm, bk), BF16))
        resident += _nbytes((bm, bk), BF16)
    semantics = ("arbitrary",) * 3 if (n_sc or emit_x_bf16) else ("parallel", "parallel", "arbitrary")
    outs = pl.pallas_call(
        body,
        grid=(nm, nn, nk),
        in_specs=in_specs,
        out_specs=out_specs,
        out_shape=out_shape,
        compiler_params=pltpu.CompilerParams(
            dimension_semantics=semantics,
            vmem_limit_bytes=_vmem_limit(pipelined, resident)),
        name=name,
    )(x, w, *[a for (a, _, _) in extras], *side_casts)
    return outs


TOKEN_BLOCK = V7X_LANES


def _matmul_nt(wt, x, *, bm, bn, epilogue, out_dtype, extras=(), w_row0=0, n_rows=None, side_casts=(), name):
    K = wt.shape[1]
    N = wt.shape[0] if n_rows is None else n_rows
    M = x.shape[0]
    tb = TOKEN_BLOCK
    assert M % bm == 0 and N % bn == 0 and w_row0 % bn == 0 and bm % tb == 0
    j0 = w_row0 // bn
    nm, nn = M // bm, N // bn
    n_ex, n_sc = len(extras), len(side_casts)

    def body(w_ref, x_ref, *refs):
        ex_refs = refs[:n_ex]
        sc_in = refs[n_ex:n_ex + n_sc]
        o_ref = refs[n_ex + n_sc]
        sc_out = refs[n_ex + n_sc + 1:]
        for src, dst in zip(sc_in, sc_out):
            dst[...] = src[...].astype(dst.dtype)
        acc = lax.dot_general(w_ref[...], x_ref[...], NT_DIMS, preferred_element_type=F32)
        (out,) = epilogue(acc, *[r[...] for r in ex_refs])
        for t in range(bm // tb):
            o_ref[t] = out[:, t * tb:(t + 1) * tb].astype(o_ref.dtype)

    sc_specs, sc_shapes, sc_bytes = _side_cast_specs(side_casts, nm * nn, lambda i, j: i * nn + j)
    pipelined = [_nbytes((bn, K), wt.dtype), _nbytes((bm, K), x.dtype), _nbytes((bn, bm), out_dtype)]
    pipelined += [_nbytes(bs, a.dtype) for (a, bs, _) in extras] + sc_bytes
    outs = pl.pallas_call(
        body,
        grid=(nm, nn),
        in_specs=[pl.BlockSpec((bn, K), lambda i, j: (j0 + j, 0)),
                  pl.BlockSpec((bm, K), lambda i, j: (i, 0))]
        + [pl.BlockSpec(bs, im) for (_, bs, im) in extras] + sc_specs,
        out_specs=[pl.BlockSpec((bm // tb, bn, tb), lambda i, j: (i, j, 0))] + sc_specs,
        out_shape=[jax.ShapeDtypeStruct((M // tb, N, tb), out_dtype)] + sc_shapes,
        compiler_params=pltpu.CompilerParams(
            dimension_semantics=("arbitrary", "arbitrary") if n_sc else ("parallel", "parallel"),
            vmem_limit_bytes=_vmem_limit(pipelined, 3 * _nbytes((bn, bm), F32))),
        name=name,
    )(wt, x, *[a for (a, _, _) in extras], *side_casts)
    return outs


def _ep_scale(scale, acc):
    return (acc * scale,)


def _ep_colscale(acc, scale_row):
    return (acc * scale_row,)


def _ep_sigmoid(acc):
    return (0.5 * jnp.tanh(0.5 * acc) + 0.5,)


def _ep_identity(acc):
    return (acc,)


def _ep_relu_sq(acc):
    r = jnp.maximum(acc, 0.0)
    return (r * r,)


def _ep_rope(acc, cos, sin_signed):
    width = acc.shape[1]
    half = A_HD // 2
    lane = lax.broadcasted_iota(jnp.int32, acc.shape, 1)
    first_half = (lane % A_HD) < half
    partner = jnp.where(first_half, pltpu.roll(acc, width - half, 1), pltpu.roll(acc, half, 1))
    return (acc * cos + partner * sin_signed,)


def _ep_rope_lead(acc, cos, sin_signed):
    w = cos.shape[1]
    (lead,) = _ep_rope(acc[:, :w], cos, sin_signed)
    return (jnp.concatenate([lead, acc[:, w:]], axis=1),)


def _ep_rope_t(acc, cos_head, sin_head):
    rows = acc.shape[0]
    half = A_HD // 2
    row = lax.broadcasted_iota(jnp.int32, acc.shape, 0)
    first_half = (row % A_HD) < half
    partner = jnp.where(first_half, pltpu.roll(acc, rows - half, 0), pltpu.roll(acc, half, 0))
    cos = jnp.concatenate([cos_head] * (rows // A_HD), axis=0)
    sin_signed = jnp.concatenate([sin_head] * (rows // A_HD), axis=0)
    return (acc * cos + partner * sin_signed,)


def _ep_residual(alpha, acc, res):
    return (alpha * res + acc,)


def _ep_residual_ln(alpha, acc, pre, mu_b, rstd_b, g, b):
    return (alpha * _ln_apply(pre, mu_b[:, :1], rstd_b[:, :1], g, b) + acc,)


def _merge(ya, yb, wa, wb, gates, g0_col, g1_col, *, bm, bn):
    M, Ka = ya.shape
    Kb = yb.shape[1]
    N = wa.shape[1]
    assert M % bm == 0 and N % bn == 0 and g0_col % bn == 0 and g1_col % bn == 0
    o0, o1 = g0_col // bn, g1_col // bn

    def body(ya_ref, yb_ref, wa_ref, wb_ref, g0_ref, g1_ref, o_ref):
        a = jnp.dot(ya_ref[...], wa_ref[...], preferred_element_type=F32)
        b = jnp.dot(yb_ref[...], wb_ref[...], preferred_element_type=F32)
        o_ref[...] = (g0_ref[...].astype(F32) * a + g1_ref[...].astype(F32) * b).astype(o_ref.dtype)

    pipelined = [_nbytes((bm, Ka), BF16), _nbytes((bm, Kb), BF16), _nbytes((Ka, bn), BF16),
                 _nbytes((Kb, bn), BF16), 3 * _nbytes((bm, bn), BF16)]
    return pl.pallas_call(
        body,
        grid=(M // bm, N // bn),
        in_specs=[pl.BlockSpec((bm, Ka), lambda i, j: (i, 0)),
                  pl.BlockSpec((bm, Kb), lambda i, j: (i, 0)),
                  pl.BlockSpec((Ka, bn), lambda i, j: (0, j)),
                  pl.BlockSpec((Kb, bn), lambda i, j: (0, j)),
                  pl.BlockSpec((bm, bn), lambda i, j: (i, o0 + j)),
                  pl.BlockSpec((bm, bn), lambda i, j: (i, o1 + j))],
        out_specs=pl.BlockSpec((bm, bn), lambda i, j: (i, j)),
        out_shape=jax.ShapeDtypeStruct((M, N), BF16),
        compiler_params=pltpu.CompilerParams(
            dimension_semantics=("parallel", "parallel"),
            vmem_limit_bytes=_vmem_limit(pipelined, 2 * _nbytes((bm, bn), F32))),
        name="branch_merge",
    )(ya, yb, wa, wb, gates, gates)


def _ln_apply(x, mu, rstd, g, b):
    return (x - mu) * rstd * g + b


def _layer_norm(x, g, b, *, bm, out_dtypes, emit_stats=False):
    M, D = x.shape
    assert M % bm == 0
    n_out = len(out_dtypes)

    def body(x_ref, g_ref, b_ref, *o_refs):
        xf = x_ref[...]
        mu = jnp.mean(xf, axis=-1, keepdims=True)
        xc = xf - mu
        rstd = lax.rsqrt(jnp.mean(xc * xc, axis=-1, keepdims=True) + LN_EPS)
        y = _ln_apply(xf, mu, rstd, g_ref[...], b_ref[...])
        for o_ref in o_refs[:n_out]:
            o_ref[...] = y.astype(o_ref.dtype)
        if emit_stats:
            o_refs[n_out][...] = jnp.broadcast_to(mu, (bm, V7X_LANES))
            o_refs[n_out + 1][...] = jnp.broadcast_to(rstd, (bm, V7X_LANES))

    stat_specs = [pl.BlockSpec((bm, V7X_LANES), lambda i: (i, 0))] * (2 if emit_stats else 0)
    stat_shapes = [jax.ShapeDtypeStruct((M, V7X_LANES), F32)] * (2 if emit_stats else 0)
    pipelined = [_nbytes((bm, D), F32)] + [_nbytes((bm, D), dt) for dt in out_dtypes]
    return pl.pallas_call(
        body,
        grid=(M // bm,),
        in_specs=[pl.BlockSpec((bm, D), lambda i: (i, 0)),
                  pl.BlockSpec((1, D), lambda i: (0, 0)),
                  pl.BlockSpec((1, D), lambda i: (0, 0))],
        out_specs=[pl.BlockSpec((bm, D), lambda i: (i, 0)) for _ in out_dtypes] + stat_specs,
        out_shape=[jax.ShapeDtypeStruct((M, D), dt) for dt in out_dtypes] + stat_shapes,
        compiler_params=pltpu.CompilerParams(
            dimension_semantics=("parallel",),
            vmem_limit_bytes=_vmem_limit(pipelined, 2 * _nbytes((bm, D), F32))),
        name="layer_norm",
    )(x, g.reshape(1, D), b.reshape(1, D))


MLSTM_HEADS_PER_STEP = 4


def _mlstm(qkv, og, gates_t, b_ig, b_fg, w_mnorm, C0, n0, m0, *, L):
    B, S, _ = qkv.shape
    nC = S // L
    nCp = gates_t.shape[2]
    H, dk, dv = M_HEADS, M_DQK, M_DV
    hb = MLSTM_HEADS_PER_STEP
    ng = H // hb
    k_blk0 = M_QK_W // (hb * dk)
    v_blk0 = (2 * M_QK_W) // (hb * dv)

    def body(big_ref, bfg_ref, q_ref, k_ref, v_ref, og_ref, ig_ref, fg_ref, wm_ref, C0_ref, n0_ref, m0_ref,
             ya_ref, C_ref, n_ref, m_ref, b_s, ig_s):
        g = pl.program_id(1)
        r_i = lax.broadcasted_iota(jnp.int32, (L, L), 0)
        c_i = lax.broadcasted_iota(jnp.int32, (L, L), 1)
        tri_incl = (r_i <= c_i).astype(F32)
        for j in range(hb):
            ig_s[j] = ig_ref[0, j] + big_ref[g * hb + j]
            lf_all = jax.nn.log_sigmoid(fg_ref[0, j] + bfg_ref[g * hb + j])
            b_s[j] = jnp.dot(lf_all, tri_incl, precision=lax.Precision.HIGHEST, preferred_element_type=F32)
        C_ref[...] = C0_ref[...]
        n_ref[...] = n0_ref[...]
        m_ref[...] = m0_ref[...]
        eye = r_i == c_i
        causal = c_i <= r_i

        def col_of(row):
            return jnp.sum(jnp.where(eye, jnp.broadcast_to(row, (L, L)), 0.0), axis=1, keepdims=True)

        def head_chunk(j, c, rows):
            q = q_ref[0, rows, j * dk:(j + 1) * dk]
            k = k_ref[0, rows, j * dk:(j + 1) * dk]
            v = v_ref[0, rows, j * dv:(j + 1) * dv]
            b_row = b_s[j, pl.ds(c, 1), :]
            ig_row = ig_s[j, pl.ds(c, 1), :]
            m_prev = m_ref[0, j]
            C_prev = C_ref[0, j]
            n_prev = n_ref[0, j]

            b_col = col_of(b_row)
            dlog = b_col - jnp.broadcast_to(b_row, (L, L)) + jnp.broadcast_to(ig_row, (L, L))
            dlog = jnp.where(causal, dlog, -jnp.inf)
            inter = b_col + m_prev
            m_t = jnp.maximum(inter, jnp.max(dlog, axis=1, keepdims=True))
            qk = lax.dot_general(q, k, NT_DIMS, preferred_element_type=F32)
            s = qk * jnp.exp(dlog - m_t)
            a = jnp.exp(inter - m_t)
            qC = lax.dot_general(q, C_prev.astype(BF16), NT_DIMS, preferred_element_type=F32)
            sv = jnp.dot(s.astype(BF16), v, preferred_element_type=F32)
            num = a * qC + sv
            qn = jnp.sum(q.astype(F32) * n_prev, axis=1, keepdims=True)
            den = a * qn + jnp.sum(s, axis=1, keepdims=True)
            hid = num / jnp.maximum(jnp.abs(den), jnp.exp(-m_t))
            hn = hid * lax.rsqrt(jnp.mean(hid * hid, axis=1, keepdims=True) + RMS_EPS)
            hn = hn * wm_ref[:, j * dv:(j + 1) * dv]
            gate = og_ref[0, rows, j * dv:(j + 1) * dv].astype(F32)
            ya_ref[0, rows, j * dv:(j + 1) * dv] = (gate * hn).astype(ya_ref.dtype)

            bL = b_row[:, L - 1:L]
            g_row = bL - b_row + ig_row
            m_new = jnp.maximum(bL + m_prev, jnp.max(g_row, axis=1, keepdims=True))
            wk_col = col_of(jnp.exp(g_row - m_new))
            decay = jnp.exp(bL + m_prev - m_new)
            kw = k.astype(F32) * wk_col
            vk = lax.dot_general(v, kw.astype(BF16), TN_DIMS, preferred_element_type=F32)
            C_ref[0, j] = decay * C_prev + vk
            n_ref[0, j] = decay * n_prev + jnp.sum(kw, axis=0, keepdims=True)
            m_ref[0, j] = m_new

        def chunk(c, carry):
            rows = pl.ds(pl.multiple_of(c * L, L), L)
            for j in range(hb):
                head_chunk(j, c, rows)
            return carry

        lax.fori_loop(0, nC, chunk, 0)

    smem = pl.BlockSpec(memory_space=pltpu.SMEM)
    pipelined = [2 * _nbytes((S, hb * dk), BF16), 3 * _nbytes((S, hb * dv), BF16),
                 2 * _nbytes((hb, nCp, L), F32), 2 * _nbytes((hb, dv, dk), F32)]
    ya, C, n, m = pl.pallas_call(
        body,
        grid=(B, ng),
        in_specs=[smem, smem,
                  pl.BlockSpec((1, S, hb * dk), lambda b, g: (b, 0, g)),
                  pl.BlockSpec((1, S, hb * dk), lambda b, g: (b, 0, k_blk0 + g)),
                  pl.BlockSpec((1, S, hb * dv), lambda b, g: (b, 0, v_blk0 + g)),
                  pl.BlockSpec((1, S, hb * dv), lambda b, g: (b, 0, g)),
                  pl.BlockSpec((1, hb, nCp, L), lambda b, g: (b, g, 0, 0)),
                  pl.BlockSpec((1, hb, nCp, L), lambda b, g: (b, ng + g, 0, 0)),
                  pl.BlockSpec((1, hb * dv), lambda b, g: (0, g)),
                  pl.BlockSpec((1, hb, dv, dk), lambda b, g: (b, g, 0, 0)),
                  pl.BlockSpec((1, hb, 1, dk), lambda b, g: (b, g, 0, 0)),
                  pl.BlockSpec((1, hb, 1, 1), lambda b, g: (b, g, 0, 0))],
        out_specs=[pl.BlockSpec((1, S, hb * dv), lambda b, g: (b, 0, g)),
                   pl.BlockSpec((1, hb, dv, dk), lambda b, g: (b, g, 0, 0)),
                   pl.BlockSpec((1, hb, 1, dk), lambda b, g: (b, g, 0, 0)),
                   pl.BlockSpec((1, hb, 1, 1), lambda b, g: (b, g, 0, 0))],
        out_shape=[jax.ShapeDtypeStruct((B, S, M_V_W), BF16),
                   jax.ShapeDtypeStruct((B, H, dv, dk), F32),
                   jax.ShapeDtypeStruct((B, H, 1, dk), F32),
                   jax.ShapeDtypeStruct((B, H, 1, 1), F32)],
        scratch_shapes=[pltpu.VMEM((hb, nCp, L), F32), pltpu.VMEM((hb, nCp, L), F32)],
        compiler_params=pltpu.CompilerParams(
            dimension_semantics=("parallel", "parallel"),
            vmem_limit_bytes=_vmem_limit(pipelined, 4 << 20)),
        name="mlstm_chunks",
    )(b_ig, b_fg, qkv, qkv, qkv, og, gates_t, gates_t, w_mnorm.reshape(1, M_V_W), C0, n0, m0)
    return ya, C, n, m


def _mlstm_prompt(qt, k, vt, ogt, gates_p, b_ig, b_fg, wm_b, *, B, S):
    TB = TOKEN_BLOCK
    assert TB == 2 * CHUNK and S % TB == 0
    nP = S // TB
    nPp = gates_p.shape[2]
    H, dk, dv = M_HEADS, M_DQK, M_DV
    hb = MLSTM_HEADS_PER_STEP
    ng = H // hb
    dva = dv + V7X_SUBLANES
    NEG = -jnp.inf

    def body(big_ref, bfg_ref, q_ref, k_ref, v_ref, og_ref, ig_ref, fg_ref, wm_ref,
             ya_ref, C_ref, n_ref, m_ref, caug_s, row_s, u_s):
        g = pl.program_id(1)
        s_i = lax.broadcasted_iota(jnp.int32, (TB, TB), 0)
        t_i = lax.broadcasted_iota(jnp.int32, (TB, TB), 1)
        causal = s_i <= t_i
        same_chunk = (s_i // CHUNK) == (t_i // CHUNK)
        intra = jnp.logical_and(causal, same_chunk)
        eye = s_i == t_i
        tri_chunk = intra.astype(F32)
        lane_p = lax.broadcasted_iota(jnp.int32, (nPp, TB), 1)
        first_p = lane_p < CHUNK
        first_1 = lax.broadcasted_iota(jnp.int32, (1, TB), 1) < CHUNK
        for j in range(hb):
            ig = ig_ref[0, j] + big_ref[g * hb + j]
            lf = jax.nn.log_sigmoid(fg_ref[0, j] + bfg_ref[g * hb + j])
            bc = jnp.dot(lf, tri_chunk, precision=lax.Precision.HIGHEST, preferred_element_type=F32)
            b_a = jnp.broadcast_to(bc[:, CHUNK - 1:CHUNK], (nPp, TB))
            b_b = jnp.broadcast_to(bc[:, TB - 1:TB], (nPp, TB))
            bp = bc + jnp.where(first_p, 0.0, b_a)
            gl = jnp.where(first_p, b_a, b_b) - bc + ig
            gmax_a = jnp.max(jnp.where(first_p, gl, NEG), axis=1, keepdims=True)
            gmax_b = jnp.max(jnp.where(first_p, NEG, gl), axis=1, keepdims=True)
            row_s[j, 0] = bc
            row_s[j, 1] = bp
            row_s[j, 2] = ig - bp
            row_s[j, 6] = b_a + b_b
            row_s[j, 3] = b_a
            row_s[j, 4] = b_b
            row_s[j, 5] = jnp.broadcast_to(gmax_a, (nPp, TB))
            row_s[j, 7] = jnp.broadcast_to(gmax_b, (nPp, TB))
            caug_s[j] = jnp.zeros((dva, dk), F32)

        def scan(p, m_rows):
            nxt = []
            for j in range(hb):
                m0 = m_rows[j]
                r = pl.ds(p, 1)
                m1 = jnp.maximum(row_s[j, 3, r, :] + m0, row_s[j, 5, r, :])
                m2 = jnp.maximum(row_s[j, 4, r, :] + m1, row_s[j, 7, r, :])
                row_s[j, 3, r, :] = jnp.where(first_1, m0, m1)
                row_s[j, 4, r, :] = m0
                row_s[j, 5, r, :] = m2
                u_row = row_s[j, 2, r, :]
                u_col = jnp.sum(jnp.where(eye, jnp.broadcast_to(u_row, (TB, TB)), 0.0), axis=1, keepdims=True)
                u_s[j, p] = jnp.broadcast_to(u_col, (TB, TB))
                nxt.append(m2)
            return tuple(nxt)

        m_fin = lax.fori_loop(0, nP, scan, tuple(jnp.zeros((1, TB), F32) for _ in range(hb)))

        def head_block(j, p, rows):
            kp = k_ref[0, rows, j * dk:(j + 1) * dk]
            q_t = q_ref[p, j * dk:(j + 1) * dk, :]
            v_t = v_ref[p, j * dv:(j + 1) * dv, :]
            r = pl.ds(p, 1)
            bc, bp, u_row = row_s[j, 0, r, :], row_s[j, 1, r, :], row_s[j, 2, r, :]
            m_chunk, m_start, m_end, b_tot = row_s[j, 3, r, :], row_s[j, 4, r, :], row_s[j, 5, r, :], row_s[j, 6, r, :]
            d_t = bp + u_s[j, p]
            m_t = jnp.maximum(bc + m_chunk, jnp.max(jnp.where(intra, d_t, NEG), axis=0, keepdims=True))
            e_t = jnp.exp(jnp.where(causal, d_t, NEG) - m_t)
            p_t = jnp.dot(kp, q_t, preferred_element_type=F32) * e_t
            a = jnp.exp(bp + m_start - m_t)
            caug = caug_s[j]
            cq = jnp.dot(caug.astype(BF16), q_t, preferred_element_type=F32)
            num = a * cq[:dv] + jnp.dot(v_t, p_t.astype(BF16), preferred_element_type=F32)
            den = a * cq[dv:dv + 1] + jnp.sum(p_t, axis=0, keepdims=True)
            hid = num / jnp.maximum(jnp.abs(den), jnp.exp(-m_t))
            hn = hid * lax.rsqrt(jnp.mean(hid * hid, axis=0, keepdims=True) + RMS_EPS)
            hn = hn * wm_ref[j * dv:(j + 1) * dv, :] * og_ref[p, j * dv:(j + 1) * dv, :].astype(F32)
            for c0 in range(0, dv, TB):
                ya_ref[0, rows, j * dv + c0:j * dv + c0 + TB] = hn[c0:c0 + TB].T.astype(ya_ref.dtype)

            wk = jnp.exp(b_tot + u_row - m_end)
            decay = jnp.exp(b_tot + m_start - m_end)
            vw = (v_t.astype(F32) * wk).astype(BF16)
            vw = jnp.concatenate([vw, jnp.broadcast_to(wk, (V7X_SUBLANES, TB)).astype(BF16)], axis=0)
            caug_s[j] = decay * caug + jnp.dot(vw, kp, preferred_element_type=F32)

        def block(p, carry):
            rows = pl.ds(pl.multiple_of(p * TB, TB), TB)
            for j in range(hb):
                head_block(j, p, rows)
            return carry

        lax.fori_loop(0, nP, block, 0)
        for j in range(hb):
            C_ref[0, j] = caug_s[j, :dv, :]
            n_ref[0, j] = caug_s[j, dv:dv + 1, :]
            m_ref[0, j] = m_fin[j][:, :1]

    smem = pl.BlockSpec(memory_space=pltpu.SMEM)
    pipelined = [_nbytes((nP, hb * dk, TB), BF16), _nbytes((S, hb * dk), BF16), 3 * _nbytes((S, hb * dv), BF16),
                 2 * _nbytes((hb, nPp, TB), F32), _nbytes((hb * dv, TB), F32), _nbytes((hb, dv, dk), F32)]
    scratch = [pltpu.VMEM((hb, dva, dk), F32), pltpu.VMEM((hb, 8, nPp, TB), F32), pltpu.VMEM((hb, nP, TB, TB), F32)]
    resident = _nbytes((hb, dva, dk), F32) + _nbytes((hb, 8, nPp, TB), F32) + _nbytes((hb, nP, TB, TB), F32)
    ya, C, n, m = pl.pallas_call(
        body,
        grid=(B, ng),
        in_specs=[smem, smem,
                  pl.BlockSpec((nP, hb * dk, TB), lambda b, g: (b, g, 0)),
                  pl.BlockSpec((1, S, hb * dk), lambda b, g: (b, 0, g)),
                  pl.BlockSpec((nP, hb * dv, TB), lambda b, g: (b, g, 0)),
                  pl.BlockSpec((nP, hb * dv, TB), lambda b, g: (b, g, 0)),
                  pl.BlockSpec((1, hb, nPp, TB), lambda b, g: (b, g, 0, 0)),
                  pl.BlockSpec((1, hb, nPp, TB), lambda b, g: (b, ng + g, 0, 0)),
                  pl.BlockSpec((hb * dv, TB), lambda b, g: (g, 0))],
        out_specs=[pl.BlockSpec((1, S, hb * dv), lambda b, g: (b, 0, g)),
                   pl.BlockSpec((1, hb, dv, dk), lambda b, g: (b, g, 0, 0)),
                   pl.BlockSpec((1, hb, 1, dk), lambda b, g: (b, g, 0, 0)),
                   pl.BlockSpec((1, hb, 1, 1), lambda b, g: (b, g, 0, 0))],
        out_shape=[jax.ShapeDtypeStruct((B, S, M_V_W), BF16),
                   jax.ShapeDtypeStruct((B, H, dv, dk), F32),
                   jax.ShapeDtypeStruct((B, H, 1, dk), F32),
                   jax.ShapeDtypeStruct((B, H, 1, 1), F32)],
        scratch_shapes=scratch,
        compiler_params=pltpu.CompilerParams(
            dimension_semantics=("parallel", "parallel"),
            vmem_limit_bytes=_vmem_limit(pipelined, resident + (4 << 20))),
        name="mlstm_prompt_blocks",
    )(b_ig, b_fg, qt, k, vt, ogt, gates_p, gates_p, wm_b)
    return ya, C, n, m


def _swa_prompt(qt, kv, vt, sink, *, B, S):
    TB = 2 * CHUNK
    assert TB == V7X_LANES and S % TB == 0 and WIN_CHUNKS == 2
    nblk = S // TB
    G, KVH = A_GROUP, A_KV_HEADS

    def body(sink_ref, q_ref, kp_ref, kc_ref, vp_ref, vc_ref, o_ref):
        p_id = pl.program_id(1)
        key_chunk = lax.broadcasted_iota(jnp.int32, (2 * TB, TB), 0) // CHUNK
        q_half = lax.broadcasted_iota(jnp.int32, (2 * TB, TB), 1) // CHUNK
        valid = jnp.logical_and(key_chunk >= q_half, key_chunk <= q_half + WIN_CHUNKS)
        valid = jnp.logical_and(valid, jnp.logical_or(key_chunk >= WIN_CHUNKS, p_id > 0))
        bias = jnp.where(valid, 0.0, -jnp.inf).astype(F32)
        bias = jnp.concatenate([bias] * G, axis=1)
        zeros = jnp.zeros((A_HD, TB), q_ref.dtype)
        for kvh in range(KVH):
            pr = kvh // 2
            lanes = slice(pr * V7X_LANES, (pr + 1) * V7X_LANES)
            kc = jnp.concatenate([kp_ref[:, lanes], kc_ref[:, lanes]], axis=0).astype(BF16)
            rows = slice(kvh * A_HD, (kvh + 1) * A_HD)
            vt_w = jnp.concatenate([vp_ref[0, rows, :], vc_ref[0, rows, :]], axis=1)
            pieces, sinks = [], []
            for g in range(G):
                head = kvh * G + g
                qg = q_ref[0, head * A_HD:(head + 1) * A_HD, :]
                pieces.append(jnp.concatenate([zeros, qg] if kvh % 2 else [qg, zeros], axis=0))
                sinks.append(jnp.full((1, TB), sink_ref[head] * LOG2_E, F32))
            qz = jnp.concatenate(pieces, axis=1)
            sk = jnp.concatenate(sinks, axis=1)
            s_t = jnp.dot(kc, qz, preferred_element_type=F32) + bias
            mx = jnp.maximum(jnp.max(s_t, axis=0, keepdims=True), sk)
            p_t = jnp.exp2(s_t - mx)
            den = jnp.sum(p_t, axis=0, keepdims=True) + jnp.exp2(sk - mx)
            o_t = jnp.dot(vt_w, p_t.astype(BF16), preferred_element_type=F32) / den
            for j in range(G // 2):
                two = jnp.concatenate([o_t[:, (2 * j) * TB:(2 * j + 1) * TB],
                                       o_t[:, (2 * j + 1) * TB:(2 * j + 2) * TB]], axis=0)
                col0 = (kvh * G + 2 * j) * A_HD
                o_ref[:, col0:col0 + V7X_LANES] = two.T.astype(o_ref.dtype)

    prev = lambda b, p: b * nblk + jnp.maximum(p - 1, 0)
    cur = lambda b, p: b * nblk + p
    pipelined = [2 * _nbytes((A_Q_W, TB), BF16), 2 * _nbytes((TB, A_KV_W), F32), 2 * _nbytes((A_KV_W, TB), BF16)]
    return pl.pallas_call(
        body,
        grid=(B, nblk),
        in_specs=[pl.BlockSpec(memory_space=pltpu.SMEM),
                  pl.BlockSpec((1, A_Q_W, TB), lambda b, p: (cur(b, p), 0, 0)),
                  pl.BlockSpec((TB, A_KV_W), lambda b, p: (prev(b, p), 0)),
                  pl.BlockSpec((TB, A_KV_W), lambda b, p: (cur(b, p), 0)),
                  pl.BlockSpec((1, A_KV_W, TB), lambda b, p: (prev(b, p), 0, 0)),
                  pl.BlockSpec((1, A_KV_W, TB), lambda b, p: (cur(b, p), 0, 0))],
        out_specs=pl.BlockSpec((TB, A_Q_W), lambda b, p: (cur(b, p), 0)),
        out_shape=jax.ShapeDtypeStruct((B * S, A_Q_W), BF16),
        compiler_params=pltpu.CompilerParams(
            dimension_semantics=("parallel", "arbitrary"),
            vmem_limit_bytes=_vmem_limit(pipelined, 16 << 20)),
        name="swa_prompt_attention",
    )(sink, qt, kv, kv, vt, vt)


def _swa_sample(q, kv, sink):
    B, T, _ = q.shape
    Skv = kv.shape[1]
    G, KVH, LN = A_GROUP, A_KV_HEADS, V7X_LANES
    assert 2 * A_HD == LN and T % 16 == 0

    def body(sink_ref, q_ref, kv_ref, o_ref):
        kvw = kv_ref[0]
        lo_q = lax.broadcasted_iota(jnp.int32, (T, LN), 1) < A_HD
        lo_k = lax.broadcasted_iota(jnp.int32, (Skv, LN), 1) < A_HD
        zero = jnp.zeros((T, LN), q_ref.dtype)
        for kvh in range(KVH):
            pair, odd = kvh // 2, kvh % 2
            kx = kvw[:, pair * LN:(pair + 1) * LN]
            vx = kvw[:, A_KV_W + pair * LN:A_KV_W + (pair + 1) * LN]
            kr = pltpu.roll(kx, A_HD, 1)
            vr = pltpu.roll(vx, A_HD, 1)
            k2 = (jnp.where(lo_k, kr, kx) if odd else jnp.where(lo_k, kx, kr)).astype(BF16)
            v2 = (jnp.where(lo_k, vr, vx) if odd else jnp.where(lo_k, vx, vr)).astype(BF16)
            pieces, sinks = [], []
            for g in range(G):
                head = kvh * G + g
                q2 = q_ref[0, :, (head // 2) * LN:(head // 2 + 1) * LN]
                pieces.append(jnp.where(lo_q, zero, q2) if head % 2 else jnp.where(lo_q, q2, zero))
                sinks.append(jnp.full((T, 1), sink_ref[head], F32))
            qs = jnp.concatenate(pieces, axis=0)
            sk = jnp.concatenate(sinks, axis=0)
            s = lax.dot_general(qs, k2, NT_DIMS, preferred_element_type=F32)
            mx = jnp.maximum(jnp.max(s, axis=1, keepdims=True), sk)
            p = jnp.exp(s - mx)
            den = jnp.sum(p, axis=1, keepdims=True) + jnp.exp(sk - mx)
            o = jnp.dot(p.astype(BF16), v2, preferred_element_type=F32) / den
            for j in range(G // 2):
                pair_o = jnp.where(lo_q, o[(2 * j) * T:(2 * j + 1) * T], o[(2 * j + 1) * T:(2 * j + 2) * T])
                col0 = (kvh * G + 2 * j) * A_HD
                o_ref[0, :, col0:col0 + LN] = pair_o.astype(o_ref.dtype)

    pipelined = [2 * _nbytes((T, A_Q_W), BF16), _nbytes((Skv, 2 * A_KV_W), F32)]
    return pl.pallas_call(
        body,
        grid=(B,),
        in_specs=[pl.BlockSpec(memory_space=pltpu.SMEM),
                  pl.BlockSpec((1, T, A_Q_W), lambda b: (b, 0, 0)),
                  pl.BlockSpec((1, Skv, 2 * A_KV_W), lambda b: (b, 0, 0))],
        out_specs=pl.BlockSpec((1, T, A_Q_W), lambda b: (b, 0, 0)),
        out_shape=jax.ShapeDtypeStruct((B, T, A_Q_W), BF16),
        compiler_params=pltpu.CompilerParams(
            dimension_semantics=("parallel",),
            vmem_limit_bytes=_vmem_limit(pipelined, 8 << 20)),
        name="swa_sample_attention",
    )(sink, q, kv)


def _rope_tables(pos, n_rot_heads, n_plain_cols, scale):
    half = A_HD // 2
    inv = ROPE_THETA ** (-jnp.arange(half, dtype=F32) / half)
    ang = pos.astype(F32)[:, None] * inv[None, :]
    cos, sin = jnp.cos(ang) * scale, jnp.sin(ang) * scale
    cos_h = jnp.concatenate([cos, cos], -1)
    sin_h = jnp.concatenate([-sin, sin], -1)
    n = pos.shape[0]
    cos_t = jnp.concatenate([jnp.tile(cos_h, (1, n_rot_heads)), jnp.ones((n, n_plain_cols), F32)], -1)
    sin_t = jnp.concatenate([jnp.tile(sin_h, (1, n_rot_heads)), jnp.zeros((n, n_plain_cols), F32)], -1)
    return cos_t, sin_t


W_IN_BLOCK = 512


def _in_proj_layout(d_model):
    widths = [("mqkv", 2 * M_QK_W + M_V_W), ("sig", M_V_W + N_BRANCH * d_model), ("aq", A_Q_W),
              ("kv", 2 * A_KV_W), ("gate", W_IN_BLOCK)]
    off, col = {}, 0
    for name, w in widths:
        off[name] = col
        col += w
    off["end"] = col
    return off


def _regroup_w_in_t(w_in_t):
    D = w_in_t.shape[1]
    off = _in_proj_layout(D)
    bw = W_IN_BLOCK
    n_gate = 2 * M_HEADS
    src_gate = 2 * M_QK_W + 2 * M_V_W
    src_aq = src_gate + n_gate
    src_gp = src_aq + A_Q_W + 2 * A_KV_W
    assert src_gate % bw == 0 and off["sig"] % bw == 0 and (N_BRANCH * D) % bw == 0 and n_gate <= bw
    assert src_aq % V7X_SUBLANES == 0
    j_gp = (off["sig"] + M_V_W) // bw
    j_aq = off["aq"] // bw
    j_gate = off["gate"] // bw

    sl = V7X_SUBLANES

    def src_row(j):
        tile = jnp.where(j < j_gp, j * (bw // sl),
                         jnp.where(j < j_aq, src_gp // sl + (j - j_gp) * (bw // sl),
                                   jnp.where(j < j_gate, src_aq // sl + (j - j_aq) * (bw // sl), src_gate // sl)))
        return tile * sl

    def body(a_ref, o_ref):
        j = pl.program_id(0)
        a = a_ref[...]
        row = lax.broadcasted_iota(jnp.int32, a.shape, 0)
        keep = jnp.logical_or(j < j_gate, row < n_gate)
        o_ref[...] = jnp.where(keep, a, 0.0).astype(o_ref.dtype)

    pipelined = [_nbytes((bw, D), F32), _nbytes((bw, D), BF16)]
    return pl.pallas_call(
        body,
        grid=(off["end"] // bw,),
        in_specs=[pl.BlockSpec((pl.Element(bw), pl.Element(D)), lambda j: (src_row(j), 0))],
        out_specs=pl.BlockSpec((bw, D), lambda j: (j, 0)),
        out_shape=jax.ShapeDtypeStruct((off["end"], D), BF16),
        compiler_params=pltpu.CompilerParams(
            dimension_semantics=("parallel",),
            vmem_limit_bytes=_vmem_limit(pipelined, 2 * _nbytes((bw, D), F32))),
        name="regroup_w_in",
    )(w_in_t)


DENSE_WEIGHTS = ("br_a", "br_b", "out", "up", "down")


def _layer(x, pos, W, b_ig, b_fg, w_mnorm, sink, ln1_g, ln1_b, ln2_g, ln2_b, alpha, state, cache, *, bm):
    B, S, D = x.shape
    M = B * S
    L = min(CHUNK, S)
    xf = x.reshape(M, D)
    bn_big = min(1024, D)
    off = _in_proj_layout(D)
    w_in_t = W["w_in_t"]
    Wb = {name: W[name] for name in DENSE_WEIGHTS if W[name].dtype == BF16}

    def pos_extra(table, rows, width):
        if rows <= S:
            per = S // rows
            return (table, (rows, width), lambda i, j, k: (i % per, j))
        return (jnp.tile(table, (rows // S, 1)), (rows, width), lambda i, j, k: (0, j))

    q_scale_m = M_DQK ** -0.5
    wkv = 2 * A_KV_W
    bm0 = min(256, M)
    n_first = wkv + 2 * V7X_LANES
    cos_k, sin_k = _rope_tables(pos, A_KV_HEADS, 0, 1.0)
    kvg, xb = _matmul(xf, w_in_t, w_nk=True, bm=bm0, bn=n_first, bk=D, epilogue=_ep_rope_lead, out_dtypes=[F32],
                      extras=[pos_extra(cos_k, bm0, A_KV_W), pos_extra(sin_k, bm0, A_KV_W)],
                      w_col0=off["kv"], n_cols=n_first, emit_x_bf16=True, name="in_proj_attn_kv_if_gates")
    gates = kvg[:, wkv:wkv + 2 * M_HEADS]

    def gate_rows(lanes):
        n = S // lanes
        t = gates.reshape(B, n, lanes, 2 * M_HEADS).transpose(0, 3, 1, 2)
        return jnp.pad(t, ((0, 0), (0, 0), (0, max(n, V7X_SUBLANES) - n), (0, 0)))

    def side(name):
        return [] if name in Wb else [W[name]]

    def keep(name, casts):
        if casts:
            Wb[name] = casts[0]

    o_mo, o_gp = off["sig"], off["sig"] + M_V_W
    if state is None:
        mk, *c = _matmul(xb, w_in_t, w_nk=True, bm=bm, bn=bn_big, bk=D, epilogue=_ep_identity, out_dtypes=[BF16],
                         w_col0=M_QK_W, n_cols=M_QK_W, side_casts=side("br_a"), name="in_proj_mlstm_k")
        keep("br_a", c)
        bn_t = min(1024, M_QK_W)
        mq_t, *c = _matmul_nt(w_in_t, xb, bm=bm, bn=bn_t, epilogue=functools.partial(_ep_scale, q_scale_m),
                              out_dtype=BF16, w_row0=0, n_rows=M_QK_W, side_casts=side("br_b"),
                              name="in_proj_mlstm_q_t")
        keep("br_b", c)
        mv_t, *c = _matmul_nt(w_in_t, xb, bm=bm, bn=bn_t, epilogue=_ep_identity, out_dtype=BF16,
                              w_row0=2 * M_QK_W, n_rows=M_V_W, side_casts=side("out"),
                              name="in_proj_mlstm_v_t")
        keep("out", c)
        (og_t,) = _matmul_nt(w_in_t, xb, bm=bm, bn=bn_t, epilogue=_ep_sigmoid, out_dtype=BF16,
                             w_row0=o_mo, n_rows=M_V_W, name="in_proj_mlstm_o_t")
        br_gates, *c = _matmul(xb, w_in_t, w_nk=True, bm=bm, bn=bn_big, bk=D, epilogue=_ep_sigmoid,
                               out_dtypes=[BF16], w_col0=o_gp, n_cols=N_BRANCH * D, side_casts=side("up"),
                               name="in_proj_branch_gates")
        keep("up", c)
        g_cols = (0, D)
        wm_b = jnp.broadcast_to(w_mnorm[:, None], (M_V_W, TOKEN_BLOCK))
        ya, C, n_st, m = _mlstm_prompt(mq_t, mk.reshape(B, S, M_QK_W), mv_t, og_t, gate_rows(TOKEN_BLOCK),
                                       b_ig, b_fg, wm_b, B=B, S=S)
    else:
        n_qkv = 2 * M_QK_W + M_V_W
        colscale = jnp.concatenate([jnp.full((1, M_QK_W), q_scale_m, F32), jnp.ones((1, M_QK_W + M_V_W), F32)], 1)
        (qkv,) = _matmul(xb, w_in_t, w_nk=True, bm=bm, bn=bn_big, bk=D, epilogue=_ep_colscale, out_dtypes=[BF16],
                         extras=[(colscale, (1, bn_big), lambda i, j, k: (0, j))],
                         w_col0=off["mqkv"], n_cols=n_qkv, name="in_proj_mlstm_qkv")
        (br_gates,) = _matmul(xb, w_in_t, w_nk=True, bm=bm, bn=bn_big, bk=D, epilogue=_ep_sigmoid,
                              out_dtypes=[BF16], w_col0=o_mo, n_cols=M_V_W + N_BRANCH * D,
                              name="in_proj_sigmoid_gates")
        g_cols = (M_V_W, M_V_W + D)
        C0 = state[0].astype(F32)
        n0 = state[1].astype(F32).reshape(B, M_HEADS, 1, M_DQK)
        m0 = state[2].astype(F32).reshape(B, M_HEADS, 1, 1)
        ya, C, n_st, m = _mlstm(qkv.reshape(B, S, -1), br_gates.reshape(B, S, -1), gate_rows(L), b_ig, b_fg,
                                w_mnorm, C0, n0, m0, L=L)

    kv3 = kvg.reshape(B, S, n_first)
    k_new = kv3[:, :, :A_KV_W].reshape(B, S, A_KV_HEADS, A_HD)
    v_new = kv3[:, :, A_KV_W:wkv].reshape(B, S, A_KV_HEADS, A_HD)
    q_scale = A_HD ** -0.5
    wq = 512
    if cache is None:
        cos_q, sin_q = _rope_tables(pos, 1, 0, q_scale * LOG2_E)
        if bm <= S:
            tab = lambda t: (t.T, (A_HD, bm), lambda i, j: (0, i % (S // bm)))
        else:
            tab = lambda t: (jnp.tile(t.T, (1, bm // S)), (A_HD, bm), lambda i, j: (0, 0))
        (qt,) = _matmul_nt(w_in_t, xb, bm=bm, bn=min(1024, A_Q_W), epilogue=_ep_rope_t, out_dtype=BF16,
                           extras=[tab(cos_q), tab(sin_q)], w_row0=off["aq"], n_rows=A_Q_W,
                           name="in_proj_attn_q_t")
        (vt,) = _matmul_nt(w_in_t, xb, bm=bm, bn=A_KV_W, epilogue=_ep_identity, out_dtype=BF16,
                           w_row0=off["kv"] + A_KV_W, n_rows=A_KV_W, name="in_proj_attn_v_t")
        yb = _swa_prompt(qt, kvg, vt, sink, B=B, S=S)
    else:
        cos_q, sin_q = _rope_tables(pos, A_HEADS, 0, q_scale)
        (aq,) = _matmul(xb, w_in_t, w_nk=True, bm=bm, bn=wq, bk=D, epilogue=_ep_rope, out_dtypes=[BF16],
                        extras=[pos_extra(cos_q, bm, wq), pos_extra(sin_q, bm, wq)],
                        w_col0=off["aq"], n_cols=A_Q_W, name="in_proj_attn_q")
        ck = cache[0].astype(F32).reshape(B, WINDOW, A_KV_W)
        cv = cache[1].astype(F32).reshape(B, WINDOW, A_KV_W)
        kv_all = jnp.concatenate([jnp.concatenate([ck, cv], -1), kv3[:, :, :wkv]], axis=1)
        yb = _swa_sample(aq.reshape(B, S, A_Q_W), kv_all, sink).reshape(M, A_Q_W)

    for name in DENSE_WEIGHTS[:3]:
        Wb.setdefault(name, W[name].astype(BF16))
    merged = _merge(ya.reshape(M, M_V_W), yb, Wb["br_a"], Wb["br_b"], br_gates,
                    g_cols[0], g_cols[1], bm=bm, bn=bn_big)
    res1 = (xf, (bm, bn_big), lambda i, j, k: (i, j))
    (h_pre,) = _matmul(merged, Wb["out"], bm=bm, bn=bn_big, bk=D, epilogue=functools.partial(_ep_residual, alpha),
                       out_dtypes=[F32], extras=[res1], name="out_proj_residual")
    bm_ln = min(512, M)
    h16, mu_b, rstd_b = _layer_norm(h_pre, ln1_g, ln1_b, bm=bm_ln, out_dtypes=[BF16], emit_stats=True)
    Wb.setdefault("up", W["up"].astype(BF16))
    act, *c = _matmul(h16, Wb["up"], bm=bm, bn=bn_big, bk=D, epilogue=_ep_relu_sq, out_dtypes=[BF16],
                      side_casts=side("down"), name="mlp_up_relu_sq")
    keep("down", c)
    stat = lambda a: (a, (bm, V7X_LANES), lambda i, j, k: (i, 0))
    row = lambda v: (v.reshape(1, D), (1, bn_big), lambda i, j, k: (0, j))
    res2 = [(h_pre, (bm, bn_big), lambda i, j, k: (i, j)), stat(mu_b), stat(rstd_b), row(ln1_g), row(ln1_b)]
    (y_pre,) = _matmul(act, Wb["down"], bm=bm, bn=bn_big, bk=min(4096, act.shape[1]),
                       epilogue=functools.partial(_ep_residual_ln, alpha), out_dtypes=[F32], extras=res2,
                       name="mlp_down_residual")
    (y,) = _layer_norm(y_pre, ln2_g, ln2_b, bm=bm_ln, out_dtypes=[F32])
    return (y.reshape(B, S, D), k_new, v_new, C, n_st.reshape(B, M_HEADS, M_DQK), m.reshape(B, M_HEADS), Wb)


def kernel(x_prompt, x_sample, cache_swa_k, cache_swa_v, state_mlstm_C, state_mlstm_n, state_mlstm_m,
           w_in, b_igate, b_fgate, w_mnorm, attn_sink, w_branch_a, w_branch_b, w_out,
           ln1_g, ln1_b, w_up, w_down, ln2_g, ln2_b):
    depth = w_in.shape[0]
    alpha = (2.0 * depth) ** 0.25
    S, T = x_prompt.shape[1], x_sample.shape[1]
    pos_p = jnp.arange(S, dtype=jnp.int32)
    pos_s = PAST_LEN + jnp.arange(T, dtype=jnp.int32)
    xp, xs = x_prompt, x_sample
    outs_p, outs_s = [], []
    for l in range(depth):
        W = dict(w_in_t=_regroup_w_in_t(w_in[l].T), br_a=w_branch_a[l], br_b=w_branch_b[l], out=w_out[l],
                 up=w_up[l], down=w_down[l])
        shared = (b_igate[l], b_fgate[l], w_mnorm[l], attn_sink[l], ln1_g[l], ln1_b[l], ln2_g[l], ln2_b[l], alpha)
        xp, *rest_p, Wb = _layer(xp, pos_p, W, *shared, None, None, bm=min(1024, xp.shape[0] * S))
        outs_p.append(rest_p)
        xs, *rest_s, _ = _layer(xs, pos_s, dict(W, **Wb), *shared,
                                (state_mlstm_C[l], state_mlstm_n[l], state_mlstm_m[l]),
                                (cache_swa_k[l], cache_swa_v[l]), bm=xs.shape[0] * T)
        outs_s.append(rest_s)
    stack = lambda outs, i: jnp.stack([o[i] for o in outs])
    pk = jnp.stack([o[0][:, -WINDOW:] for o in outs_p])
    pv = jnp.stack([o[1][:, -WINDOW:] for o in outs_p])
    return (xp, xs, pk, pv, stack(outs_p, 2), stack(outs_p, 3), stack(outs_p, 4),
            stack(outs_s, 0), stack(outs_s, 1), stack(outs_s, 2), stack(outs_s, 3), stack(outs_s, 4))
```
